```python
import jax, jax.numpy as jnp
from jax import lax
import numpy as np


D_MODEL = 1024
BATCH = 8
SEQ = 2048
DEPTH = 1
DEC_BATCH = 32
DEC_SEQ = 4
PAST_LEN = 16384
PAGE_SIZE = 128

D_MIX = D_MODEL
D_RET = D_MIX // 2
D_MOBA = D_MIX - D_RET
RET_HEADS = 4
RET_DK = D_RET // RET_HEADS
RET_DV = D_RET // RET_HEADS
RET_CHUNK = 128
MOBA_HEADS = 8
MOBA_HD = D_MOBA // MOBA_HEADS
MOBA_BLOCK = 256
MOBA_TOPK = 3
MOBA_Q_BLOCK = 64
N_EXPERTS = 32
TOP_K = 4
D_FF = D_MODEL
SWIGLU_LIMIT = 7.0
SWIGLU_ALPHA = 1.702
ROPE_BASE = 10000.0
EPS = 1e-5
D_IN = 4 * D_RET + 3 * D_MOBA
SPLIT_POINTS = (D_RET, 2 * D_RET, 3 * D_RET, 4 * D_RET, 4 * D_RET + D_MOBA, 4 * D_RET + 2 * D_MOBA)
NEG = -1e30

kernel_name = 'hymba_retention_moba_moe_step'


def rmsnorm(x, g):
    xf = x.astype(jnp.float32)
    return xf * lax.rsqrt(jnp.mean(jnp.square(xf), axis=-1, keepdims=True) + EPS) * g


def rotary(x, pos):
    half = x.shape[-1] // 2
    inv = ROPE_BASE ** (-jnp.arange(half, dtype=jnp.float32) / half)
    ang = pos.astype(jnp.float32)[:, None] * inv[None, :]
    c = jnp.cos(ang)[:, None, :]
    s = jnp.sin(ang)[:, None, :]
    x1, x2 = x[..., :half], x[..., half:]
    return jnp.concatenate([x1 * c - x2 * s, x1 * s + x2 * c], axis=-1)


def project(x, norm_g, w_in, pos):
    b, l, _ = x.shape
    z = jnp.einsum('bld,de->ble', rmsnorm(x, norm_g), w_in).astype(jnp.float32)
    rq, rk, rv, rg, mq, mk, mv = jnp.split(z, SPLIT_POINTS, axis=-1)
    rq = rotary(rq.reshape(b, l, RET_HEADS, RET_DK), pos)
    rk = rotary(rk.reshape(b, l, RET_HEADS, RET_DK), pos) * RET_DK ** -0.5
    rv = rv.reshape(b, l, RET_HEADS, RET_DV)
    mq = mq.reshape(b, l, MOBA_HEADS, MOBA_HD)
    mk = mk.reshape(b, l, MOBA_HEADS, MOBA_HD)
    mv = mv.reshape(b, l, MOBA_HEADS, MOBA_HD)
    return rq, rk, rv, rg, mq, mk, mv


def retention_chunked(q, k, v, s0, chunk):
    b, l, h, dk = q.shape
    dv = v.shape[-1]
    nc = l // chunk
    log_g = jnp.log1p(-jnp.exp2(-5.0 - jnp.arange(h, dtype=jnp.float32)))
    n = jnp.arange(chunk, dtype=jnp.float32)
    rel = n[:, None] - n[None, :]
    causal = rel >= 0
    decay_in = jnp.where(causal[None], jnp.exp(jnp.where(causal, rel, 0.0)[None] * log_g[:, None, None]), 0.0)
    q_dec = jnp.exp((n[:, None] + 1.0) * log_g[None, :])
    k_dec = jnp.exp((chunk - 1.0 - n[:, None]) * log_g[None, :])
    chunk_dec = jnp.exp(chunk * log_g)

    def to_chunks(t):
        return t.reshape(b, nc, chunk, h, t.shape[-1]).swapaxes(0, 1)

    def step(s, xs):
        qc, kc, vc = xs
        scores = jnp.einsum('bnhd,bmhd->bhnm', qc, kc) * decay_in[None]
        inner = jnp.einsum('bhnm,bmhv->bnhv', scores, vc)
        cross = jnp.einsum('bnhd,bhdv->bnhv', qc * q_dec[None, :, :, None], s)
        s_new = chunk_dec[None, :, None, None] * s + jnp.einsum('bmhd,bmhv->bhdv', kc * k_dec[None, :, :, None], vc)
        return s_new, inner + cross

    s_final, o = lax.scan(step, s0, (to_chunks(q), to_chunks(k), to_chunks(v)))
    return o.swapaxes(0, 1).reshape(b, l, h, dv), s_final


def moba_attend_seq(q, k_all, v_all, q_pos):
    l, h, hd = q.shape
    t = k_all.shape[0]
    nb = -(-t // MOBA_BLOCK)
    pad = nb * MOBA_BLOCK - t
    kb = jnp.pad(k_all, ((0, pad), (0, 0), (0, 0))).reshape(nb, MOBA_BLOCK, h, hd).transpose(2, 0, 1, 3)
    vb = jnp.pad(v_all, ((0, pad), (0, 0), (0, 0))).reshape(nb, MOBA_BLOCK, h, hd).transpose(2, 0, 1, 3)
    k_mean = jnp.mean(kb.astype(jnp.float32), axis=2)
    k_sel = min(MOBA_TOPK, nb)
    qb = MOBA_Q_BLOCK if l % MOBA_Q_BLOCK == 0 else l
    blk_ids = jnp.arange(nb)
    offs = jnp.arange(MOBA_BLOCK)
    h_idx = jnp.arange(h)[None, :, None]
    scale = hd ** -0.5

    def one_block(args):
        qblk, pblk = args
        qf = qblk.astype(jnp.float32)
        cur = pblk // MOBA_BLOCK
        past = blk_ids[None, :] < cur[:, None]
        gate = jnp.einsum('qhd,hnd->qhn', qf, k_mean)
        gate = jnp.where(past[:, None, :], gate, NEG)
        _, top_i = lax.top_k(gate, k_sel)
        top_ok = jnp.take_along_axis(jnp.broadcast_to(past[:, None, :], gate.shape), top_i, axis=-1)
        idx = jnp.concatenate([top_i, jnp.broadcast_to(cur[:, None, None], (qb, h, 1))], axis=-1)
        ok = jnp.concatenate([top_ok, jnp.ones((qb, h, 1), dtype=bool)], axis=-1)
        kg = kb[h_idx, idx].astype(jnp.float32)
        vg = vb[h_idx, idx].astype(jnp.float32)
        key_pos = idx[..., None] * MOBA_BLOCK + offs
        mask = ok[..., None] & (key_pos <= pblk[:, None, None, None])
        s = jnp.einsum('qhd,qhjkd->qhjk', qf, kg) * scale
        s = jnp.where(mask, s, -jnp.inf)
        p = jax.nn.softmax(s.reshape(qb, h, -1), axis=-1).reshape(s.shape)
        return jnp.einsum('qhjk,qhjkd->qhd', p, vg)

    out = lax.map(one_block, (q.reshape(l // qb, qb, h, hd), q_pos.reshape(l // qb, qb)))
    return out.reshape(l, h, hd)


def moe(h, router_w, router_b, w_gu, b_gu, w_d, b_d):
    tok = h.reshape(-1, D_MODEL).astype(jnp.float32)
    logits = tok @ router_w + router_b
    top_v, top_i = lax.top_k(logits, TOP_K)
    top_w = jax.nn.softmax(top_v, axis=-1)
    gates = jnp.einsum('tk,tke->te', top_w, jax.nn.one_hot(top_i, N_EXPERTS, dtype=jnp.float32))

    def expert(acc, xs):
        wgu, bgu, wd, bd, g = xs
        u = tok @ wgu + bgu
        glu = jnp.minimum(u[:, ::2], SWIGLU_LIMIT)
        lin = jnp.clip(u[:, 1::2], -SWIGLU_LIMIT, SWIGLU_LIMIT)
        a = glu * jax.nn.sigmoid(SWIGLU_ALPHA * glu) * (lin + 1.0)
        return acc + g[:, None] * (a @ wd + bd), None

    acc, _ = lax.scan(expert, jnp.zeros_like(tok), (w_gu, b_gu, w_d, b_d, gates.T))
    return acc


def merge_and_channel_mix(x, rg, ret_o, moba_o, ret_norm_g, w_out, norm2_g, router_w, router_b, w_gu, b_gu, w_d, b_d):
    b, l, _ = x.shape
    mu = jnp.mean(ret_o, axis=-1, keepdims=True)
    var = jnp.mean(jnp.square(ret_o - mu), axis=-1, keepdims=True)
    ret = ((ret_o - mu) * lax.rsqrt(var + EPS)).reshape(b, l, D_RET) * ret_norm_g * jax.nn.silu(rg)
    mix = jnp.concatenate([ret, moba_o.reshape(b, l, D_MOBA)], axis=-1)
    hres = x.astype(jnp.float32) + jnp.einsum('ble,ed->bld', mix, w_out)
    return hres + moe(rmsnorm(hres, norm2_g), router_w, router_b, w_gu, b_gu, w_d, b_d).reshape(b, l, D_MODEL)


def setup_inputs(seed: int = 0) -> dict:
    key = jax.random.key(seed)
    ks = jax.random.split(key, 18)
    f32 = jnp.float32
    n_pages = PAST_LEN // PAGE_SIZE
    n_used = DEC_BATCH * n_pages
    n_phys = n_used + n_used // 4

    def nrm(k, shape, s):
        return jax.random.normal(k, shape, f32) * s

    return {
        'x_prompt': nrm(ks[0], (BATCH, SEQ, D_MODEL), 1.0),
        'x_sample': nrm(ks[1], (DEC_BATCH, DEC_SEQ, D_MODEL), 1.0),
        'cache_k': nrm(ks[2], (DEPTH, n_phys, PAGE_SIZE, MOBA_HEADS, MOBA_HD), 1.0),
        'cache_v': nrm(ks[3], (DEPTH, n_phys, PAGE_SIZE, MOBA_HEADS, MOBA_HD), 1.0),
        'state_ret': nrm(ks[4], (DEPTH, DEC_BATCH, RET_HEADS, RET_DK, RET_DV), 0.3),
        'page_table': jax.random.permutation(ks[5], n_phys)[:n_used].reshape(DEC_BATCH, n_pages).astype(jnp.int32),
        'norm1_g': 1.0 + nrm(ks[6], (DEPTH, D_MODEL), 0.02),
        'w_in': nrm(ks[7], (DEPTH, D_MODEL, D_IN), D_MODEL ** -0.5),
        'ret_norm_g': 1.0 + nrm(ks[8], (DEPTH, D_RET), 0.02),
        'w_out': nrm(ks[9], (DEPTH, D_MIX, D_MODEL), D_MIX ** -0.5),
        'norm2_g': 1.0 + nrm(ks[10], (DEPTH, D_MODEL), 0.02),
        'router_w': nrm(ks[11], (DEPTH, D_MODEL, N_EXPERTS), D_MODEL ** -0.5),
        'router_b': nrm(ks[12], (DEPTH, N_EXPERTS), 0.01),
        'w_gate_up': nrm(ks[13], (DEPTH, N_EXPERTS, D_MODEL, 2 * D_FF), D_MODEL ** -0.5),
        'b_gate_up': nrm(ks[14], (DEPTH, N_EXPERTS, 2 * D_FF), 0.01),
        'w_down': nrm(ks[15], (DEPTH, N_EXPERTS, D_FF, D_MODEL), D_FF ** -0.5),
        'b_down': nrm(ks[16], (DEPTH, N_EXPERTS, D_MODEL), 0.01),
        'final_norm_g': 1.0 + nrm(ks[17], (D_MODEL,), 0.02),
    }


def reference(x_prompt, x_sample, cache_k, cache_v, state_ret, page_table, norm1_g, w_in, ret_norm_g, w_out, norm2_g, router_w, router_b, w_gate_up, b_gate_up, w_down, b_down, final_norm_g):
    bp, lp, _ = x_prompt.shape
    bs, ls, _ = x_sample.shape
    past_len = page_table.shape[1] * PAGE_SIZE
    pos_p = jnp.arange(lp, dtype=jnp.int32)
    pos_s = past_len + jnp.arange(ls, dtype=jnp.int32)
    chunk_s = RET_CHUNK if ls % RET_CHUNK == 0 else ls
    yp, ys = x_prompt, x_sample
    kp_l, vp_l, sp_l, ks_l, vs_l, ss_l = [], [], [], [], [], []
    for layer in range(DEPTH):
        ffn_args = (ret_norm_g[layer], w_out[layer], norm2_g[layer], router_w[layer], router_b[layer],
                    w_gate_up[layer], b_gate_up[layer], w_down[layer], b_down[layer])
        rq, rk, rv, rg, mq, mk, mv = project(yp, norm1_g[layer], w_in[layer], pos_p)
        ret_o, s_p = retention_chunked(rq, rk, rv, jnp.zeros((bp, RET_HEADS, RET_DK, RET_DV), jnp.float32), RET_CHUNK)
        moba_o = lax.map(lambda a: moba_attend_seq(a[0], a[1], a[2], pos_p), (mq, mk, mv))
        yp = merge_and_channel_mix(yp, rg, ret_o, moba_o, *ffn_args)
        kp_l.append(mk)
        vp_l.append(mv)
        sp_l.append(s_p)
        rq, rk, rv, rg, mq, mk, mv = project(ys, norm1_g[layer], w_in[layer], pos_s)
        ret_o, s_s = retention_chunked(rq, rk, rv, state_ret[layer].astype(jnp.float32), chunk_s)
        ck = cache_k[layer]
        cv = cache_v[layer]

        def sample_seq(a, ck=ck, cv=cv):
            q, kn, vn, pages = a
            k_past = ck[pages].reshape(past_len, MOBA_HEADS, MOBA_HD).astype(jnp.float32)
            v_past = cv[pages].reshape(past_len, MOBA_HEADS, MOBA_HD).astype(jnp.float32)
            k_all = jnp.concatenate([k_past, kn], axis=0)
            v_all = jnp.concatenate([v_past, vn], axis=0)
            return moba_attend_seq(q, k_all, v_all, pos_s)

        moba_o = lax.map(sample_seq, (mq, mk, mv, page_table))
        ys = merge_and_channel_mix(ys, rg, ret_o, moba_o, *ffn_args)
        ks_l.append(mk)
        vs_l.append(mv)
        ss_l.append(s_s)
    y_prompt = rmsnorm(yp, final_norm_g)
    y_sample = rmsnorm(ys, final_norm_g)
    k_prompt = jnp.stack(kp_l)
    v_prompt = jnp.stack(vp_l)
    ret_state_prompt = jnp.stack(sp_l)
    k_sample = jnp.stack(ks_l)
    v_sample = jnp.stack(vs_l)
    ret_state_sample = jnp.stack(ss_l)
    return (y_prompt, y_sample, k_prompt, v_prompt, ret_state_prompt, k_sample, v_sample, ret_state_sample)
```

```python
import functools

import jax
import jax.numpy as jnp
from jax import lax
from jax.experimental import pallas as pl
from jax.experimental.pallas import tpu as pltpu

F32 = jnp.float32
BF16 = jnp.bfloat16

EPS = 1e-5
ROPE_BASE = 10000.0
RET_CHUNK = 128
MOBA_BLOCK = 256
MOBA_TOPK = 3
TOP_K = 4
SWIGLU_LIMIT = 7.0
SWIGLU_ALPHA = 1.702
NEG = -1e30
LANES = 128
VMEM_LIMIT = 56 * 1024 * 1024

_NT = (((1,), (1,)), ((), ()))


def _cparams(*sem):
    return pltpu.CompilerParams(dimension_semantics=sem, vmem_limit_bytes=VMEM_LIMIT)


def _token_tile(seq):
    for tm in (512, 256, 128):
        if seq % tm == 0:
            return tm
    raise ValueError(f"sequence length {seq} must be a multiple of 128")


def _rms(x, g):
    return x * lax.rsqrt(jnp.mean(x * x, axis=-1, keepdims=True) + EPS) * g


def _rope_tables(pos, dk):
    half = dk // 2
    inv = ROPE_BASE ** (-jnp.arange(half, dtype=F32) / half)
    ang = pos.astype(F32)[:, None] * inv[None, :]
    c = jnp.cos(ang)
    s = jnp.sin(ang)
    return jnp.concatenate([c, c], axis=1), jnp.concatenate([-s, s], axis=1)


def _rotary(z, cos2, sin2, n_heads, dk):
    outs = []
    for h in range(n_heads):
        s = z[:, h * dk:(h + 1) * dk]
        outs.append(s * cos2 + pltpu.roll(s, dk // 2, 1) * sin2)
    return jnp.concatenate(outs, axis=1)


def _inproj_prompt_kernel(x_ref, g_ref, w_ref, wkvT_ref, cos_ref, sin_ref,
                          rq_ref, rk_ref, rv_ref, rg_ref, mq_ref, mv_ref, kT_ref, vT_ref,
                          *, d_ret, d_moba, n_ret_heads):
    dk = d_ret // n_ret_heads
    xn = _rms(x_ref[...], g_ref[...]).astype(BF16)
    cos2 = cos_ref[...]
    sin2 = sin_ref[...]

    def proj(c0, n):
        return jnp.dot(xn, w_ref[:, c0:c0 + n], preferred_element_type=F32)

    rq_ref[...] = _rotary(proj(0, d_ret), cos2, sin2, n_ret_heads, dk)
    rk_ref[...] = _rotary(proj(d_ret, d_ret), cos2, sin2, n_ret_heads, dk) * (dk ** -0.5)
    rv_ref[...] = proj(2 * d_ret, d_ret)
    rg_ref[...] = proj(3 * d_ret, d_ret)
    mq_ref[...] = proj(4 * d_ret, d_moba).astype(BF16)
    mv_ref[...] = proj(4 * d_ret + d_moba, d_moba).astype(BF16)
    kT_ref[0] = lax.dot_general(wkvT_ref[0:d_moba, :], xn, _NT, preferred_element_type=F32)
    vT_ref[0] = lax.dot_general(wkvT_ref[d_moba:2 * d_moba, :], xn, _NT, preferred_element_type=F32)


def _inproj_prompt(x, g, w_main, wkvT, cos2, sin2, *, batch, seq, d_ret, d_moba, n_ret_heads, tm):
    d_model = x.shape[1]
    nt = seq // tm
    dk = d_ret // n_ret_heads
    tok_spec = lambda n: pl.BlockSpec((tm, n), lambda b, i: (b * nt + i, 0))
    full = lambda a: pl.BlockSpec(a.shape, lambda b, i: (0,) * a.ndim)
    t = batch * seq
    out_shape = (
        [jax.ShapeDtypeStruct((t, d_ret), F32)] * 4
        + [jax.ShapeDtypeStruct((t, d_moba), BF16)] * 2
        + [jax.ShapeDtypeStruct((batch, d_moba, seq), F32)] * 2
    )
    out_specs = (
        [tok_spec(d_ret)] * 4 + [tok_spec(d_moba)] * 2
        + [pl.BlockSpec((1, d_moba, tm), lambda b, i: (b, 0, i))] * 2
    )
    return pl.pallas_call(
        functools.partial(_inproj_prompt_kernel, d_ret=d_ret, d_moba=d_moba, n_ret_heads=n_ret_heads),
        grid=(batch, nt),
        in_specs=[tok_spec(d_model), full(g), full(w_main), full(wkvT),
                  pl.BlockSpec((tm, dk), lambda b, i: (i, 0)), pl.BlockSpec((tm, dk), lambda b, i: (i, 0))],
        out_specs=out_specs,
        out_shape=out_shape,
        compiler_params=_cparams("parallel", "arbitrary"),
        name="inproj_prompt",
    )(x, g, w_main, wkvT, cos2, sin2)


def _inproj_sample_kernel(x_ref, g_ref, w_ref, cos_ref, sin_ref,
                          rq_ref, rk_ref, rv_ref, rg_ref, mq_ref, mk_ref, mv_ref,
                          *, d_ret, d_moba, n_ret_heads):
    dk = d_ret // n_ret_heads
    xn = _rms(x_ref[...], g_ref[...]).astype(BF16)
    cos2 = cos_ref[...]
    sin2 = sin_ref[...]

    def proj(c0, n):
        return jnp.dot(xn, w_ref[:, c0:c0 + n], preferred_element_type=F32)

    rq_ref[...] = _rotary(proj(0, d_ret), cos2, sin2, n_ret_heads, dk)
    rk_ref[...] = _rotary(proj(d_ret, d_ret), cos2, sin2, n_ret_heads, dk) * (dk ** -0.5)
    rv_ref[...] = proj(2 * d_ret, d_ret)
    rg_ref[...] = proj(3 * d_ret, d_ret)
    mq_ref[...] = proj(4 * d_ret, d_moba)
    mk_ref[...] = proj(4 * d_ret + d_moba, d_moba)
    mv_ref[...] = proj(4 * d_ret + 2 * d_moba, d_moba)


def _inproj_sample(x, g, w_all, cos2, sin2, *, d_ret, d_moba, n_ret_heads):
    t = x.shape[0]
    args = (x, g, w_all, cos2, sin2)
    full = lambda a: pl.BlockSpec(a.shape, lambda i: (0,) * a.ndim)
    out_shape = [jax.ShapeDtypeStruct((t, d_ret), F32)] * 4 + [jax.ShapeDtypeStruct((t, d_moba), F32)] * 3
    return pl.pallas_call(
        functools.partial(_inproj_sample_kernel, d_ret=d_ret, d_moba=d_moba, n_ret_heads=n_ret_heads),
        grid=(1,),
        in_specs=[full(a) for a in args],
        out_specs=[pl.BlockSpec(s.shape, lambda i: (0, 0)) for s in out_shape],
        out_shape=out_shape,
        compiler_params=_cparams("arbitrary"),
        name="inproj_sample",
    )(*args)


def _ret_tables(n_heads, chunk):
    log_g = jnp.log1p(-jnp.exp2(-5.0 - jnp.arange(n_heads, dtype=F32)))
    n = jnp.arange(chunk, dtype=F32)
    rel = n[:, None] - n[None, :]
    causal = rel >= 0
    decay_in = jnp.where(causal[None], jnp.exp(jnp.where(causal, rel, 0.0)[None] * log_g[:, None, None]), 0.0)
    q_dec = jnp.exp((n[:, None] + 1.0) * log_g[None, :])
    k_dec = jnp.exp((chunk - 1.0 - n[:, None]) * log_g[None, :])
    chunk_dec = jnp.exp(chunk * log_g)
    return decay_in, q_dec, k_dec, chunk_dec


def _ret_prompt_kernel(rq_ref, rk_ref, rv_ref, din_ref, qdec_ref, kdec_ref, cdec_ref,
                       o_ref, s_out_ref, s_scr, *, n_heads, dk, dv, n_chunks):
    c = pl.program_id(1)

    @pl.when(c == 0)
    def _():
        s_scr[...] = jnp.zeros_like(s_scr)

    for h in range(n_heads):
        q = rq_ref[:, h * dk:(h + 1) * dk]
        k = rk_ref[:, h * dk:(h + 1) * dk]
        v = rv_ref[:, h * dv:(h + 1) * dv].astype(BF16)
        s = s_scr[h]
        scores = lax.dot_general(q.astype(BF16), k.astype(BF16), _NT, preferred_element_type=F32) * din_ref[h]
        inner = jnp.dot(scores.astype(BF16), v, preferred_element_type=F32)
        cross = jnp.dot((q * qdec_ref[h]).astype(BF16), s.astype(BF16), preferred_element_type=F32)
        o_ref[:, h * dv:(h + 1) * dv] = inner + cross
        kdT = (k * kdec_ref[h]).T.astype(BF16)
        s_scr[h] = cdec_ref[h] * s + jnp.dot(kdT, v, preferred_element_type=F32)

    @pl.when(c == n_chunks - 1)
    def _():
        s_out_ref[0] = s_scr[...]


def _ret_prompt(rq, rk, rv, *, batch, seq, n_heads, dk, dv):
    chunk = RET_CHUNK
    nc = seq // chunk
    decay_in, q_dec, k_dec, chunk_dec = _ret_tables(n_heads, chunk)
    qdec_b = jnp.broadcast_to(q_dec.T[:, :, None], (n_heads, chunk, dk))
    kdec_b = jnp.broadcast_to(k_dec.T[:, :, None], (n_heads, chunk, dk))
    cdec_b = jnp.broadcast_to(chunk_dec[:, None, None], (n_heads, dk, dv))
    tok = lambda n: pl.BlockSpec((chunk, n), lambda b, c: (b * nc + c, 0))
    full = lambda a: pl.BlockSpec(a.shape, lambda b, c: (0,) * a.ndim)
    return pl.pallas_call(
        functools.partial(_ret_prompt_kernel, n_heads=n_heads, dk=dk, dv=dv, n_chunks=nc),
        grid=(batch, nc),
        in_specs=[tok(n_heads * dk), tok(n_heads * dk), tok(n_heads * dv),
                  full(decay_in), full(qdec_b), full(kdec_b), full(cdec_b)],
        out_specs=[tok(n_heads * dv), pl.BlockSpec((1, n_heads, dk, dv), lambda b, c: (b, 0, 0, 0))],
        out_shape=[jax.ShapeDtypeStruct((batch * seq, n_heads * dv), F32),
                   jax.ShapeDtypeStruct((batch, n_heads, dk, dv), F32)],
        scratch_shapes=[pltpu.VMEM((n_heads, dk, dv), F32)],
        compiler_params=_cparams("parallel", "arbitrary"),
        name="retention_prompt",
    )(rq, rk, rv, decay_in, qdec_b, kdec_b, cdec_b)


def _ret_sample_kernel(rq_ref, rk_ref, rv_ref, s_ref, din_ref, qdec_ref, kdec_ref, cdec_ref,
                       o_ref, s_out_ref, *, n_heads, dk, dv):
    row = lax.broadcasted_iota(jnp.int32, (dk, dk), 0)
    col = lax.broadcasted_iota(jnp.int32, (dk, dk), 1)
    eye = (row == col).astype(BF16)
    for h in range(n_heads):
        q = rq_ref[0, :, h * dk:(h + 1) * dk]
        k = rk_ref[0, :, h * dk:(h + 1) * dk]
        v = rv_ref[0, :, h * dv:(h + 1) * dv].astype(BF16)
        s = s_ref[0, h]
        scores = lax.dot_general(q.astype(BF16), k.astype(BF16), _NT, preferred_element_type=F32) * din_ref[h]
        inner = jnp.dot(scores.astype(BF16), v, preferred_element_type=F32)
        cross = jnp.dot((q * qdec_ref[h]).astype(BF16), s.astype(BF16), preferred_element_type=F32)
        o_ref[0, :, h * dv:(h + 1) * dv] = inner + cross
        kd = (k * kdec_ref[h]).astype(BF16)
        kdT = lax.dot_general(eye, kd, _NT, preferred_element_type=F32).astype(BF16)
        s_out_ref[0, h] = cdec_ref[h] * s + jnp.dot(kdT, v, preferred_element_type=F32)


def _ret_sample(rq, rk, rv, state, *, n_heads, dk, dv):
    bs, ls, _ = rq.shape
    decay_in, q_dec, k_dec, chunk_dec = _ret_tables(n_heads, ls)
    qdec_b = jnp.broadcast_to(q_dec.T[:, :, None], (n_heads, ls, dk))
    kdec_b = jnp.broadcast_to(k_dec.T[:, :, None], (n_heads, ls, dk))
    cdec_b = jnp.broadcast_to(chunk_dec[:, None, None], (n_heads, dk, dv))
    tok = lambda n: pl.BlockSpec((1, ls, n), lambda b: (b, 0, 0))
    full = lambda a: pl.BlockSpec(a.shape, lambda b: (0,) * a.ndim)
    st = pl.BlockSpec((1, n_heads, dk, dv), lambda b: (b, 0, 0, 0))
    return pl.pallas_call(
        functools.partial(_ret_sample_kernel, n_heads=n_heads, dk=dk, dv=dv),
        grid=(bs,),
        in_specs=[tok(n_heads * dk), tok(n_heads * dk), tok(n_heads * dv), st,
                  full(decay_in), full(qdec_b), full(kdec_b), full(cdec_b)],
        out_specs=[tok(n_heads * dv), st],
        out_shape=[jax.ShapeDtypeStruct((bs, ls, n_heads * dv), F32),
                   jax.ShapeDtypeStruct((bs, n_heads, dk, dv), F32)],
        compiler_params=_cparams("parallel"),
        name="retention_sample",
    )(rq, rk, rv, state, decay_in, qdec_b, kdec_b, cdec_b)


def _moba_prompt_kernel(q_ref, kTd_ref, vd_ref, kT_ref, v_ref, o_ref, km_scr, m_scr, l_scr, acc_scr,
                        *, blk, hd, n_blocks, topk):
    c = pl.program_id(2)
    width = 2 * hd
    scale = hd ** -0.5

    @pl.when(c == 0)
    def _():
        lane = lax.broadcasted_iota(jnp.int32, (width, LANES), 1)
        km = jnp.zeros((width, LANES), F32)
        for n in range(n_blocks):
            s = jnp.sum(kT_ref[0, :, n * blk:(n + 1) * blk], axis=1, keepdims=True) * (1.0 / blk)
            km = jnp.where(lane == n, s, km)
        km_scr[...] = km

    q = q_ref[...]
    lane_q = lax.broadcasted_iota(jnp.int32, (blk, width), 1)
    lane_g = lax.broadcasted_iota(jnp.int32, (blk, LANES), 1)
    row_s = lax.broadcasted_iota(jnp.int32, (blk, blk), 0)
    col_s = lax.broadcasted_iota(jnp.int32, (blk, blk), 1)
    kmb = km_scr[...].astype(BF16)
    kd = kTd_ref[0].astype(BF16)
    vd = vd_ref[...]

    for j in range(2):
        qh = jnp.where((lane_q // hd) == j, q, jnp.zeros_like(q))
        gate = jnp.dot(qh, kmb, preferred_element_type=F32)
        gate = jnp.where(lane_g < c, gate, NEG)
        cnt = jnp.zeros((blk, LANES), jnp.int32)
        for m in range(n_blocks):
            gm = gate[:, m:m + 1]
            beats = (gm > gate) | ((gm == gate) & (m < lane_g))
            cnt = cnt + beats.astype(jnp.int32)
        sel = ((lane_g < c) & (cnt < topk)).astype(jnp.int32)

        s = jnp.dot(qh, kd, preferred_element_type=F32) * scale
        s = jnp.where(col_s <= row_s, s, -jnp.inf)
        m0 = jnp.max(s, axis=1, keepdims=True)
        p = jnp.exp(s - m0)
        m_scr[j] = m0
        l_scr[j] = jnp.sum(p, axis=1, keepdims=True)
        acc_scr[j] = jnp.dot(p.astype(BF16), vd, preferred_element_type=F32)

        for n in range(n_blocks - 1):
            @pl.when(n < c)
            def _(n=n, j=j, qh=qh, sel=sel):
                kn = kT_ref[0, :, n * blk:(n + 1) * blk].astype(BF16)
                sn = jnp.dot(qh, kn, preferred_element_type=F32) * scale
                sn = jnp.where(sel[:, n:n + 1] > 0, sn, -jnp.inf)
                m_old = m_scr[j]
                m_new = jnp.maximum(m_old, jnp.max(sn, axis=1, keepdims=True))
                alpha = jnp.exp(m_old - m_new)
                pn = jnp.exp(sn - m_new)
                l_scr[j] = alpha * l_scr[j] + jnp.sum(pn, axis=1, keepdims=True)
                acc_scr[j] = alpha * acc_scr[j] + jnp.dot(
                    pn.astype(BF16), v_ref[n * blk:(n + 1) * blk, :], preferred_element_type=F32)
                m_scr[j] = m_new

    o0 = acc_scr[0] / l_scr[0]
    o1 = acc_scr[1] / l_scr[1]
    o_ref[...] = jnp.where(lane_q < hd, o0, o1).astype(BF16)


def _moba_prompt(mq, mv, kT, *, batch, seq, n_heads, hd):
    blk = MOBA_BLOCK
    nb = seq // blk
    width = 2 * hd
    npair = n_heads // 2
    return pl.pallas_call(
        functools.partial(_moba_prompt_kernel, blk=blk, hd=hd, n_blocks=nb, topk=MOBA_TOPK),
        grid=(batch, npair, nb),
        in_specs=[
            pl.BlockSpec((blk, width), lambda b, hp, c: (b * nb + c, hp)),
            pl.BlockSpec((1, width, blk), lambda b, hp, c: (b, hp, c)),
            pl.BlockSpec((blk, width), lambda b, hp, c: (b * nb + c, hp)),
            pl.BlockSpec((1, width, seq), lambda b, hp, c: (b, hp, 0)),
            pl.BlockSpec((seq, width), lambda b, hp, c: (b, hp)),
        ],
        out_specs=pl.BlockSpec((blk, width), lambda b, hp, c: (b * nb + c, hp)),
        out_shape=jax.ShapeDtypeStruct((batch * seq, n_heads * hd), BF16),
        scratch_shapes=[pltpu.VMEM((width, LANES), F32), pltpu.VMEM((2, blk, 1), F32),
                        pltpu.VMEM((2, blk, 1), F32), pltpu.VMEM((2, blk, width), F32)],
        compiler_params=_cparams("parallel", "parallel", "arbitrary"),
        name="moba_prompt",
    )(mq, kT, mv, kT, mv)


def _moba_gate_kernel(pt_ref, q_ref, *refs, pages_per_step, pages_per_block, n_heads, hd, ls, n_past, topk):
    del pt_ref
    pages = refs[:pages_per_step]
    idx_ref = refs[pages_per_step]
    km_scr = refs[pages_per_step + 1]
    j = pl.program_id(1)
    nj = pl.num_programs(1)
    d_moba = n_heads * hd
    page_size = pages[0].shape[-1]
    blocks_per_step = pages_per_step // pages_per_block

    @pl.when(j == 0)
    def _():
        km_scr[...] = jnp.zeros_like(km_scr)

    col = lax.broadcasted_iota(jnp.int32, (page_size, LANES), 1)
    acc = km_scr[...]
    for u in range(blocks_per_step):
        ks = pages[u * pages_per_block][0]
        for pg in range(1, pages_per_block):
            ks = ks + pages[u * pages_per_block + pg][0]
        ks = ks.reshape(d_moba, page_size).astype(BF16)
        onecol = (col == j * blocks_per_step + u).astype(BF16)
        acc = acc + jnp.dot(ks, onecol, preferred_element_type=F32)
    km_scr[...] = acc

    @pl.when(j == nj - 1)
    def _():
        kmean = (km_scr[...] * (1.0 / (pages_per_block * page_size))).astype(BF16)
        q = q_ref[0]
        rows = ls * n_heads
        qrep = jnp.concatenate([jnp.broadcast_to(q[i:i + 1, :], (n_heads, d_moba)) for i in range(ls)], axis=0)
        r = lax.broadcasted_iota(jnp.int32, (rows, d_moba), 0)
        cc = lax.broadcasted_iota(jnp.int32, (rows, d_moba), 1)
        qbd = jnp.where((cc // hd) == (r % n_heads), qrep, 0.0).astype(BF16)
        gate = jnp.dot(qbd, kmean, preferred_element_type=F32)
        lane = lax.broadcasted_iota(jnp.int32, (rows, LANES), 1)
        work = jnp.where(lane < n_past, gate, jnp.where(lane == n_past, NEG, -jnp.inf))
        out = jnp.zeros((rows, LANES), jnp.int32)
        for t in range(topk):
            mx = jnp.max(work, axis=1, keepdims=True)
            it = jnp.min(jnp.where(work == mx, lane, LANES), axis=1, keepdims=True)
            out = jnp.where(lane == t, it, out)
            work = jnp.where(lane == it, -jnp.inf, work)
        idx_ref[0] = out


def _moba_gate(page_table, mq_s, cacheT, *, n_heads, hd, ls, pages_per_step):
    bs, n_pages = page_table.shape
    page_size = cacheT.shape[-1]
    ppb = MOBA_BLOCK // page_size
    n_past = n_pages // ppb
    steps = n_pages // pages_per_step
    d_moba = n_heads * hd

    def page_spec(i):
        return pl.BlockSpec((1, n_heads, hd, page_size),
                            lambda b, j, pt: (pt[b * n_pages + j * pages_per_step + i], 0, 0, 0))

    grid_spec = pltpu.PrefetchScalarGridSpec(
        num_scalar_prefetch=1,
        grid=(bs, steps),
        in_specs=[pl.BlockSpec((1, ls, d_moba), lambda b, j, pt: (b, 0, 0))]
        + [page_spec(i) for i in range(pages_per_step)],
        out_specs=pl.BlockSpec((1, ls * n_heads, LANES), lambda b, j, pt: (b, 0, 0)),
        scratch_shapes=[pltpu.VMEM((d_moba, LANES), F32)],
    )
    return pl.pallas_call(
        functools.partial(_moba_gate_kernel, pages_per_step=pages_per_step, pages_per_block=ppb,
                          n_heads=n_heads, hd=hd, ls=ls, n_past=n_past, topk=MOBA_TOPK),
        grid_spec=grid_spec,
        out_shape=jax.ShapeDtypeStruct((bs, ls * n_heads, LANES), jnp.int32),
        compiler_params=_cparams("parallel", "arbitrary"),
        name="moba_sample_gate",
    )(page_table.reshape(-1), mq_s, *([cacheT] * pages_per_step))


def _moba_sample_kernel(pt_ref, idx_ref, qc_ref, kc_ref, vc_ref, *refs, ls, topk, ppb, n_heads, hd, n_past):
    del pt_ref
    nslab = ls * topk * ppb
    k_refs = refs[:nslab]
    v_refs = refs[nslab:2 * nslab]
    o_ref = refs[2 * nslab]
    b = pl.program_id(0)
    h = pl.program_id(1)
    scale = hd ** -0.5
    qc = qc_ref[0, 0]
    knew = kc_ref[0, 0]
    vnew = vc_ref[0, 0]
    lane_own = lax.broadcasted_iota(jnp.int32, (1, ls), 1)
    for i in range(ls):
        qcol = qc[:, i:i + 1]
        s_list = []
        for j in range(topk):
            ok = idx_ref[((b * ls + i) * n_heads + h) * topk + j] < n_past
            for pg in range(ppb):
                kt = k_refs[(i * topk + j) * ppb + pg][0, 0]
                s = jnp.sum(kt * qcol, axis=0, keepdims=True) * scale
                s_list.append(jnp.where(ok, s, -jnp.inf))
        s_own = jnp.sum(knew * qcol, axis=0, keepdims=True) * scale
        s_own = jnp.where(lane_own <= i, s_own, -jnp.inf)
        mx = jnp.max(s_own, axis=1, keepdims=True)
        for s in s_list:
            mx = jnp.maximum(mx, jnp.max(s, axis=1, keepdims=True))
        p_own = jnp.exp(s_own - mx)
        denom = jnp.sum(p_own, axis=1, keepdims=True)
        o = jnp.sum(vnew * p_own, axis=1, keepdims=True)
        pv = None
        for t, s in enumerate(s_list):
            p = jnp.exp(s - mx)
            denom = denom + jnp.sum(p, axis=1, keepdims=True)
            term = v_refs[i * topk * ppb + t][0, 0] * p
            pv = term if pv is None else pv + term
        o = o + jnp.sum(pv, axis=1, keepdims=True)
        o_ref[0, 0, :, i:i + 1] = o / denom


def _moba_sample(page_table, idx, q_cols, k_cols, v_cols, cacheT_k, cacheT_v, *, n_heads, hd, ls):
    bs, n_pages = page_table.shape
    page_size = cacheT_k.shape[-1]
    ppb = MOBA_BLOCK // page_size
    n_past = n_pages // ppb
    topk = MOBA_TOPK
    nslab = ls * topk * ppb

    def slab_spec(i, j, pg):
        def imap(b, h, pt, ix):
            blk = jnp.minimum(ix[((b * ls + i) * n_heads + h) * topk + j], n_past - 1)
            return (pt[b * n_pages + blk * ppb + pg], h, 0, 0)
        return pl.BlockSpec((1, 1, hd, page_size), imap)

    slabs = [slab_spec(i, j, pg) for i in range(ls) for j in range(topk) for pg in range(ppb)]
    col_spec = pl.BlockSpec((1, 1, hd, ls), lambda b, h, pt, ix: (b, h, 0, 0))
    grid_spec = pltpu.PrefetchScalarGridSpec(
        num_scalar_prefetch=2,
        grid=(bs, n_heads),
        in_specs=[col_spec, col_spec, col_spec] + slabs + slabs,
        out_specs=col_spec,
    )
    return pl.pallas_call(
        functools.partial(_moba_sample_kernel, ls=ls, topk=topk, ppb=ppb, n_heads=n_heads, hd=hd, n_past=n_past),
        grid_spec=grid_spec,
        out_shape=jax.ShapeDtypeStruct((bs, n_heads, hd, ls), F32),
        compiler_params=_cparams("parallel", "parallel"),
        name="moba_sample_attn",
    )(page_table.reshape(-1), idx, q_cols, k_cols, v_cols, *([cacheT_k] * nslab), *([cacheT_v] * nslab))


def _merge_kernel(x_ref, ret_ref, rg_ref, mo_ref, rng_ref, wo_ref, n2_ref, rwT_ref, rb_ref,
                  hres_ref, xn_ref, gates_ref, *, n_heads, dv, n_experts, top_k):
    d_ret = n_heads * dv
    ret_o = ret_ref[...]
    parts = []
    for h in range(n_heads):
        r = ret_o[:, h * dv:(h + 1) * dv]
        mu = jnp.mean(r, axis=-1, keepdims=True)
        var = jnp.mean(jnp.square(r - mu), axis=-1, keepdims=True)
        parts.append((r - mu) * lax.rsqrt(var + EPS))
    rg = rg_ref[...]
    ret = jnp.concatenate(parts, axis=1) * rng_ref[...] * (rg * jax.nn.sigmoid(rg))
    mix = (jnp.dot(ret.astype(BF16), wo_ref[0:d_ret, :], preferred_element_type=F32)
           + jnp.dot(mo_ref[...], wo_ref[d_ret:, :], preferred_element_type=F32))
    hres = x_ref[...] + mix
    hres_ref[...] = hres
    xn = _rms(hres, n2_ref[...]).astype(BF16)
    xn_ref[...] = xn
    logits = lax.dot_general(xn, rwT_ref[...], _NT, preferred_element_type=F32) + rb_ref[...]
    lane = lax.broadcasted_iota(jnp.int32, logits.shape, 1)
    work = logits
    vals, hots = [], []
    for _ in range(top_k):
        mx = jnp.max(work, axis=1, keepdims=True)
        it = jnp.min(jnp.where(work == mx, lane, n_experts), axis=1, keepdims=True)
        hot = lane == it
        vals.append(mx)
        hots.append(hot)
        work = jnp.where(hot, -jnp.inf, work)
    exps = [jnp.exp(v - vals[0]) for v in vals]
    denom = exps[0]
    for e in exps[1:]:
        denom = denom + e
    gates = jnp.zeros_like(logits)
    for e, hot in zip(exps, hots):
        gates = jnp.where(hot, e / denom, gates)
    gates_ref[...] = gates


def _merge(x, ret_o, rg, moba_o, ret_norm_g, w_out, norm2_g, rwT, router_b, *, n_heads, dv, tm):
    t, d_model = x.shape
    n_experts = rwT.shape[0]
    tok = lambda n: pl.BlockSpec((tm, n), lambda i: (i, 0))
    full = lambda a: pl.BlockSpec(a.shape, lambda i: (0,) * a.ndim)
    return pl.pallas_call(
        functools.partial(_merge_kernel, n_heads=n_heads, dv=dv, n_experts=n_experts, top_k=TOP_K),
        grid=(t // tm,),
        in_specs=[tok(d_model), tok(n_heads * dv), tok(n_heads * dv), tok(moba_o.shape[1]),
                  full(ret_norm_g), full(w_out), full(norm2_g), full(rwT), full(router_b)],
        out_specs=[tok(d_model), tok(d_model), tok(n_experts)],
        out_shape=[jax.ShapeDtypeStruct((t, d_model), F32), jax.ShapeDtypeStruct((t, d_model), BF16),
                   jax.ShapeDtypeStruct((t, n_experts), F32)],
        compiler_params=_cparams("parallel"),
        name="merge_router",
    )(x, ret_o, rg, moba_o, ret_norm_g, w_out, norm2_g, rwT, router_b)


def _moe_kernel(xn_ref, gates_ref, hres_ref, wg_ref, wl_ref, bg_ref, bl_ref, wd_ref, bd_ref, fg_ref,
                y_ref, acc_scr):
    e = pl.program_id(1)
    ne = pl.num_programs(1)

    @pl.when(e == 0)
    def _():
        acc_scr[...] = jnp.zeros_like(acc_scr)

    xn = xn_ref[...]
    gates = gates_ref[...]
    lane = lax.broadcasted_iota(jnp.int32, gates.shape, 1)
    g = jnp.sum(jnp.where(lane == e, gates, 0.0), axis=1, keepdims=True)
    ug = jnp.dot(xn, wg_ref[0], preferred_element_type=F32) + bg_ref[0]
    ul = jnp.dot(xn, wl_ref[0], preferred_element_type=F32) + bl_ref[0]
    glu = jnp.minimum(ug, SWIGLU_LIMIT)
    lin = jnp.clip(ul, -SWIGLU_LIMIT, SWIGLU_LIMIT)
    a = glu * jax.nn.sigmoid(SWIGLU_ALPHA * glu) * (lin + 1.0)
    out = jnp.dot(a.astype(BF16), wd_ref[0], preferred_element_type=F32) + bd_ref[0]
    acc_scr[...] += g * out

    @pl.when(e == ne - 1)
    def _():
        y_ref[...] = _rms(hres_ref[...] + acc_scr[...], fg_ref[...])


def _moe(xn, gates, hres, wg, wl, bg, bl, wd, bd, fg, *, tm):
    t, d_model = xn.shape
    n_experts, _, d_ff = wg.shape
    tok = lambda n: pl.BlockSpec((tm, n), lambda i, e: (i, 0))
    ex = lambda a: pl.BlockSpec((1,) + a.shape[1:], lambda i, e: (e,) + (0,) * (a.ndim - 1))
    return pl.pallas_call(
        _moe_kernel,
        grid=(t // tm, n_experts),
        in_specs=[tok(d_model), tok(n_experts), tok(d_model), ex(wg), ex(wl), ex(bg), ex(bl), ex(wd), ex(bd),
                  pl.BlockSpec(fg.shape, lambda i, e: (0, 0))],
        out_specs=tok(d_model),
        out_shape=jax.ShapeDtypeStruct((t, d_model), F32),
        scratch_shapes=[pltpu.VMEM((tm, d_model), F32)],
        compiler_params=_cparams("parallel", "arbitrary"),
        name="moe_ffn",
    )(xn, gates, hres, wg, wl, bg, bl, wd, bd, fg)


def kernel(x_prompt, x_sample, cache_k, cache_v, state_ret, page_table, norm1_g, w_in, ret_norm_g, w_out,
           norm2_g, router_w, router_b, w_gate_up, b_gate_up, w_down, b_down, final_norm_g):
    bp, lp, d_model = x_prompt.shape
    bs, ls, _ = x_sample.shape
    depth = w_in.shape[0]
    assert depth == 1, "single-layer step"
    n_pages = page_table.shape[1]
    page_size, mh, hd = cache_k.shape[2], cache_k.shape[3], cache_k.shape[4]
    rh, dk, dv = state_ret.shape[2], state_ret.shape[3], state_ret.shape[4]
    d_ret = rh * dk
    d_moba = mh * hd
    n_experts = router_w.shape[2]
    d_ff = w_down.shape[2]
    past_len = n_pages * page_size
    layer = 0

    w = w_in[layer]
    w_all = w.astype(BF16)
    w_main = jnp.concatenate([w[:, :4 * d_ret + d_moba], w[:, 4 * d_ret + 2 * d_moba:]], axis=1).astype(BF16)
    wkvT = w[:, 4 * d_ret + d_moba:].T.astype(BF16)
    g1 = norm1_g[layer][None, :]
    wo = w_out[layer].astype(BF16)
    rng = ret_norm_g[layer][None, :]
    n2 = norm2_g[layer][None, :]
    rwT = router_w[layer].T.astype(BF16)
    rb = router_b[layer][None, :]
    wgu = w_gate_up[layer]
    wg = wgu[:, :, 0::2].astype(BF16)
    wl = wgu[:, :, 1::2].astype(BF16)
    bg = b_gate_up[layer][:, None, 0::2]
    bl = b_gate_up[layer][:, None, 1::2]
    wd = w_down[layer].astype(BF16)
    bd = b_down[layer][:, None, :]
    fg = final_norm_g[None, :]

    tm = _token_tile(lp)
    pos_p = jnp.arange(lp, dtype=jnp.int32)
    cos_p, sin_p = _rope_tables(pos_p, dk)
    xp = x_prompt.reshape(bp * lp, d_model)
    rq, rk, rv, rg, mq, mv, kT, vT = _inproj_prompt(
        xp, g1, w_main, wkvT, cos_p, sin_p, batch=bp, seq=lp, d_ret=d_ret, d_moba=d_moba, n_ret_heads=rh, tm=tm)
    ret_o, s_p = _ret_prompt(rq, rk, rv, batch=bp, seq=lp, n_heads=rh, dk=dk, dv=dv)
    moba_o = _moba_prompt(mq, mv, kT, batch=bp, seq=lp, n_heads=mh, hd=hd)
    hres, xn, gates = _merge(xp, ret_o, rg, moba_o, rng, wo, n2, rwT, rb, n_heads=rh, dv=dv, tm=tm)
    y_p = _moe(xn, gates, hres, wg, wl, bg, bl, wd, bd, fg, tm=tm)

    pos_s = past_len + jnp.arange(ls, dtype=jnp.int32)
    cos_s, sin_s = _rope_tables(jnp.tile(pos_s, bs), dk)
    xs = x_sample.reshape(bs * ls, d_model)
    rq_s, rk_s, rv_s, rg_s, mq_s, mk_s, mv_s = _inproj_sample(
        xs, g1, w_all, cos_s, sin_s, d_ret=d_ret, d_moba=d_moba, n_ret_heads=rh)
    r3 = lambda a: a.reshape(bs, ls, a.shape[1])
    ret_o_s, s_s = _ret_sample(r3(rq_s), r3(rk_s), r3(rv_s), state_ret[layer], n_heads=rh, dk=dk, dv=dv)
    ckT = cache_k[layer].transpose(0, 2, 3, 1)
    cvT = cache_v[layer].transpose(0, 2, 3, 1)
    idx_pad = _moba_gate(page_table, r3(mq_s), ckT, n_heads=mh, hd=hd, ls=ls, pages_per_step=16)
    idx = idx_pad[:, :, :MOBA_TOPK].reshape(-1)
    cols = lambda a: a.reshape(bs, ls, mh, hd).transpose(0, 2, 3, 1)
    o_cols = _moba_sample(page_table, idx, cols(mq_s), cols(mk_s), cols(mv_s), ckT, cvT,
                          n_heads=mh, hd=hd, ls=ls)
    moba_o_s = o_cols.transpose(0, 3, 1, 2).reshape(bs * ls, d_moba).astype(BF16)
    hres_s, xn_s, gates_s = _merge(xs, ret_o_s.reshape(bs * ls, d_ret), rg_s, moba_o_s, rng, wo, n2, rwT, rb,
                                   n_heads=rh, dv=dv, tm=bs * ls)
    y_s = _moe(xn_s, gates_s, hres_s, wg, wl, bg, bl, wd, bd, fg, tm=bs * ls)

    y_prompt = y_p.reshape(bp, lp, d_model)
    y_sample = y_s.reshape(bs, ls, d_model)
    k_prompt = kT.reshape(bp, mh, hd, lp).transpose(0, 3, 1, 2)[None]
    v_prompt = vT.reshape(bp, mh, hd, lp).transpose(0, 3, 1, 2)[None]
    k_sample = mk_s.reshape(1, bs, ls, mh, hd)
    v_sample = mv_s.reshape(1, bs, ls, mh, hd)
    return (y_prompt, y_sample, k_prompt, v_prompt, s_p[None], k_sample, v_sample, s_s[None])
```

```python
import functools

import jax
import jax.numpy as jnp
from jax import lax
from jax.experimental import pallas as pl
from jax.experimental.pallas import tpu as pltpu

F32 = jnp.float32
BF16 = jnp.bfloat16

EPS = 1e-5
ROPE_BASE = 10000.0
RET_CHUNK = 128
MOBA_BLOCK = 256
MOBA_TOPK = 3
TOP_K = 4
SWIGLU_LIMIT = 7.0
SWIGLU_ALPHA = 1.702
NEG = -1e30
LANES = 128
SLOT_ALIGN = 16
VMEM_LIMIT = 56 * 1024 * 1024

_NT = (((1,), (1,)), ((), ()))


def _cparams(*sem):
    return pltpu.CompilerParams(dimension_semantics=sem, vmem_limit_bytes=VMEM_LIMIT)


def _token_tile(seq):
    for tm in (512, 256, 128):
        if seq % tm == 0:
            return tm
    raise ValueError(f"sequence length {seq} must be a multiple of 128")


def _rms(x, g):
    return x * lax.rsqrt(jnp.mean(x * x, axis=-1, keepdims=True) + EPS) * g


def _rope_tables(pos, dk):
    half = dk // 2
    inv = ROPE_BASE ** (-jnp.arange(half, dtype=F32) / half)
    ang = pos.astype(F32)[:, None] * inv[None, :]
    c = jnp.cos(ang)
    s = jnp.sin(ang)
    return jnp.concatenate([c, c], axis=1), jnp.concatenate([-s, s], axis=1)


def _rotary(z, cos2, sin2, n_heads, dk):
    outs = []
    for h in range(n_heads):
        s = z[:, h * dk:(h + 1) * dk]
        outs.append(s * cos2 + pltpu.roll(s, dk // 2, 1) * sin2)
    return jnp.concatenate(outs, axis=1)


def _inproj_prompt_kernel(x_ref, g_ref, w_ref, wkvT_ref, cos_ref, sin_ref,
                          rq_ref, rk_ref, rv_ref, rg_ref, mq_ref, mv_ref, kT_ref, vT_ref,
                          *, d_ret, d_moba, n_ret_heads):
    dk = d_ret // n_ret_heads
    xn = _rms(x_ref[...], g_ref[...]).astype(BF16)
    cos2 = cos_ref[...]
    sin2 = sin_ref[...]

    def proj(c0, n):
        return jnp.dot(xn, w_ref[:, c0:c0 + n], preferred_element_type=F32)

    rq_ref[...] = _rotary(proj(0, d_ret), cos2, sin2, n_ret_heads, dk)
    rk_ref[...] = _rotary(proj(d_ret, d_ret), cos2, sin2, n_ret_heads, dk) * (dk ** -0.5)
    rv_ref[...] = proj(2 * d_ret, d_ret)
    rg_ref[...] = proj(3 * d_ret, d_ret)
    mq_ref[...] = proj(4 * d_ret, d_moba).astype(BF16)
    mv_ref[...] = proj(4 * d_ret + d_moba, d_moba).astype(BF16)
    kT_ref[0] = lax.dot_general(wkvT_ref[0:d_moba, :], xn, _NT, preferred_element_type=F32)
    vT_ref[0] = lax.dot_general(wkvT_ref[d_moba:2 * d_moba, :], xn, _NT, preferred_element_type=F32)


def _inproj_prompt(x, g, w_main, wkvT, cos2, sin2, *, batch, seq, d_ret, d_moba, n_ret_heads, tm):
    d_model = x.shape[1]
    nt = seq // tm
    dk = d_ret // n_ret_heads
    tok_spec = lambda n: pl.BlockSpec((tm, n), lambda b, i: (b * nt + i, 0))
    full = lambda a: pl.BlockSpec(a.shape, lambda b, i: (0,) * a.ndim)
    t = batch * seq
    out_shape = (
        [jax.ShapeDtypeStruct((t, d_ret), F32)] * 4
        + [jax.ShapeDtypeStruct((t, d_moba), BF16)] * 2
        + [jax.ShapeDtypeStruct((batch, d_moba, seq), F32)] * 2
    )
    out_specs = (
        [tok_spec(d_ret)] * 4 + [tok_spec(d_moba)] * 2
        + [pl.BlockSpec((1, d_moba, tm), lambda b, i: (b, 0, i))] * 2
    )
    return pl.pallas_call(
        functools.partial(_inproj_prompt_kernel, d_ret=d_ret, d_moba=d_moba, n_ret_heads=n_ret_heads),
        grid=(batch, nt),
        in_specs=[tok_spec(d_model), full(g), full(w_main), full(wkvT),
                  pl.BlockSpec((tm, dk), lambda b, i: (i, 0)), pl.BlockSpec((tm, dk), lambda b, i: (i, 0))],
        out_specs=out_specs,
        out_shape=out_shape,
        compiler_params=_cparams("parallel", "arbitrary"),
        name="inproj_prompt",
    )(x, g, w_main, wkvT, cos2, sin2)


def _inproj_sample_kernel(x_ref, g_ref, w_ref, cos_ref, sin_ref,
                          rq_ref, rk_ref, rv_ref, rg_ref, mq_ref, mk_ref, mv_ref,
                          *, d_ret, d_moba, n_ret_heads):
    dk = d_ret // n_ret_heads
    xn = _rms(x_ref[...], g_ref[...]).astype(BF16)
    cos2 = cos_ref[...]
    sin2 = sin_ref[...]

    def proj(c0, n):
        return jnp.dot(xn, w_ref[:, c0:c0 + n], preferred_element_type=F32)

    rq_ref[...] = _rotary(proj(0, d_ret), cos2, sin2, n_ret_heads, dk)
    rk_ref[...] = _rotary(proj(d_ret, d_ret), cos2, sin2, n_ret_heads, dk) * (dk ** -0.5)
    rv_ref[...] = proj(2 * d_ret, d_ret)
    rg_ref[...] = proj(3 * d_ret, d_ret)
    mq_ref[...] = proj(4 * d_ret, d_moba)
    mk_ref[...] = proj(4 * d_ret + d_moba, d_moba)
    mv_ref[...] = proj(4 * d_ret + 2 * d_moba, d_moba)


def _inproj_sample(x, g, w_all, cos2, sin2, *, d_ret, d_moba, n_ret_heads):
    t = x.shape[0]
    args = (x, g, w_all, cos2, sin2)
    full = lambda a: pl.BlockSpec(a.shape, lambda i: (0,) * a.ndim)
    out_shape = [jax.ShapeDtypeStruct((t, d_ret), F32)] * 4 + [jax.ShapeDtypeStruct((t, d_moba), F32)] * 3
    return pl.pallas_call(
        functools.partial(_inproj_sample_kernel, d_ret=d_ret, d_moba=d_moba, n_ret_heads=n_ret_heads),
        grid=(1,),
        in_specs=[full(a) for a in args],
        out_specs=[pl.BlockSpec(s.shape, lambda i: (0, 0)) for s in out_shape],
        out_shape=out_shape,
        compiler_params=_cparams("arbitrary"),
        name="inproj_sample",
    )(*args)


def _ret_tables(n_heads, chunk):
    log_g = jnp.log1p(-jnp.exp2(-5.0 - jnp.arange(n_heads, dtype=F32)))
    n = jnp.arange(chunk, dtype=F32)
    rel = n[:, None] - n[None, :]
    causal = rel >= 0
    decay_in = jnp.where(causal[None], jnp.exp(jnp.where(causal, rel, 0.0)[None] * log_g[:, None, None]), 0.0)
    q_dec = jnp.exp((n[:, None] + 1.0) * log_g[None, :])
    k_dec = jnp.exp((chunk - 1.0 - n[:, None]) * log_g[None, :])
    chunk_dec = jnp.exp(chunk * log_g)
    return decay_in, q_dec, k_dec, chunk_dec


def _ret_prompt_kernel(rq_ref, rk_ref, rv_ref, din_ref, qdec_ref, kdec_ref, cdec_ref,
                       o_ref, s_out_ref, s_scr, *, n_heads, dk, dv, n_chunks):
    c = pl.program_id(1)

    @pl.when(c == 0)
    def _():
        s_scr[...] = jnp.zeros_like(s_scr)

    for h in range(n_heads):
        q = rq_ref[:, h * dk:(h + 1) * dk]
        k = rk_ref[:, h * dk:(h + 1) * dk]
        v = rv_ref[:, h * dv:(h + 1) * dv].astype(BF16)
        s = s_scr[h]
        scores = lax.dot_general(q.astype(BF16), k.astype(BF16), _NT, preferred_element_type=F32) * din_ref[h]
        inner = jnp.dot(scores.astype(BF16), v, preferred_element_type=F32)
        cross = jnp.dot((q * qdec_ref[h]).astype(BF16), s.astype(BF16), preferred_element_type=F32)
        o_ref[:, h * dv:(h + 1) * dv] = inner + cross
        kdT = (k * kdec_ref[h]).T.astype(BF16)
        s_scr[h] = cdec_ref[h] * s + jnp.dot(kdT, v, preferred_element_type=F32)

    @pl.when(c == n_chunks - 1)
    def _():
        s_out_ref[0] = s_scr[...]


def _ret_prompt(rq, rk, rv, *, batch, seq, n_heads, dk, dv):
    chunk = RET_CHUNK
    nc = seq // chunk
    decay_in, q_dec, k_dec, chunk_dec = _ret_tables(n_heads, chunk)
    qdec_b = jnp.broadcast_to(q_dec.T[:, :, None], (n_heads, chunk, dk))
    kdec_b = jnp.broadcast_to(k_dec.T[:, :, None], (n_heads, chunk, dk))
    cdec_b = jnp.broadcast_to(chunk_dec[:, None, None], (n_heads, dk, dv))
    tok = lambda n: pl.BlockSpec((chunk, n), lambda b, c: (b * nc + c, 0))
    full = lambda a: pl.BlockSpec(a.shape, lambda b, c: (0,) * a.ndim)
    return pl.pallas_call(
        functools.partial(_ret_prompt_kernel, n_heads=n_heads, dk=dk, dv=dv, n_chunks=nc),
        grid=(batch, nc),
        in_specs=[tok(n_heads * dk), tok(n_heads * dk), tok(n_heads * dv),
                  full(decay_in), full(qdec_b), full(kdec_b), full(cdec_b)],
        out_specs=[tok(n_heads * dv), pl.BlockSpec((1, n_heads, dk, dv), lambda b, c: (b, 0, 0, 0))],
        out_shape=[jax.ShapeDtypeStruct((batch * seq, n_heads * dv), F32),
                   jax.ShapeDtypeStruct((batch, n_heads, dk, dv), F32)],
        scratch_shapes=[pltpu.VMEM((n_heads, dk, dv), F32)],
        compiler_params=_cparams("parallel", "arbitrary"),
        name="retention_prompt",
    )(rq, rk, rv, decay_in, qdec_b, kdec_b, cdec_b)


def _ret_sample_kernel(rq_ref, rk_ref, rv_ref, s_ref, din_ref, qdec_ref, kdec_ref, cdec_ref,
                       o_ref, s_out_ref, *, n_heads, dk, dv):
    row = lax.broadcasted_iota(jnp.int32, (dk, dk), 0)
    col = lax.broadcasted_iota(jnp.int32, (dk, dk), 1)
    eye = (row == col).astype(BF16)
    for h in range(n_heads):
        q = rq_ref[0, :, h * dk:(h + 1) * dk]
        k = rk_ref[0, :, h * dk:(h + 1) * dk]
        v = rv_ref[0, :, h * dv:(h + 1) * dv].astype(BF16)
        s = s_ref[0, h]
        scores = lax.dot_general(q.astype(BF16), k.astype(BF16), _NT, preferred_element_type=F32) * din_ref[h]
        inner = jnp.dot(scores.astype(BF16), v, preferred_element_type=F32)
        cross = jnp.dot((q * qdec_ref[h]).astype(BF16), s.astype(BF16), preferred_element_type=F32)
        o_ref[0, :, h * dv:(h + 1) * dv] = inner + cross
        kd = (k * kdec_ref[h]).astype(BF16)
        kdT = lax.dot_general(eye, kd, _NT, preferred_element_type=F32).astype(BF16)
        s_out_ref[0, h] = cdec_ref[h] * s + jnp.dot(kdT, v, preferred_element_type=F32)


def _ret_sample(rq, rk, rv, state, *, n_heads, dk, dv):
    bs, ls, _ = rq.shape
    decay_in, q_dec, k_dec, chunk_dec = _ret_tables(n_heads, ls)
    qdec_b = jnp.broadcast_to(q_dec.T[:, :, None], (n_heads, ls, dk))
    kdec_b = jnp.broadcast_to(k_dec.T[:, :, None], (n_heads, ls, dk))
    cdec_b = jnp.broadcast_to(chunk_dec[:, None, None], (n_heads, dk, dv))
    tok = lambda n: pl.BlockSpec((1, ls, n), lambda b: (b, 0, 0))
    full = lambda a: pl.BlockSpec(a.shape, lambda b: (0,) * a.ndim)
    st = pl.BlockSpec((1, n_heads, dk, dv), lambda b: (b, 0, 0, 0))
    return pl.pallas_call(
        functools.partial(_ret_sample_kernel, n_heads=n_heads, dk=dk, dv=dv),
        grid=(bs,),
        in_specs=[tok(n_heads * dk), tok(n_heads * dk), tok(n_heads * dv), st,
                  full(decay_in), full(qdec_b), full(kdec_b), full(cdec_b)],
        out_specs=[tok(n_heads * dv), st],
        out_shape=[jax.ShapeDtypeStruct((bs, ls, n_heads * dv), F32),
                   jax.ShapeDtypeStruct((bs, n_heads, dk, dv), F32)],
        compiler_params=_cparams("parallel"),
        name="retention_sample",
    )(rq, rk, rv, state, decay_in, qdec_b, kdec_b, cdec_b)


def _moba_prompt_kernel(q_ref, kTd_ref, vd_ref, kT_ref, v_ref, eneg_ref, o_ref, km_scr, kaug_scr,
                        *, blk, hd, n_blocks, topk):
    c = pl.program_id(2)
    width = 2 * hd
    scale = hd ** -0.5

    @pl.when(c == 0)
    def _():
        lane = lax.broadcasted_iota(jnp.int32, (width, LANES), 1)
        km = jnp.zeros((width, LANES), F32)
        for n in range(n_blocks):
            s = jnp.sum(kT_ref[0, :, n * blk:(n + 1) * blk], axis=1, keepdims=True) * (1.0 / blk)
            km = jnp.where(lane == n, s, km)
        km_scr[...] = km
        kaug_scr[0:width, :] = kT_ref[0].astype(BF16)
        kaug_scr[width:width + LANES, :] = eneg_ref[...]

    q = q_ref[...]
    lane_q = lax.broadcasted_iota(jnp.int32, (blk, width), 1)
    lane_g = lax.broadcasted_iota(jnp.int32, (blk, LANES), 1)
    row_s = lax.broadcasted_iota(jnp.int32, (blk, blk), 0)
    col_s = lax.broadcasted_iota(jnp.int32, (blk, blk), 1)
    kmb = km_scr[...].astype(BF16)
    kd = kTd_ref[0].astype(BF16)
    vd = vd_ref[...]

    heads = []
    for j in range(2):
        qh = jnp.where((lane_q // hd) == j, q, jnp.zeros_like(q))
        gate = jnp.dot(qh, kmb, preferred_element_type=F32)
        gate = jnp.where(lane_g < c, gate, NEG)
        cnt = jnp.zeros((blk, LANES), jnp.int32)
        for m in range(n_blocks):
            gm = gate[:, m:m + 1]
            beats = (gm > gate) | ((gm == gate) & (m < lane_g))
            cnt = cnt + beats.astype(jnp.int32)
        sel = (lane_g < c) & (cnt < topk)
        pen = jnp.where(sel | (lane_g >= n_blocks), 0.0, 1.0).astype(BF16)
        qs = qh * scale
        s_d = jnp.dot(qs, kd, preferred_element_type=F32)
        s_d = jnp.where(col_s <= row_s, s_d, -jnp.inf)
        heads.append((jnp.concatenate([qs, pen], axis=1), s_d, jnp.max(s_d, axis=1, keepdims=True)))

    def finish(n_wide):
        outs = []
        for qa, s_d, m_d in heads:
            if n_wide:
                w = n_wide * blk
                s_w = jnp.dot(qa, kaug_scr[:, :w], preferred_element_type=F32)
                mx = jnp.maximum(m_d, jnp.max(s_w, axis=1, keepdims=True))
                p_w = jnp.exp(s_w - mx)
                p_d = jnp.exp(s_d - mx)
                den = jnp.sum(p_d, axis=1, keepdims=True) + jnp.sum(p_w, axis=1, keepdims=True)
                acc = (jnp.dot(p_d.astype(BF16), vd, preferred_element_type=F32)
                       + jnp.dot(p_w.astype(BF16), v_ref[0:w, :], preferred_element_type=F32))
            else:
                p_d = jnp.exp(s_d - m_d)
                den = jnp.sum(p_d, axis=1, keepdims=True)
                acc = jnp.dot(p_d.astype(BF16), vd, preferred_element_type=F32)
            outs.append(acc / den)
        o_ref[...] = jnp.where(lane_q < hd, outs[0], outs[1]).astype(BF16)

    widths = sorted({w for w in (2, 4) if w < n_blocks - 1} | ({n_blocks - 1} if n_blocks > 1 else set()))
    lo = 1
    for n_wide in widths:
        @pl.when((c >= lo) & (c <= n_wide))
        def _(n_wide=n_wide):
            finish(n_wide)
        lo = n_wide + 1

    @pl.when(c == 0)
    def _():
        finish(0)


def _moba_prompt(mq, mv, kT, *, batch, seq, n_heads, hd):
    blk = MOBA_BLOCK
    nb = seq // blk
    width = 2 * hd
    npair = n_heads // 2
    assert nb <= LANES and width == LANES
    eneg = jnp.where(jnp.arange(LANES)[:, None] == (jnp.arange(seq) // blk)[None, :], NEG, 0.0).astype(BF16)
    return pl.pallas_call(
        functools.partial(_moba_prompt_kernel, blk=blk, hd=hd, n_blocks=nb, topk=MOBA_TOPK),
        grid=(batch, npair, nb),
        in_specs=[
            pl.BlockSpec((blk, width), lambda b, hp, c: (b * nb + c, hp)),
            pl.BlockSpec((1, width, blk), lambda b, hp, c: (b, hp, c)),
            pl.BlockSpec((blk, width), lambda b, hp, c: (b * nb + c, hp)),
            pl.BlockSpec((1, width, seq), lambda b, hp, c: (b, hp, 0)),
            pl.BlockSpec((seq, width), lambda b, hp, c: (b, hp)),
            pl.BlockSpec((LANES, seq), lambda b, hp, c: (0, 0)),
        ],
        out_specs=pl.BlockSpec((blk, width), lambda b, hp, c: (b * nb + c, hp)),
        out_shape=jax.ShapeDtypeStruct((batch * seq, n_heads * hd), BF16),
        scratch_shapes=[pltpu.VMEM((width, LANES), F32), pltpu.VMEM((width + LANES, seq), BF16)],
        compiler_params=_cparams("parallel", "parallel", "arbitrary"),
        name="moba_prompt",
    )(mq, kT, mv, kT, mv, eneg)


def _moba_gate_kernel(pt_ref, q_ref, *refs, pages_per_step, pages_per_block, n_heads, hd, ls, n_past, topk):
    del pt_ref
    pages = refs[:pages_per_step]
    idx_ref = refs[pages_per_step]
    km_scr = refs[pages_per_step + 1]
    j = pl.program_id(1)
    nj = pl.num_programs(1)
    d_moba = n_heads * hd
    page_size = pages[0].shape[-1]
    blocks_per_step = pages_per_step // pages_per_block

    @pl.when(j == 0)
    def _():
        km_scr[...] = jnp.zeros_like(km_scr)

    col = lax.broadcasted_iota(jnp.int32, (page_size, LANES), 1)
    acc = km_scr[...]
    for u in range(blocks_per_step):
        ks = pages[u * pages_per_block][0]
        for pg in range(1, pages_per_block):
            ks = ks + pages[u * pages_per_block + pg][0]
        ks = ks.reshape(d_moba, page_size).astype(BF16)
        onecol = (col == j * blocks_per_step + u).astype(BF16)
        acc = acc + jnp.dot(ks, onecol, preferred_element_type=F32)
    km_scr[...] = acc

    @pl.when(j == nj - 1)
    def _():
        kmean = (km_scr[...] * (1.0 / (pages_per_block * page_size))).astype(BF16)
        q = q_ref[0]
        rows = ls * n_heads
        qrep = jnp.concatenate([jnp.broadcast_to(q[i:i + 1, :], (n_heads, d_moba)) for i in range(ls)], axis=0)
        r = lax.broadcasted_iota(jnp.int32, (rows, d_moba), 0)
        cc = lax.broadcasted_iota(jnp.int32, (rows, d_moba), 1)
        qbd = jnp.where((cc // hd) == (r % n_heads), qrep, 0.0).astype(BF16)
        gate = jnp.dot(qbd, kmean, preferred_element_type=F32)
        lane = lax.broadcasted_iota(jnp.int32, (rows, LANES), 1)
        work = jnp.where(lane < n_past, gate, jnp.where(lane == n_past, NEG, -jnp.inf))
        out = jnp.zeros((rows, LANES), jnp.int32)
        for t in range(topk):
            mx = jnp.max(work, axis=1, keepdims=True)
            it = jnp.min(jnp.where(work == mx, lane, LANES), axis=1, keepdims=True)
            out = jnp.where(lane == t, it, out)
            work = jnp.where(lane == it, -jnp.inf, work)
        idx_ref[0] = out


def _moba_gate(page_table, mq_s, cacheT, *, n_heads, hd, ls, pages_per_step):
    bs, n_pages = page_table.shape
    page_size = cacheT.shape[-1]
    ppb = MOBA_BLOCK // page_size
    n_past = n_pages // ppb
    steps = n_pages // pages_per_step
    d_moba = n_heads * hd

    def page_spec(i):
        return pl.BlockSpec((1, n_heads, hd, page_size),
                            lambda b, j, pt: (pt[b * n_pages + j * pages_per_step + i], 0, 0, 0))

    grid_spec = pltpu.PrefetchScalarGridSpec(
        num_scalar_prefetch=1,
        grid=(bs, steps),
        in_specs=[pl.BlockSpec((1, ls, d_moba), lambda b, j, pt: (b, 0, 0))]
        + [page_spec(i) for i in range(pages_per_step)],
        out_specs=pl.BlockSpec((1, ls * n_heads, LANES), lambda b, j, pt: (b, 0, 0)),
        scratch_shapes=[pltpu.VMEM((d_moba, LANES), F32)],
    )
    return pl.pallas_call(
        functools.partial(_moba_gate_kernel, pages_per_step=pages_per_step, pages_per_block=ppb,
                          n_heads=n_heads, hd=hd, ls=ls, n_past=n_past, topk=MOBA_TOPK),
        grid_spec=grid_spec,
        out_shape=jax.ShapeDtypeStruct((bs, ls * n_heads, LANES), jnp.int32),
        compiler_params=_cparams("parallel", "arbitrary"),
        name="moba_sample_gate",
    )(page_table.reshape(-1), mq_s, *([cacheT] * pages_per_step))


def _moba_sample_kernel(pt_ref, idx_ref, qc_ref, kc_ref, vc_ref, *refs, ls, topk, ppb, n_heads, hd, n_past):
    del pt_ref
    nslab = ls * topk * ppb
    k_refs = refs[:nslab]
    v_refs = refs[nslab:2 * nslab]
    o_ref = refs[2 * nslab]
    b = pl.program_id(0)
    h = pl.program_id(1)
    scale = hd ** -0.5
    qc = qc_ref[0, 0]
    knew = kc_ref[0, 0]
    vnew = vc_ref[0, 0]
    lane_own = lax.broadcasted_iota(jnp.int32, (1, ls), 1)
    for i in range(ls):
        qcol = qc[:, i:i + 1]
        s_list = []
        for j in range(topk):
            ok = idx_ref[((b * ls + i) * n_heads + h) * topk + j] < n_past
            for pg in range(ppb):
                kt = k_refs[(i * topk + j) * ppb + pg][0, 0]
                s = jnp.sum(kt * qcol, axis=0, keepdims=True) * scale
                s_list.append(jnp.where(ok, s, -jnp.inf))
        s_own = jnp.sum(knew * qcol, axis=0, keepdims=True) * scale
        s_own = jnp.where(lane_own <= i, s_own, -jnp.inf)
        mx = jnp.max(s_own, axis=1, keepdims=True)
        for s in s_list:
            mx = jnp.maximum(mx, jnp.max(s, axis=1, keepdims=True))
        p_own = jnp.exp(s_own - mx)
        denom = jnp.sum(p_own, axis=1, keepdims=True)
        o = jnp.sum(vnew * p_own, axis=1, keepdims=True)
        pv = None
        for t, s in enumerate(s_list):
            p = jnp.exp(s - mx)
            denom = denom + jnp.sum(p, axis=1, keepdims=True)
            term = v_refs[i * topk * ppb + t][0, 0] * p
            pv = term if pv is None else pv + term
        o = o + jnp.sum(pv, axis=1, keepdims=True)
        o_ref[0, 0, :, i:i + 1] = o / denom


def _moba_sample(page_table, idx, q_cols, k_cols, v_cols, cacheT_k, cacheT_v, *, n_heads, hd, ls):
    bs, n_pages = page_table.shape
    page_size = cacheT_k.shape[-1]
    ppb = MOBA_BLOCK // page_size
    n_past = n_pages // ppb
    topk = MOBA_TOPK
    nslab = ls * topk * ppb

    def slab_spec(i, j, pg):
        def imap(b, h, pt, ix):
            blk = jnp.minimum(ix[((b * ls + i) * n_heads + h) * topk + j], n_past - 1)
            return (pt[b * n_pages + blk * ppb + pg], h, 0, 0)
        return pl.BlockSpec((1, 1, hd, page_size), imap)

    slabs = [slab_spec(i, j, pg) for i in range(ls) for j in range(topk) for pg in range(ppb)]
    col_spec = pl.BlockSpec((1, 1, hd, ls), lambda b, h, pt, ix: (b, h, 0, 0))
    grid_spec = pltpu.PrefetchScalarGridSpec(
        num_scalar_prefetch=2,
        grid=(bs, n_heads),
        in_specs=[col_spec, col_spec, col_spec] + slabs + slabs,
        out_specs=col_spec,
    )
    return pl.pallas_call(
        functools.partial(_moba_sample_kernel, ls=ls, topk=topk, ppb=ppb, n_heads=n_heads, hd=hd, n_past=n_past),
        grid_spec=grid_spec,
        out_shape=jax.ShapeDtypeStruct((bs, n_heads, hd, ls), F32),
        compiler_params=_cparams("parallel", "parallel"),
        name="moba_sample_attn",
    )(page_table.reshape(-1), idx, q_cols, k_cols, v_cols, *([cacheT_k] * nslab), *([cacheT_v] * nslab))


def _merge_kernel(*refs, n_in_tiles, **kw):
    i = pl.program_id(0)

    @pl.when(i < n_in_tiles)
    def _():
        _merge_body(*refs, **kw)

    @pl.when(i >= n_in_tiles)
    def _():
        hres_ref, xn_ref, slot_ref, w_ref, cnt_ref = refs[-5:]
        hres_ref[...] = jnp.zeros_like(hres_ref)
        xn_ref[...] = jnp.zeros_like(xn_ref)
        slot_ref[...] = jnp.full(slot_ref.shape, -1, jnp.int32)
        w_ref[...] = jnp.zeros_like(w_ref)
        cnt_ref[...] = jnp.zeros_like(cnt_ref)


def _merge_body(x_ref, ret_ref, rg_ref, mo_ref, rng_ref, wo_ref, n2_ref, rwT_ref, rb_ref, *refs,
                n_heads, dv, n_experts, top_k, n_valid):
    hres_ref, xn_ref, slot_ref, w_ref, cnt_ref = refs[-5:]
    tm = x_ref.shape[0]
    d_ret = n_heads * dv
    ret_o = ret_ref[...]
    parts = []
    for h in range(n_heads):
        r = ret_o[:, h * dv:(h + 1) * dv]
        mu = jnp.mean(r, axis=-1, keepdims=True)
        var = jnp.mean(jnp.square(r - mu), axis=-1, keepdims=True)
        parts.append((r - mu) * lax.rsqrt(var + EPS))
    rg = rg_ref[...]
    ret = jnp.concatenate(parts, axis=1) * rng_ref[...] * (rg * jax.nn.sigmoid(rg))
    mix = (jnp.dot(ret.astype(BF16), wo_ref[0:d_ret, :], preferred_element_type=F32)
           + jnp.dot(mo_ref[...], wo_ref[d_ret:, :], preferred_element_type=F32))
    hres = x_ref[...] + mix
    hres_ref[...] = hres
    xn = _rms(hres, n2_ref[...]).astype(BF16)
    xn_ref[...] = xn
    logits = lax.dot_general(rwT_ref[...], xn, _NT, preferred_element_type=F32) + rb_ref[...]
    row = lax.broadcasted_iota(jnp.int32, logits.shape, 0)
    work = logits
    vals, hots = [], []
    for _ in range(top_k):
        mx = jnp.max(work, axis=0, keepdims=True)
        it = jnp.min(jnp.where(work == mx, row, n_experts), axis=0, keepdims=True)
        hot = row == it
        vals.append(mx)
        hots.append(hot)
        work = jnp.where(hot, -jnp.inf, work)
    exps = [jnp.exp(v - vals[0]) for v in vals]
    denom = exps[0]
    for e in exps[1:]:
        denom = denom + e

    valid = lax.broadcasted_iota(jnp.int32, (1, tm), 1) < n_valid
    mask = hots[0]
    for hot in hots[1:]:
        mask = mask | hot
    mask = mask & valid
    maskf = mask.astype(F32)
    r_i = lax.broadcasted_iota(jnp.int32, (tm, tm), 0)
    c_i = lax.broadcasted_iota(jnp.int32, (tm, tm), 1)
    rank = jnp.dot(maskf.astype(BF16), (r_i < c_i).astype(BF16), preferred_element_type=F32)
    cnt = jnp.sum(maskf, axis=1, keepdims=True).astype(jnp.int32)
    cnt_pad = ((cnt + (SLOT_ALIGN - 1)) // SLOT_ALIGN) * SLOT_ALIGN
    e_r = lax.broadcasted_iota(jnp.int32, (n_experts, n_experts), 0)
    e_c = lax.broadcasted_iota(jnp.int32, (n_experts, n_experts), 1)
    cpb = jnp.broadcast_to(cnt_pad.astype(F32), (n_experts, LANES)).astype(BF16)
    tile_off = jnp.dot((e_c < e_r).astype(BF16), cpb, preferred_element_type=F32)[:, 0:1]
    slot = tile_off + rank

    row8 = lax.broadcasted_iota(jnp.int32, (8, tm), 0)
    slot_out = jnp.full((8, tm), -1, jnp.int32)
    w_out = jnp.zeros((8, tm), F32)
    for k in range(top_k):
        sk = jnp.sum(jnp.where(hots[k], slot, 0.0), axis=0, keepdims=True).astype(jnp.int32)
        sk = jnp.where(valid, sk, -1)
        slot_out = jnp.where(row8 == k, sk, slot_out)
        w_out = jnp.where(row8 == k, exps[k] / denom, w_out)
    slot_ref[...] = slot_out
    w_ref[...] = w_out
    cnt_ref[0] = jnp.broadcast_to(cnt, (n_experts, LANES))


def _merge(x, ret_o, rg, moba_o, ret_norm_g, w_out, norm2_g, rwT, router_b, prev, *,
           n_heads, dv, tm, n_tiles_total, tile0, n_valid, n_clear_tiles=0):
    t, d_model = x.shape
    n_experts = rwT.shape[0]
    t_pad = n_tiles_total * tm
    n_in = t // tm
    tok_in = lambda n: pl.BlockSpec((tm, n), lambda i: (jnp.minimum(i, n_in - 1), 0))
    tok = lambda n: pl.BlockSpec((tm, n), lambda i: (tile0 + i, 0))
    full = lambda a: pl.BlockSpec(a.shape, lambda i: (0,) * a.ndim)
    out_shape = [jax.ShapeDtypeStruct((t_pad, d_model), F32), jax.ShapeDtypeStruct((t_pad, d_model), BF16),
                 jax.ShapeDtypeStruct((8, t_pad), jnp.int32), jax.ShapeDtypeStruct((8, t_pad), F32),
                 jax.ShapeDtypeStruct((n_tiles_total, n_experts, LANES), jnp.int32)]
    out_specs = [tok(d_model), tok(d_model),
                 pl.BlockSpec((8, tm), lambda i: (0, tile0 + i)), pl.BlockSpec((8, tm), lambda i: (0, tile0 + i)),
                 pl.BlockSpec((1, n_experts, LANES), lambda i: (tile0 + i, 0, 0))]
    args = [x, ret_o, rg, moba_o, ret_norm_g, w_out, norm2_g, rwT, router_b]
    in_specs = [tok_in(d_model), tok_in(n_heads * dv), tok_in(n_heads * dv), tok_in(moba_o.shape[1]),
                full(ret_norm_g), full(w_out), full(norm2_g), full(rwT), full(router_b)]
    aliases = {}
    if prev is not None:
        aliases = {len(args) + k: k for k in range(len(prev))}
        args += list(prev)
        in_specs += [pl.BlockSpec(memory_space=pl.ANY)] * len(prev)
    return pl.pallas_call(
        functools.partial(_merge_kernel, n_heads=n_heads, dv=dv, n_experts=n_experts, top_k=TOP_K,
                          n_valid=n_valid, n_in_tiles=n_in),
        grid=(n_in + n_clear_tiles,),
        in_specs=in_specs,
        out_specs=out_specs,
        out_shape=out_shape,
        input_output_aliases=aliases,
        compiler_params=_cparams("parallel"),
        name="merge_router",
    )(*args)


def _run_copies(cp_ref, to_ref, dr_ref, t, n_experts, make_copy):
    def per_expert(e, total):
        n = cp_ref[t * n_experts + e] // SLOT_ALIGN
        src0 = to_ref[t * n_experts + e]
        dst0 = dr_ref[t * n_experts + e]

        def one(j, c):
            make_copy(pl.multiple_of(src0 + j * SLOT_ALIGN, SLOT_ALIGN),
                      pl.multiple_of(dst0 + j * SLOT_ALIGN, SLOT_ALIGN)).start()
            return c

        lax.fori_loop(0, n, one, 0)
        return total + n

    return lax.fori_loop(0, n_experts, per_expert, 0)


def _dispatch_kernel(cp_ref, to_ref, dr_ref, xn_ref, slot_ref, xs_in_ref, xs_ref, slots_scr, sem,
                     *, n_experts, n_slots, chunk, top_k):
    del xs_in_ref
    t = pl.program_id(0)
    tm = xn_ref.shape[0]
    xn = xn_ref[...]
    sl = slot_ref[...]
    for c in range(n_slots // chunk):
        s_iota = c * chunk + lax.broadcasted_iota(jnp.int32, (chunk, tm), 0)
        pm = sl[0:1, :] == s_iota
        for k in range(1, top_k):
            pm = pm | (sl[k:k + 1, :] == s_iota)
        slots_scr[c * chunk:(c + 1) * chunk, :] = jnp.dot(
            pm.astype(BF16), xn, preferred_element_type=F32).astype(BF16)

    def make_copy(src, dst):
        return pltpu.make_async_copy(slots_scr.at[pl.ds(src, SLOT_ALIGN)], xs_ref.at[pl.ds(dst, SLOT_ALIGN)], sem)

    total = _run_copies(cp_ref, to_ref, dr_ref, t, n_experts, make_copy)

    def wait_one(j, c):
        make_copy(0, 0).wait()
        return c

    lax.fori_loop(0, total, wait_one, 0)


def _dispatch(cp, to, dr, xn_all, slot, xs0, *, tm, n_experts, n_slots):
    t_pad, d_model = xn_all.shape
    grid_spec = pltpu.PrefetchScalarGridSpec(
        num_scalar_prefetch=3,
        grid=(t_pad // tm,),
        in_specs=[pl.BlockSpec((tm, d_model), lambda i, *_: (i, 0)),
                  pl.BlockSpec((8, tm), lambda i, *_: (0, i)),
                  pl.BlockSpec(memory_space=pl.ANY)],
        out_specs=pl.BlockSpec(memory_space=pl.ANY),
        scratch_shapes=[pltpu.VMEM((n_slots, d_model), BF16), pltpu.SemaphoreType.DMA(())],
    )
    return pl.pallas_call(
        functools.partial(_dispatch_kernel, n_experts=n_experts, n_slots=n_slots, chunk=256, top_k=TOP_K),
        grid_spec=grid_spec,
        out_shape=jax.ShapeDtypeStruct(xs0.shape, xs0.dtype),
        input_output_aliases={5: 0},
        compiler_params=_cparams("arbitrary"),
        name="moe_dispatch",
    )(cp, to, dr, xn_all, slot, xs0)


def _expert_kernel(te_ref, tv_ref, x_ref, wgu_ref, bgu_ref, wd_ref, bd_ref, y_ref, wp_scr, wdb_scr, *, d_ff):
    j = pl.program_id(0)
    grp = 2 * LANES
    n_grp = 2 * d_ff // grp

    @pl.when(tv_ref[j] > 0)
    def _():
        first = (j == 0) | (te_ref[j] != te_ref[jnp.maximum(j - 1, 0)])

        @pl.when(first)
        def _():
            r = lax.broadcasted_iota(jnp.int32, (grp, grp), 0)
            c = lax.broadcasted_iota(jnp.int32, (grp, grp), 1)
            perm = (((c < LANES) & (r == 2 * c)) | ((c >= LANES) & (r == 2 * (c - LANES) + 1))).astype(BF16)
            for g in range(n_grp):
                wp_scr[:, g * grp:(g + 1) * grp] = jnp.dot(
                    wgu_ref[0, :, g * grp:(g + 1) * grp].astype(BF16), perm, preferred_element_type=F32).astype(BF16)
            wdb_scr[...] = wd_ref[0].astype(BF16)

        x = x_ref[...]
        parts = []
        for g in range(n_grp):
            u = jnp.dot(x, wp_scr[:, g * grp:(g + 1) * grp], preferred_element_type=F32) + bgu_ref[0, :, g * grp:(g + 1) * grp]
            glu = jnp.minimum(u[:, :LANES], SWIGLU_LIMIT)
            lin = jnp.clip(u[:, LANES:], -SWIGLU_LIMIT, SWIGLU_LIMIT)
            parts.append((glu * jax.nn.sigmoid(SWIGLU_ALPHA * glu) * (lin + 1.0)).astype(BF16))
        a = jnp.concatenate(parts, axis=1)
        y_ref[...] = (jnp.dot(a, wdb_scr[...], preferred_element_type=F32) + bd_ref[0]).astype(BF16)

    @pl.when(tv_ref[j] == 0)
    def _():
        y_ref[...] = jnp.zeros_like(y_ref)


def _experts(te, tv, x_sorted, wgu, bgu_perm, wd, bd, *, tme):
    p_rows, d_model = x_sorted.shape
    n_experts, _, d_ff2 = wgu.shape
    d_ff = d_ff2 // 2
    ex = lambda a: pl.BlockSpec((1,) + a.shape[1:], lambda j, te_, tv_: (te_[j],) + (0,) * (a.ndim - 1))
    grid_spec = pltpu.PrefetchScalarGridSpec(
        num_scalar_prefetch=2,
        grid=(p_rows // tme,),
        in_specs=[pl.BlockSpec((tme, d_model), lambda j, *_: (j, 0)), ex(wgu), ex(bgu_perm), ex(wd), ex(bd)],
        out_specs=pl.BlockSpec((tme, d_model), lambda j, *_: (j, 0)),
        scratch_shapes=[pltpu.VMEM((d_model, d_ff2), BF16), pltpu.VMEM((d_ff, d_model), BF16)],
    )
    return pl.pallas_call(
        functools.partial(_expert_kernel, d_ff=d_ff),
        grid_spec=grid_spec,
        out_shape=jax.ShapeDtypeStruct((p_rows, d_model), BF16),
        compiler_params=_cparams("arbitrary"),
        name="moe_experts",
    )(te, tv, x_sorted, wgu, bgu_perm, wd, bd)


def _combine_kernel(cp_ref, to_ref, dr_ref, hres_ref, slotT_ref, wT_ref, fg_ref, ys_ref, yp_ref, ysm_ref,
                    slots_scr, sem, *, n_experts, n_slots, chunk, top_k, n_prompt_tiles):
    t = pl.program_id(0)
    tm, d_model = hres_ref.shape

    def make_copy(src, dst):
        return pltpu.make_async_copy(ys_ref.at[pl.ds(dst, SLOT_ALIGN)], slots_scr.at[pl.ds(src, SLOT_ALIGN)], sem)

    total = _run_copies(cp_ref, to_ref, dr_ref, t, n_experts, make_copy)

    used = to_ref[t * n_experts + n_experts - 1] + cp_ref[t * n_experts + n_experts - 1]

    def zero_one(j, c):
        slots_scr[pl.ds(pl.multiple_of(used + j * SLOT_ALIGN, SLOT_ALIGN), SLOT_ALIGN), :] = jnp.zeros(
            (SLOT_ALIGN, d_model), BF16)
        return c

    lax.fori_loop(0, (n_slots - used) // SLOT_ALIGN, zero_one, 0)

    def wait_one(j, c):
        make_copy(0, 0).wait()
        return c

    lax.fori_loop(0, total, wait_one, 0)

    sl = slotT_ref[...]
    wt = wT_ref[...]
    acc = hres_ref[...]
    for c in range(n_slots // chunk):
        s_iota = c * chunk + lax.broadcasted_iota(jnp.int32, (tm, chunk), 1)
        pw = jnp.where(sl[:, 0:1] == s_iota, wt[:, 0:1], 0.0)
        for k in range(1, top_k):
            pw = pw + jnp.where(sl[:, k:k + 1] == s_iota, wt[:, k:k + 1], 0.0)
        acc = acc + jnp.dot(pw.astype(BF16), slots_scr[c * chunk:(c + 1) * chunk, :], preferred_element_type=F32)
    y = _rms(acc, fg_ref[...])

    @pl.when(t < n_prompt_tiles)
    def _():
        yp_ref[...] = y

    @pl.when(t >= n_prompt_tiles)
    def _():
        ysm_ref[...] = y[:ysm_ref.shape[0], :]


def _combine(cp, to, dr, hres_all, slotT, wT, fg, y_sorted, *, tm, n_experts, n_slots, n_prompt_tiles, n_sample):
    t_pad, d_model = hres_all.shape
    npt = n_prompt_tiles
    grid_spec = pltpu.PrefetchScalarGridSpec(
        num_scalar_prefetch=3,
        grid=(t_pad // tm,),
        in_specs=[pl.BlockSpec((tm, d_model), lambda i, *_: (i, 0)),
                  pl.BlockSpec((tm, 8), lambda i, *_: (i, 0)),
                  pl.BlockSpec((tm, 8), lambda i, *_: (i, 0)),
                  pl.BlockSpec(fg.shape, lambda i, *_: (0, 0)),
                  pl.BlockSpec(memory_space=pl.ANY)],
        out_specs=[pl.BlockSpec((tm, d_model), lambda i, *_: (jnp.minimum(i, npt - 1), 0)),
                   pl.BlockSpec((n_sample, d_model), lambda i, *_: (0, 0))],
        scratch_shapes=[pltpu.VMEM((n_slots, d_model), BF16), pltpu.SemaphoreType.DMA(())],
    )
    return pl.pallas_call(
        functools.partial(_combine_kernel, n_experts=n_experts, n_slots=n_slots, chunk=256, top_k=TOP_K,
                          n_prompt_tiles=npt),
        grid_spec=grid_spec,
        out_shape=[jax.ShapeDtypeStruct((npt * tm, d_model), F32), jax.ShapeDtypeStruct((n_sample, d_model), F32)],
        compiler_params=_cparams("arbitrary"),
        name="moe_combine",
    )(cp, to, dr, hres_all, slotT, wT, fg, y_sorted)


def _route_tables(cnt, *, tme, n_row_tiles):
    n_tiles, n_experts = cnt.shape
    cnt_pad = (cnt + (SLOT_ALIGN - 1)) // SLOT_ALIGN * SLOT_ALIGN
    tile_off = jnp.cumsum(cnt_pad, axis=1) - cnt_pad
    tot = jnp.sum(cnt_pad, axis=0)
    tot_t = (tot + (tme - 1)) // tme * tme
    ends = jnp.cumsum(tot_t)
    base = ends - tot_t
    dst_row = base[None, :] + jnp.cumsum(cnt_pad, axis=0) - cnt_pad
    j = jnp.arange(n_row_tiles, dtype=jnp.int32)
    te = jnp.minimum(jnp.searchsorted(ends // tme, j, side="right"), n_experts - 1).astype(jnp.int32)
    tv = (j < ends[-1] // tme).astype(jnp.int32)
    flat = lambda a: a.reshape(-1).astype(jnp.int32)
    return flat(cnt_pad), flat(tile_off), flat(dst_row), te, tv


def kernel(x_prompt, x_sample, cache_k, cache_v, state_ret, page_table, norm1_g, w_in, ret_norm_g, w_out,
           norm2_g, router_w, router_b, w_gate_up, b_gate_up, w_down, b_down, final_norm_g):
    bp, lp, d_model = x_prompt.shape
    bs, ls, _ = x_sample.shape
    depth = w_in.shape[0]
    assert depth == 1, "single-layer step"
    n_pages = page_table.shape[1]
    page_size, mh, hd = cache_k.shape[2], cache_k.shape[3], cache_k.shape[4]
    rh, dk, dv = state_ret.shape[2], state_ret.shape[3], state_ret.shape[4]
    d_ret = rh * dk
    d_moba = mh * hd
    n_experts = router_w.shape[2]
    d_ff = w_down.shape[2]
    past_len = n_pages * page_size
    layer = 0

    w = w_in[layer]
    w_all = w.astype(BF16)
    w_main = jnp.concatenate([w[:, :4 * d_ret + d_moba], w[:, 4 * d_ret + 2 * d_moba:]], axis=1).astype(BF16)
    wkvT = w[:, 4 * d_ret + d_moba:].T.astype(BF16)
    g1 = norm1_g[layer][None, :]
    wo = w_out[layer].astype(BF16)
    rng = ret_norm_g[layer][None, :]
    n2 = norm2_g[layer][None, :]
    rwT = router_w[layer].T.astype(BF16)
    rb = router_b[layer][:, None]
    wgu = w_gate_up[layer]
    bgu_perm = b_gate_up[layer].reshape(n_experts, -1, LANES, 2).transpose(0, 1, 3, 2).reshape(n_experts, 1, 2 * d_ff)
    wd = w_down[layer]
    bd = b_down[layer][:, None, :]
    fg = final_norm_g[None, :]

    tm = _token_tile(lp)
    pos_p = jnp.arange(lp, dtype=jnp.int32)
    cos_p, sin_p = _rope_tables(pos_p, dk)
    xp = x_prompt.reshape(bp * lp, d_model)
    rq, rk, rv, rg, mq, mv, kT, vT = _inproj_prompt(
        xp, g1, w_main, wkvT, cos_p, sin_p, batch=bp, seq=lp, d_ret=d_ret, d_moba=d_moba, n_ret_heads=rh, tm=tm)
    ret_o, s_p = _ret_prompt(rq, rk, rv, batch=bp, seq=lp, n_heads=rh, dk=dk, dv=dv)
    moba_o = _moba_prompt(mq, mv, kT, batch=bp, seq=lp, n_heads=mh, hd=hd)
    n_sample = bs * ls
    assert n_sample <= tm, "sample group must fit one token tile"
    npt = bp * lp // tm
    n_tiles = npt + 1
    bufs = _merge(xp, ret_o, rg, moba_o, rng, wo, n2, rwT, rb, None,
                  n_heads=rh, dv=dv, tm=tm, n_tiles_total=n_tiles, tile0=0, n_valid=tm, n_clear_tiles=1)

    pos_s = past_len + jnp.arange(ls, dtype=jnp.int32)
    cos_s, sin_s = _rope_tables(jnp.tile(pos_s, bs), dk)
    xs = x_sample.reshape(bs * ls, d_model)
    rq_s, rk_s, rv_s, rg_s, mq_s, mk_s, mv_s = _inproj_sample(
        xs, g1, w_all, cos_s, sin_s, d_ret=d_ret, d_moba=d_moba, n_ret_heads=rh)
    r3 = lambda a: a.reshape(bs, ls, a.shape[1])
    ret_o_s, s_s = _ret_sample(r3(rq_s), r3(rk_s), r3(rv_s), state_ret[layer], n_heads=rh, dk=dk, dv=dv)
    ckT = cache_k[layer].transpose(0, 2, 3, 1)
    cvT = cache_v[layer].transpose(0, 2, 3, 1)
    idx_pad = _moba_gate(page_table, r3(mq_s), ckT, n_heads=mh, hd=hd, ls=ls, pages_per_step=16)
    idx = idx_pad[:, :, :MOBA_TOPK].reshape(-1)
    cols = lambda a: a.reshape(bs, ls, mh, hd).transpose(0, 2, 3, 1)
    o_cols = _moba_sample(page_table, idx, cols(mq_s), cols(mk_s), cols(mv_s), ckT, cvT,
                          n_heads=mh, hd=hd, ls=ls)
    moba_o_s = o_cols.transpose(0, 3, 1, 2).reshape(bs * ls, d_moba).astype(BF16)
    padt = lambda a: jnp.pad(a, ((0, tm - n_sample), (0, 0)))
    hres_all, xn_all, slot, wts, cnt = _merge(
        padt(xs), padt(ret_o_s.reshape(n_sample, d_ret)), padt(rg_s), padt(moba_o_s), rng, wo, n2, rwT, rb, bufs,
        n_heads=rh, dv=dv, tm=tm, n_tiles_total=n_tiles, tile0=npt, n_valid=n_sample)

    tme = 512
    n_slots = -(-(tm * TOP_K + n_experts * (SLOT_ALIGN - 1)) // 256) * 256
    n_assign = TOP_K * (bp * lp + n_sample)
    n_row_tiles = -(-(n_assign + (SLOT_ALIGN - 1) * n_experts * n_tiles + n_experts * (tme - 1)) // tme)
    cp, to, dr, te, tv = _route_tables(cnt[:, :, 0], tme=tme, n_row_tiles=n_row_tiles)
    xs0 = jnp.zeros((n_row_tiles * tme, d_model), BF16)
    x_sorted = _dispatch(cp, to, dr, xn_all, slot, xs0, tm=tm, n_experts=n_experts, n_slots=n_slots)
    y_sorted = _experts(te, tv, x_sorted, wgu, bgu_perm, wd, bd, tme=tme)
    y_p, y_s = _combine(cp, to, dr, hres_all, slot.T, wts.T, fg, y_sorted, tm=tm, n_experts=n_experts,
                        n_slots=n_slots, n_prompt_tiles=npt, n_sample=n_sample)

    y_prompt = y_p.reshape(bp, lp, d_model)
    y_sample = y_s.reshape(bs, ls, d_model)
    k_prompt = kT.reshape(bp, mh, hd, lp).transpose(0, 3, 1, 2)[None]
    v_prompt = vT.reshape(bp, mh, hd, lp).transpose(0, 3, 1, 2)[None]
    k_sample = mk_s.reshape(1, bs, ls, mh, hd)
    v_sample = mv_s.reshape(1, bs, ls, mh, hd)
    return (y_prompt, y_sample, k_prompt, v_prompt, s_p[None], k_sample, v_sample, s_s[None])
```

```python
import functools

import jax
import jax.numpy as jnp
from jax import lax
from jax.experimental import pallas as pl
from jax.experimental.pallas import tpu as pltpu

F32 = jnp.float32
BF16 = jnp.bfloat16

EPS = 1e-5
ROPE_BASE = 10000.0
RET_CHUNK = 128
MOBA_BLOCK = 256
MOBA_TOPK = 3
TOP_K = 4
SWIGLU_LIMIT = 7.0
SWIGLU_ALPHA = 1.702
NEG = -1e30
LANES = 128
SLOT_ALIGN = 16
VMEM_LIMIT = 56 * 1024 * 1024

_NT = (((1,), (1,)), ((), ()))


def _cparams(*sem):
    return pltpu.CompilerParams(dimension_semantics=sem, vmem_limit_bytes=VMEM_LIMIT)


def _token_tile(seq):
    for tm in (512, 256, 128):
        if seq % tm == 0:
            return tm
    raise ValueError(f"sequence length {seq} must be a multiple of 128")


def _rms(x, g):
    return x * lax.rsqrt(jnp.mean(x * x, axis=-1, keepdims=True) + EPS) * g


def _rope_tables(pos, dk):
    half = dk // 2
    inv = ROPE_BASE ** (-jnp.arange(half, dtype=F32) / half)
    ang = pos.astype(F32)[:, None] * inv[None, :]
    c = jnp.cos(ang)
    s = jnp.sin(ang)
    return jnp.concatenate([c, c], axis=1), jnp.concatenate([-s, s], axis=1)


def _rotary(z, cos2, sin2, n_heads, dk):
    outs = []
    for h in range(n_heads):
        s = z[:, h * dk:(h + 1) * dk]
        outs.append(s * cos2 + pltpu.roll(s, dk // 2, 1) * sin2)
    return jnp.concatenate(outs, axis=1)


def _inproj_prompt_kernel(x_ref, g_ref, w_ref, wkvT_ref, cos_ref, sin_ref,
                          rq_ref, rk_ref, rv_ref, rg_ref, mq_ref, mv_ref, kT_ref, vT_ref,
                          *, d_ret, d_moba, n_ret_heads):
    dk = d_ret // n_ret_heads
    xn = _rms(x_ref[...], g_ref[...]).astype(BF16)
    cos2 = cos_ref[...]
    sin2 = sin_ref[...]

    def proj(c0, n):
        return jnp.dot(xn, w_ref[:, c0:c0 + n], preferred_element_type=F32)

    rq_ref[...] = _rotary(proj(0, d_ret), cos2, sin2, n_ret_heads, dk)
    rk_ref[...] = _rotary(proj(d_ret, d_ret), cos2, sin2, n_ret_heads, dk) * (dk ** -0.5)
    rv_ref[...] = proj(2 * d_ret, d_ret)
    rg_ref[...] = proj(3 * d_ret, d_ret)
    mq_ref[...] = proj(4 * d_ret, d_moba).astype(BF16)
    mv_ref[...] = proj(4 * d_ret + d_moba, d_moba).astype(BF16)
    kT_ref[0] = lax.dot_general(wkvT_ref[0:d_moba, :], xn, _NT, preferred_element_type=F32)
    vT_ref[0] = lax.dot_general(wkvT_ref[d_moba:2 * d_moba, :], xn, _NT, preferred_element_type=F32)


def _inproj_prompt(x, g, w_main, wkvT, cos2, sin2, *, batch, seq, d_ret, d_moba, n_ret_heads, tm):
    d_model = x.shape[1]
    nt = seq // tm
    dk = d_ret // n_ret_heads
    tok_spec = lambda n: pl.BlockSpec((tm, n), lambda b, i: (b * nt + i, 0))
    full = lambda a: pl.BlockSpec(a.shape, lambda b, i: (0,) * a.ndim)
    t = batch * seq
    out_shape = (
        [jax.ShapeDtypeStruct((t, d_ret), F32)] * 4
        + [jax.ShapeDtypeStruct((t, d_moba), BF16)] * 2
        + [jax.ShapeDtypeStruct((batch, d_moba, seq), F32)] * 2
    )
    out_specs = (
        [tok_spec(d_ret)] * 4 + [tok_spec(d_moba)] * 2
        + [pl.BlockSpec((1, d_moba, tm), lambda b, i: (b, 0, i))] * 2
    )
    return pl.pallas_call(
        functools.partial(_inproj_prompt_kernel, d_ret=d_ret, d_moba=d_moba, n_ret_heads=n_ret_heads),
        grid=(batch, nt),
        in_specs=[tok_spec(d_model), full(g), full(w_main), full(wkvT),
                  pl.BlockSpec((tm, dk), lambda b, i: (i, 0)), pl.BlockSpec((tm, dk), lambda b, i: (i, 0))],
        out_specs=out_specs,
        out_shape=out_shape,
        compiler_params=_cparams("parallel", "arbitrary"),
        name="inproj_prompt",
    )(x, g, w_main, wkvT, cos2, sin2)


def _inproj_sample_kernel(x_ref, g_ref, w_ref, cos_ref, sin_ref,
                          rq_ref, rk_ref, rv_ref, rg_ref, mq_ref, mk_ref, mv_ref,
                          *, d_ret, d_moba, n_ret_heads):
    dk = d_ret // n_ret_heads
    xn = _rms(x_ref[...], g_ref[...]).astype(BF16)
    cos2 = cos_ref[...]
    sin2 = sin_ref[...]

    def proj(c0, n):
        return jnp.dot(xn, w_ref[:, c0:c0 + n], preferred_element_type=F32)

    rq_ref[...] = _rotary(proj(0, d_ret), cos2, sin2, n_ret_heads, dk)
    rk_ref[...] = _rotary(proj(d_ret, d_ret), cos2, sin2, n_ret_heads, dk) * (dk ** -0.5)
    rv_ref[...] = proj(2 * d_ret, d_ret)
    rg_ref[...] = proj(3 * d_ret, d_ret)
    mq_ref[...] = proj(4 * d_ret, d_moba)
    mk_ref[...] = proj(4 * d_ret + d_moba, d_moba)
    mv_ref[...] = proj(4 * d_ret + 2 * d_moba, d_moba)


def _inproj_sample(x, g, w_all, cos2, sin2, *, d_ret, d_moba, n_ret_heads):
    t = x.shape[0]
    args = (x, g, w_all, cos2, sin2)
    full = lambda a: pl.BlockSpec(a.shape, lambda i: (0,) * a.ndim)
    out_shape = [jax.ShapeDtypeStruct((t, d_ret), F32)] * 4 + [jax.ShapeDtypeStruct((t, d_moba), F32)] * 3
    return pl.pallas_call(
        functools.partial(_inproj_sample_kernel, d_ret=d_ret, d_moba=d_moba, n_ret_heads=n_ret_heads),
        grid=(1,),
        in_specs=[full(a) for a in args],
        out_specs=[pl.BlockSpec(s.shape, lambda i: (0, 0)) for s in out_shape],
        out_shape=out_shape,
        compiler_params=_cparams("arbitrary"),
        name="inproj_sample",
    )(*args)


def _ret_tables(n_heads, chunk):
    log_g = jnp.log1p(-jnp.exp2(-5.0 - jnp.arange(n_heads, dtype=F32)))
    n = jnp.arange(chunk, dtype=F32)
    rel = n[:, None] - n[None, :]
    causal = rel >= 0
    decay_in = jnp.where(causal[None], jnp.exp(jnp.where(causal, rel, 0.0)[None] * log_g[:, None, None]), 0.0)
    q_dec = jnp.exp((n[:, None] + 1.0) * log_g[None, :])
    k_dec = jnp.exp((chunk - 1.0 - n[:, None]) * log_g[None, :])
    chunk_dec = jnp.exp(chunk * log_g)
    return decay_in, q_dec, k_dec, chunk_dec


def _ret_prompt_kernel(rq_ref, rk_ref, rv_ref, din_ref, qdec_ref, kdec_ref, cdec_ref,
                       o_ref, s_out_ref, s_scr, *, n_heads, dk, dv, n_chunks):
    c = pl.program_id(1)

    @pl.when(c == 0)
    def _():
        s_scr[...] = jnp.zeros_like(s_scr)

    for h in range(n_heads):
        q = rq_ref[:, h * dk:(h + 1) * dk]
        k = rk_ref[:, h * dk:(h + 1) * dk]
        v = rv_ref[:, h * dv:(h + 1) * dv].astype(BF16)
        s = s_scr[h]
        scores = lax.dot_general(q.astype(BF16), k.astype(BF16), _NT, preferred_element_type=F32) * din_ref[h]
        inner = jnp.dot(scores.astype(BF16), v, preferred_element_type=F32)
        cross = jnp.dot((q * qdec_ref[h]).astype(BF16), s.astype(BF16), preferred_element_type=F32)
        o_ref[:, h * dv:(h + 1) * dv] = inner + cross
        kdT = (k * kdec_ref[h]).T.astype(BF16)
        s_scr[h] = cdec_ref[h] * s + jnp.dot(kdT, v, preferred_element_type=F32)

    @pl.when(c == n_chunks - 1)
    def _():
        s_out_ref[0] = s_scr[...]


def _ret_prompt(rq, rk, rv, *, batch, seq, n_heads, dk, dv):
    chunk = RET_CHUNK
    nc = seq // chunk
    decay_in, q_dec, k_dec, chunk_dec = _ret_tables(n_heads, chunk)
    qdec_b = jnp.broadcast_to(q_dec.T[:, :, None], (n_heads, chunk, dk))
    kdec_b = jnp.broadcast_to(k_dec.T[:, :, None], (n_heads, chunk, dk))
    cdec_b = jnp.broadcast_to(chunk_dec[:, None, None], (n_heads, dk, dv))
    tok = lambda n: pl.BlockSpec((chunk, n), lambda b, c: (b * nc + c, 0))
    full = lambda a: pl.BlockSpec(a.shape, lambda b, c: (0,) * a.ndim)
    return pl.pallas_call(
        functools.partial(_ret_prompt_kernel, n_heads=n_heads, dk=dk, dv=dv, n_chunks=nc),
        grid=(batch, nc),
        in_specs=[tok(n_heads * dk), tok(n_heads * dk), tok(n_heads * dv),
                  full(decay_in), full(qdec_b), full(kdec_b), full(cdec_b)],
        out_specs=[tok(n_heads * dv), pl.BlockSpec((1, n_heads, dk, dv), lambda b, c: (b, 0, 0, 0))],
        out_shape=[jax.ShapeDtypeStruct((batch * seq, n_heads * dv), F32),
                   jax.ShapeDtypeStruct((batch, n_heads, dk, dv), F32)],
        scratch_shapes=[pltpu.VMEM((n_heads, dk, dv), F32)],
        compiler_params=_cparams("parallel", "arbitrary"),
        name="retention_prompt",
    )(rq, rk, rv, decay_in, qdec_b, kdec_b, cdec_b)


def _ret_sample_kernel(rq_ref, rk_ref, rv_ref, s_ref, din_ref, qdec_ref, kdec_ref, cdec_ref,
                       o_ref, s_out_ref, *, n_heads, dk, dv):
    row = lax.broadcasted_iota(jnp.int32, (dk, dk), 0)
    col = lax.broadcasted_iota(jnp.int32, (dk, dk), 1)
    eye = (row == col).astype(BF16)
    for h in range(n_heads):
        q = rq_ref[0, :, h * dk:(h + 1) * dk]
        k = rk_ref[0, :, h * dk:(h + 1) * dk]
        v = rv_ref[0, :, h * dv:(h + 1) * dv].astype(BF16)
        s = s_ref[0, h]
        scores = lax.dot_general(q.astype(BF16), k.astype(BF16), _NT, preferred_element_type=F32) * din_ref[h]
        inner = jnp.dot(scores.astype(BF16), v, preferred_element_type=F32)
        cross = jnp.dot((q * qdec_ref[h]).astype(BF16), s.astype(BF16), preferred_element_type=F32)
        o_ref[0, :, h * dv:(h + 1) * dv] = inner + cross
        kd = (k * kdec_ref[h]).astype(BF16)
        kdT = lax.dot_general(eye, kd, _NT, preferred_element_type=F32).astype(BF16)
        s_out_ref[0, h] = cdec_ref[h] * s + jnp.dot(kdT, v, preferred_element_type=F32)


def _ret_sample(rq, rk, rv, state, *, n_heads, dk, dv):
    bs, ls, _ = rq.shape
    decay_in, q_dec, k_dec, chunk_dec = _ret_tables(n_heads, ls)
    qdec_b = jnp.broadcast_to(q_dec.T[:, :, None], (n_heads, ls, dk))
    kdec_b = jnp.broadcast_to(k_dec.T[:, :, None], (n_heads, ls, dk))
    cdec_b = jnp.broadcast_to(chunk_dec[:, None, None], (n_heads, dk, dv))
    tok = lambda n: pl.BlockSpec((1, ls, n), lambda b: (b, 0, 0))
    full = lambda a: pl.BlockSpec(a.shape, lambda b: (0,) * a.ndim)
    st = pl.BlockSpec((1, n_heads, dk, dv), lambda b: (b, 0, 0, 0))
    return pl.pallas_call(
        functools.partial(_ret_sample_kernel, n_heads=n_heads, dk=dk, dv=dv),
        grid=(bs,),
        in_specs=[tok(n_heads * dk), tok(n_heads * dk), tok(n_heads * dv), st,
                  full(decay_in), full(qdec_b), full(kdec_b), full(cdec_b)],
        out_specs=[tok(n_heads * dv), st],
        out_shape=[jax.ShapeDtypeStruct((bs, ls, n_heads * dv), F32),
                   jax.ShapeDtypeStruct((bs, n_heads, dk, dv), F32)],
        compiler_params=_cparams("parallel"),
        name="retention_sample",
    )(rq, rk, rv, state, decay_in, qdec_b, kdec_b, cdec_b)


def _moba_prompt_kernel(q_ref, kTd_ref, vd_ref, kT_ref, v_ref, eneg_ref, o_ref, km_scr, kaug_scr,
                        *, blk, hd, n_blocks, topk):
    c = pl.program_id(2)
    width = 2 * hd
    scale = hd ** -0.5

    @pl.when(c == 0)
    def _():
        lane = lax.broadcasted_iota(jnp.int32, (width, LANES), 1)
        km = jnp.zeros((width, LANES), F32)
        for n in range(n_blocks):
            s = jnp.sum(kT_ref[0, :, n * blk:(n + 1) * blk], axis=1, keepdims=True) * (1.0 / blk)
            km = jnp.where(lane == n, s, km)
        km_scr[...] = km.T
        kaug_scr[0:width, :] = kT_ref[0].astype(BF16)
        kaug_scr[width:width + LANES, :] = eneg_ref[...]

    q = q_ref[...]
    lane_q = lax.broadcasted_iota(jnp.int32, (blk, width), 1)
    row_s = lax.broadcasted_iota(jnp.int32, (blk, blk), 0)
    col_s = lax.broadcasted_iota(jnp.int32, (blk, blk), 1)
    kmbT = km_scr[...].astype(BF16)
    kd = kTd_ref[0].astype(BF16)
    vd = vd_ref[...]

    nbp = -(-n_blocks // 8) * 8
    row_b = lax.broadcasted_iota(jnp.int32, (nbp, blk), 0)
    eye = (row_s == col_s).astype(BF16)
    heads = []
    for j in range(2):
        qh = jnp.where((lane_q // hd) == j, q, jnp.zeros_like(q))
        gate = lax.dot_general(kmbT, qh, _NT, preferred_element_type=F32)[0:nbp, :]
        gate = jnp.where(row_b < c, gate, NEG)
        cnt = jnp.zeros((nbp, blk), jnp.int32)
        for m in range(n_blocks):
            gm = gate[m:m + 1, :]
            beats = (gm > gate) | ((gm == gate) & (m < row_b))
            cnt = cnt + beats.astype(jnp.int32)
        sel = (row_b < c) & (cnt < topk)
        penT = jnp.where(sel | (row_b >= n_blocks), 0.0, 1.0).astype(BF16)
        penT = jnp.concatenate([penT, jnp.zeros((LANES - nbp, blk), BF16)], axis=0)
        pen = lax.dot_general(eye, penT, _NT, preferred_element_type=F32).astype(BF16)
        qs = qh * scale
        s_d = jnp.dot(qs, kd, preferred_element_type=F32)
        s_d = jnp.where(col_s <= row_s, s_d, -jnp.inf)
        heads.append((jnp.concatenate([qs, pen], axis=1), s_d, jnp.max(s_d, axis=1, keepdims=True)))

    def finish(n_wide):
        outs = []
        for qa, s_d, m_d in heads:
            if n_wide:
                w = n_wide * blk
                s_w = jnp.dot(qa, kaug_scr[:, :w], preferred_element_type=F32)
                mx = jnp.maximum(m_d, jnp.max(s_w, axis=1, keepdims=True))
                p_w = jnp.exp(s_w - mx)
                p_d = jnp.exp(s_d - mx)
                den = jnp.sum(p_d, axis=1, keepdims=True) + jnp.sum(p_w, axis=1, keepdims=True)
                acc = (jnp.dot(p_d.astype(BF16), vd, preferred_element_type=F32)
                       + jnp.dot(p_w.astype(BF16), v_ref[0:w, :], preferred_element_type=F32))
            else:
                p_d = jnp.exp(s_d - m_d)
                den = jnp.sum(p_d, axis=1, keepdims=True)
                acc = jnp.dot(p_d.astype(BF16), vd, preferred_element_type=F32)
            outs.append(acc / den)
        o_ref[...] = jnp.where(lane_q < hd, outs[0], outs[1]).astype(BF16)

    widths = sorted({w for w in (2, 4) if w < n_blocks - 1} | ({n_blocks - 1} if n_blocks > 1 else set()))
    lo = 1
    for n_wide in widths:
        @pl.when((c >= lo) & (c <= n_wide))
        def _(n_wide=n_wide):
            finish(n_wide)
        lo = n_wide + 1

    @pl.when(c == 0)
    def _():
        finish(0)


def _moba_prompt(mq, mv, kT, *, batch, seq, n_heads, hd):
    blk = MOBA_BLOCK
    nb = seq // blk
    width = 2 * hd
    npair = n_heads // 2
    assert nb <= LANES and width == LANES
    eneg = jnp.where(jnp.arange(LANES)[:, None] == (jnp.arange(seq) // blk)[None, :], NEG, 0.0).astype(BF16)
    return pl.pallas_call(
        functools.partial(_moba_prompt_kernel, blk=blk, hd=hd, n_blocks=nb, topk=MOBA_TOPK),
        grid=(batch, npair, nb),
        in_specs=[
            pl.BlockSpec((blk, width), lambda b, hp, c: (b * nb + c, hp)),
            pl.BlockSpec((1, width, blk), lambda b, hp, c: (b, hp, c)),
            pl.BlockSpec((blk, width), lambda b, hp, c: (b * nb + c, hp)),
            pl.BlockSpec((1, width, seq), lambda b, hp, c: (b, hp, 0)),
            pl.BlockSpec((seq, width), lambda b, hp, c: (b, hp)),
            pl.BlockSpec((LANES, seq), lambda b, hp, c: (0, 0)),
        ],
        out_specs=pl.BlockSpec((blk, width), lambda b, hp, c: (b * nb + c, hp)),
        out_shape=jax.ShapeDtypeStruct((batch * seq, n_heads * hd), BF16),
        scratch_shapes=[pltpu.VMEM((width, LANES), F32), pltpu.VMEM((width + LANES, seq), BF16)],
        compiler_params=_cparams("parallel", "parallel", "arbitrary"),
        name="moba_prompt",
    )(mq, kT, mv, kT, mv, eneg)


def _moba_gate_kernel(pt_ref, q_ref, *refs, pages_per_step, pages_per_block, n_heads, hd, ls, n_past, topk):
    del pt_ref
    pages = refs[:pages_per_step]
    idx_ref = refs[pages_per_step]
    km_scr = refs[pages_per_step + 1]
    j = pl.program_id(1)
    nj = pl.num_programs(1)
    d_moba = n_heads * hd
    page_size = pages[0].shape[-1]
    blocks_per_step = pages_per_step // pages_per_block

    @pl.when(j == 0)
    def _():
        km_scr[...] = jnp.zeros_like(km_scr)

    col = lax.broadcasted_iota(jnp.int32, (page_size, LANES), 1)
    acc = km_scr[...]
    for u in range(blocks_per_step):
        ks = pages[u * pages_per_block][0]
        for pg in range(1, pages_per_block):
            ks = ks + pages[u * pages_per_block + pg][0]
        ks = ks.reshape(d_moba, page_size).astype(BF16)
        onecol = (col == j * blocks_per_step + u).astype(BF16)
        acc = acc + jnp.dot(ks, onecol, preferred_element_type=F32)
    km_scr[...] = acc

    @pl.when(j == nj - 1)
    def _():
        kmean = (km_scr[...] * (1.0 / (pages_per_block * page_size))).astype(BF16)
        q = q_ref[0]
        rows = ls * n_heads
        qrep = jnp.concatenate([jnp.broadcast_to(q[i:i + 1, :], (n_heads, d_moba)) for i in range(ls)], axis=0)
        r = lax.broadcasted_iota(jnp.int32, (rows, d_moba), 0)
        cc = lax.broadcasted_iota(jnp.int32, (rows, d_moba), 1)
        qbd = jnp.where((cc // hd) == (r % n_heads), qrep, 0.0).astype(BF16)
        gate = jnp.dot(qbd, kmean, preferred_element_type=F32)
        lane = lax.broadcasted_iota(jnp.int32, (rows, LANES), 1)
        work = jnp.where(lane < n_past, gate, jnp.where(lane == n_past, NEG, -jnp.inf))
        out = jnp.zeros((rows, LANES), jnp.int32)
        for t in range(topk):
            mx = jnp.max(work, axis=1, keepdims=True)
            it = jnp.min(jnp.where(work == mx, lane, LANES), axis=1, keepdims=True)
            out = jnp.where(lane == t, it, out)
            work = jnp.where(lane == it, -jnp.inf, work)
        idx_ref[0] = out


def _moba_gate(page_table, mq_s, cacheT, *, n_heads, hd, ls, pages_per_step):
    bs, n_pages = page_table.shape
    page_size = cacheT.shape[-1]
    ppb = MOBA_BLOCK // page_size
    n_past = n_pages // ppb
    steps = n_pages // pages_per_step
    d_moba = n_heads * hd

    def page_spec(i):
        return pl.BlockSpec((1, n_heads, hd, page_size),
                            lambda b, j, pt: (pt[b * n_pages + j * pages_per_step + i], 0, 0, 0))

    grid_spec = pltpu.PrefetchScalarGridSpec(
        num_scalar_prefetch=1,
        grid=(bs, steps),
        in_specs=[pl.BlockSpec((1, ls, d_moba), lambda b, j, pt: (b, 0, 0))]
        + [page_spec(i) for i in range(pages_per_step)],
        out_specs=pl.BlockSpec((1, ls * n_heads, LANES), lambda b, j, pt: (b, 0, 0)),
        scratch_shapes=[pltpu.VMEM((d_moba, LANES), F32)],
    )
    return pl.pallas_call(
        functools.partial(_moba_gate_kernel, pages_per_step=pages_per_step, pages_per_block=ppb,
                          n_heads=n_heads, hd=hd, ls=ls, n_past=n_past, topk=MOBA_TOPK),
        grid_spec=grid_spec,
        out_shape=jax.ShapeDtypeStruct((bs, ls * n_heads, LANES), jnp.int32),
        compiler_params=_cparams("parallel", "arbitrary"),
        name="moba_sample_gate",
    )(page_table.reshape(-1), mq_s, *([cacheT] * pages_per_step))


def _moba_sample_kernel(pt_ref, idx_ref, qc_ref, kc_ref, vc_ref, ck_ref, cv_ref, o_ref, kbuf, vbuf, sems,
                        *, ls, topk, ppb, n_heads, hd, n_past, n_pages):
    b = pl.program_id(0)
    nb = pl.num_programs(0)
    cur = b % 2
    per_head = ls * topk * ppb
    rows_per_head = per_head // n_heads
    scale = hd ** -0.5

    def slab_at(buf_ref, buf, h, r):
        return buf_ref.at[buf, h * rows_per_head + r // n_heads, r % n_heads]

    def issue(bb, buf):
        def per_head_body(h, c):
            for i in range(ls):
                for j in range(topk):
                    blk = jnp.minimum(idx_ref[((bb * ls + i) * n_heads + h) * topk + j], n_past - 1)
                    for pg in range(ppb):
                        page = pt_ref[bb * n_pages + blk * ppb + pg]
                        r = (i * topk + j) * ppb + pg
                        pltpu.make_async_copy(ck_ref.at[page, h], slab_at(kbuf, buf, h, r), sems.at[buf]).start()
                        pltpu.make_async_copy(cv_ref.at[page, h], slab_at(vbuf, buf, h, r), sems.at[buf]).start()
            return c

        lax.fori_loop(0, n_heads, per_head_body, 0)

    @pl.when(b == 0)
    def _():
        issue(b, cur)

    @pl.when(b + 1 < nb)
    def _():
        issue(b + 1, 1 - cur)

    pltpu.make_async_copy(ck_ref.at[pl.ds(0, per_head)], kbuf.at[cur], sems.at[cur]).wait()
    pltpu.make_async_copy(cv_ref.at[pl.ds(0, per_head)], vbuf.at[cur], sems.at[cur]).wait()

    lane_own = lax.broadcasted_iota(jnp.int32, (1, ls), 1)

    def head_body(h, c):
        qc = qc_ref[0, h]
        knew = kc_ref[0, h]
        vnew = vc_ref[0, h]
        for i in range(ls):
            qcol = qc[:, i:i + 1]
            s_list = []
            for j in range(topk):
                ok = idx_ref[((b * ls + i) * n_heads + h) * topk + j] < n_past
                for pg in range(ppb):
                    kt = slab_at(kbuf, cur, h, (i * topk + j) * ppb + pg)[...]
                    s = jnp.sum(kt * qcol, axis=0, keepdims=True) * scale
                    s_list.append(jnp.where(ok, s, -jnp.inf))
            s_own = jnp.sum(knew * qcol, axis=0, keepdims=True) * scale
            s_own = jnp.where(lane_own <= i, s_own, -jnp.inf)
            mx = jnp.max(s_own, axis=1, keepdims=True)
            for s in s_list:
                mx = jnp.maximum(mx, jnp.max(s, axis=1, keepdims=True))
            p_own = jnp.exp(s_own - mx)
            denom = jnp.sum(p_own, axis=1, keepdims=True)
            o = jnp.sum(vnew * p_own, axis=1, keepdims=True)
            pv = None
            for t, s in enumerate(s_list):
                p = jnp.exp(s - mx)
                denom = denom + jnp.sum(p, axis=1, keepdims=True)
                term = slab_at(vbuf, cur, h, i * topk * ppb + t)[...] * p
                pv = term if pv is None else pv + term
            o = o + jnp.sum(pv, axis=1, keepdims=True)
            o_ref[0, h, :, i:i + 1] = o / denom
        return c

    lax.fori_loop(0, n_heads, head_body, 0)


def _moba_sample(page_table, idx, q_cols, k_cols, v_cols, cacheT_k, cacheT_v, *, n_heads, hd, ls):
    bs, n_pages = page_table.shape
    page_size = cacheT_k.shape[-1]
    ppb = MOBA_BLOCK // page_size
    n_past = n_pages // ppb
    topk = MOBA_TOPK
    per_head = ls * topk * ppb
    assert per_head % n_heads == 0 and cacheT_k.shape[0] >= per_head
    col_spec = pl.BlockSpec((1, n_heads, hd, ls), lambda b, pt, ix: (b, 0, 0, 0))
    any_spec = pl.BlockSpec(memory_space=pl.ANY)
    slab_buf = pltpu.VMEM((2, per_head, n_heads, hd, page_size), F32)
    grid_spec = pltpu.PrefetchScalarGridSpec(
        num_scalar_prefetch=2,
        grid=(bs,),
        in_specs=[col_spec, col_spec, col_spec, any_spec, any_spec],
        out_specs=col_spec,
        scratch_shapes=[slab_buf, slab_buf, pltpu.SemaphoreType.DMA((2,))],
    )
    return pl.pallas_call(
        functools.partial(_moba_sample_kernel, ls=ls, topk=topk, ppb=ppb, n_heads=n_heads, hd=hd, n_past=n_past,
                          n_pages=n_pages),
        grid_spec=grid_spec,
        out_shape=jax.ShapeDtypeStruct((bs, n_heads, hd, ls), F32),
        compiler_params=_cparams("arbitrary"),
        name="moba_sample_attn",
    )(page_table.reshape(-1), idx, q_cols, k_cols, v_cols, cacheT_k, cacheT_v)


def _merge_kernel(*refs, n_in_tiles, **kw):
    i = pl.program_id(0)

    @pl.when(i < n_in_tiles)
    def _():
        _merge_body(*refs, **kw)

    @pl.when(i >= n_in_tiles)
    def _():
        hres_ref, xn_ref, slot_ref, w_ref, cnt_ref = refs[-5:]
        hres_ref[...] = jnp.zeros_like(hres_ref)
        xn_ref[...] = jnp.zeros_like(xn_ref)
        slot_ref[...] = jnp.full(slot_ref.shape, -1, jnp.int32)
        w_ref[...] = jnp.zeros_like(w_ref)
        cnt_ref[...] = jnp.zeros_like(cnt_ref)


def _merge_body(x_ref, ret_ref, rg_ref, mo_ref, rng_ref, wo_ref, n2_ref, rwT_ref, rb_ref, *refs,
                n_heads, dv, n_experts, top_k, n_valid):
    hres_ref, xn_ref, slot_ref, w_ref, cnt_ref = refs[-5:]
    tm = x_ref.shape[0]
    d_ret = n_heads * dv
    ret_o = ret_ref[...]
    parts = []
    for h in range(n_heads):
        r = ret_o[:, h * dv:(h + 1) * dv]
        mu = jnp.mean(r, axis=-1, keepdims=True)
        var = jnp.mean(jnp.square(r - mu), axis=-1, keepdims=True)
        parts.append((r - mu) * lax.rsqrt(var + EPS))
    rg = rg_ref[...]
    ret = jnp.concatenate(parts, axis=1) * rng_ref[...] * (rg * jax.nn.sigmoid(rg))
    mix = (jnp.dot(ret.astype(BF16), wo_ref[0:d_ret, :], preferred_element_type=F32)
           + jnp.dot(mo_ref[...], wo_ref[d_ret:, :], preferred_element_type=F32))
    hres = x_ref[...] + mix
    hres_ref[...] = hres
    xn = _rms(hres, n2_ref[...]).astype(BF16)
    xn_ref[...] = xn
    logits = lax.dot_general(rwT_ref[...], xn, _NT, preferred_element_type=F32) + rb_ref[...]
    row = lax.broadcasted_iota(jnp.int32, logits.shape, 0)
    work = logits
    vals, hots = [], []
    for _ in range(top_k):
        mx = jnp.max(work, axis=0, keepdims=True)
        it = jnp.min(jnp.where(work == mx, row, n_experts), axis=0, keepdims=True)
        hot = row == it
        vals.append(mx)
        hots.append(hot)
        work = jnp.where(hot, -jnp.inf, work)
    exps = [jnp.exp(v - vals[0]) for v in vals]
    denom = exps[0]
    for e in exps[1:]:
        denom = denom + e

    valid = lax.broadcasted_iota(jnp.int32, (1, tm), 1) < n_valid
    mask = hots[0]
    for hot in hots[1:]:
        mask = mask | hot
    mask = mask & valid
    maskf = mask.astype(F32)
    r_i = lax.broadcasted_iota(jnp.int32, (tm, tm), 0)
    c_i = lax.broadcasted_iota(jnp.int32, (tm, tm), 1)
    rank = jnp.dot(maskf.astype(BF16), (r_i < c_i).astype(BF16), preferred_element_type=F32)
    cnt = jnp.sum(maskf, axis=1, keepdims=True).astype(jnp.int32)
    cnt_pad = ((cnt + (SLOT_ALIGN - 1)) // SLOT_ALIGN) * SLOT_ALIGN
    e_r = lax.broadcasted_iota(jnp.int32, (n_experts, n_experts), 0)
    e_c = lax.broadcasted_iota(jnp.int32, (n_experts, n_experts), 1)
    cpb = jnp.broadcast_to(cnt_pad.astype(F32), (n_experts, LANES)).astype(BF16)
    tile_off = jnp.dot((e_c < e_r).astype(BF16), cpb, preferred_element_type=F32)[:, 0:1]
    slot = tile_off + rank

    row8 = lax.broadcasted_iota(jnp.int32, (8, tm), 0)
    slot_out = jnp.full((8, tm), -1, jnp.int32)
    w_out = jnp.zeros((8, tm), F32)
    for k in range(top_k):
        sk = jnp.sum(jnp.where(hots[k], slot, 0.0), axis=0, keepdims=True).astype(jnp.int32)
        sk = jnp.where(valid, sk, -1)
        slot_out = jnp.where(row8 == k, sk, slot_out)
        w_out = jnp.where(row8 == k, exps[k] / denom, w_out)
    slot_ref[...] = slot_out
    w_ref[...] = w_out
    cnt_ref[0] = jnp.broadcast_to(cnt, (n_experts, LANES))


def _merge(x, ret_o, rg, moba_o, ret_norm_g, w_out, norm2_g, rwT, router_b, prev, *,
           n_heads, dv, tm, n_tiles_total, tile0, n_valid, n_clear_tiles=0):
    t, d_model = x.shape
    n_experts = rwT.shape[0]
    t_pad = n_tiles_total * tm
    n_in = t // tm
    tok_in = lambda n: pl.BlockSpec((tm, n), lambda i: (jnp.minimum(i, n_in - 1), 0))
    tok = lambda n: pl.BlockSpec((tm, n), lambda i: (tile0 + i, 0))
    full = lambda a: pl.BlockSpec(a.shape, lambda i: (0,) * a.ndim)
    out_shape = [jax.ShapeDtypeStruct((t_pad, d_model), F32), jax.ShapeDtypeStruct((t_pad, d_model), BF16),
                 jax.ShapeDtypeStruct((8, t_pad), jnp.int32), jax.ShapeDtypeStruct((8, t_pad), F32),
                 jax.ShapeDtypeStruct((n_tiles_total, n_experts, LANES), jnp.int32)]
    out_specs = [tok(d_model), tok(d_model),
                 pl.BlockSpec((8, tm), lambda i: (0, tile0 + i)), pl.BlockSpec((8, tm), lambda i: (0, tile0 + i)),
                 pl.BlockSpec((1, n_experts, LANES), lambda i: (tile0 + i, 0, 0))]
    args = [x, ret_o, rg, moba_o, ret_norm_g, w_out, norm2_g, rwT, router_b]
    in_specs = [tok_in(d_model), tok_in(n_heads * dv), tok_in(n_heads * dv), tok_in(moba_o.shape[1]),
                full(ret_norm_g), full(w_out), full(norm2_g), full(rwT), full(router_b)]
    aliases = {}
    if prev is not None:
        aliases = {len(args) + k: k for k in range(len(prev))}
        args += list(prev)
        in_specs += [pl.BlockSpec(memory_space=pl.ANY)] * len(prev)
    return pl.pallas_call(
        functools.partial(_merge_kernel, n_heads=n_heads, dv=dv, n_experts=n_experts, top_k=TOP_K,
                          n_valid=n_valid, n_in_tiles=n_in),
        grid=(n_in + n_clear_tiles,),
        in_specs=in_specs,
        out_specs=out_specs,
        out_shape=out_shape,
        input_output_aliases=aliases,
        compiler_params=_cparams("parallel"),
        name="merge_router",
    )(*args)


def _run_copies(cp_ref, to_ref, dr_ref, t, n_experts, make_copy):
    def per_expert(e, total):
        n = cp_ref[t * n_experts + e] // SLOT_ALIGN
        src0 = to_ref[t * n_experts + e]
        dst0 = dr_ref[t * n_experts + e]

        def one(j, c):
            make_copy(pl.multiple_of(src0 + j * SLOT_ALIGN, SLOT_ALIGN),
                      pl.multiple_of(dst0 + j * SLOT_ALIGN, SLOT_ALIGN)).start()
            return c

        lax.fori_loop(0, n, one, 0)
        return total + n

    return lax.fori_loop(0, n_experts, per_expert, 0)


def _run_count(cp_ref, t, n_experts):
    return lax.fori_loop(0, n_experts, lambda e, c: c + cp_ref[t * n_experts + e] // SLOT_ALIGN, 0)


def _dispatch_kernel(cp_ref, to_ref, dr_ref, ts_ref, tn_ref, xn_ref, slot_ref, xs_ref, slots_scr, zero_scr, sems,
                     *, n_experts, n_slots, chunk, top_k):
    t = pl.program_id(0)
    nt = pl.num_programs(0)
    cur = t % 2
    tm = xn_ref.shape[0]
    xn = xn_ref[...]
    sl = slot_ref[...]
    for c in range(n_slots // chunk):
        s_iota = c * chunk + lax.broadcasted_iota(jnp.int32, (chunk, tm), 0)
        pm = sl[0:1, :] == s_iota
        for k in range(1, top_k):
            pm = pm | (sl[k:k + 1, :] == s_iota)
        slots_scr[cur, c * chunk:(c + 1) * chunk, :] = jnp.dot(
            pm.astype(BF16), xn, preferred_element_type=F32).astype(BF16)

    def make_copy(src, dst, buf=cur):
        return pltpu.make_async_copy(slots_scr.at[buf, pl.ds(src, SLOT_ALIGN)],
                                     xs_ref.at[pl.ds(dst, SLOT_ALIGN)], sems.at[buf])

    def wait_n(n, buf):
        lax.fori_loop(0, n, lambda j, c: (make_copy(0, 0, buf).wait(), c)[1], 0)

    total = _run_copies(cp_ref, to_ref, dr_ref, t, n_experts, make_copy)

    @pl.when(t > 0)
    def _():
        wait_n(_run_count(cp_ref, t - 1, n_experts), 1 - cur)

    @pl.when(t == nt - 1)
    def _():
        zero_scr[...] = jnp.zeros_like(zero_scr)

        def zcopy(dst):
            return pltpu.make_async_copy(zero_scr.at[pl.ds(0, SLOT_ALIGN)], xs_ref.at[pl.ds(dst, SLOT_ALIGN)],
                                         sems.at[cur])

        def per_expert(e, tot):
            def one(j, c):
                zcopy(pl.multiple_of(ts_ref[e] + j * SLOT_ALIGN, SLOT_ALIGN)).start()
                return c
            lax.fori_loop(0, tn_ref[e], one, 0)
            return tot + tn_ref[e]

        n_tail = lax.fori_loop(0, n_experts, per_expert, 0)
        wait_n(total + n_tail, cur)

        tme = zero_scr.shape[0]

        def ztile(j):
            return pltpu.make_async_copy(zero_scr, xs_ref.at[pl.ds(pl.multiple_of(j * tme, tme), tme)], sems.at[cur])

        n_used = tn_ref[n_experts]
        n_all = xs_ref.shape[0] // tme
        lax.fori_loop(n_used, n_all, lambda j, c: (ztile(j).start(), c)[1], 0)
        lax.fori_loop(n_used, n_all, lambda j, c: (ztile(j).wait(), c)[1], 0)


def _dispatch(cp, to, dr, ts, tn, xn_all, slot, *, tm, tme, n_experts, n_slots, p_rows):
    t_pad, d_model = xn_all.shape
    grid_spec = pltpu.PrefetchScalarGridSpec(
        num_scalar_prefetch=5,
        grid=(t_pad // tm,),
        in_specs=[pl.BlockSpec((tm, d_model), lambda i, *_: (i, 0)),
                  pl.BlockSpec((8, tm), lambda i, *_: (0, i))],
        out_specs=pl.BlockSpec(memory_space=pl.ANY),
        scratch_shapes=[pltpu.VMEM((2, n_slots, d_model), BF16), pltpu.VMEM((tme, d_model), BF16),
                        pltpu.SemaphoreType.DMA((2,))],
    )
    return pl.pallas_call(
        functools.partial(_dispatch_kernel, n_experts=n_experts, n_slots=n_slots, chunk=256, top_k=TOP_K),
        grid_spec=grid_spec,
        out_shape=jax.ShapeDtypeStruct((p_rows, d_model), BF16),
        compiler_params=_cparams("arbitrary"),
        name="moe_dispatch",
    )(cp, to, dr, ts, tn, xn_all, slot)


def _expert_kernel(te_ref, tv_ref, xi_ref, x_ref, wgu_ref, bgu_ref, wd_ref, bd_ref, y_ref, wp_scr, wdb_scr, *, d_ff):
    j = pl.program_id(0)
    grp = 2 * LANES
    n_grp = 2 * d_ff // grp

    @pl.when(tv_ref[j] > 0)
    def _():
        first = (j == 0) | (te_ref[j] != te_ref[jnp.maximum(j - 1, 0)])

        @pl.when(first)
        def _():
            r = lax.broadcasted_iota(jnp.int32, (grp, grp), 0)
            c = lax.broadcasted_iota(jnp.int32, (grp, grp), 1)
            perm = (((c < LANES) & (r == 2 * c)) | ((c >= LANES) & (r == 2 * (c - LANES) + 1))).astype(BF16)
            for g in range(n_grp):
                wp_scr[:, g * grp:(g + 1) * grp] = jnp.dot(
                    wgu_ref[0, :, g * grp:(g + 1) * grp].astype(BF16), perm, preferred_element_type=F32).astype(BF16)
            wdb_scr[...] = wd_ref[0].astype(BF16)

        x = x_ref[...]
        parts = []
        for g in range(n_grp):
            u = jnp.dot(x, wp_scr[:, g * grp:(g + 1) * grp], preferred_element_type=F32) + bgu_ref[0, :, g * grp:(g + 1) * grp]
            glu = jnp.minimum(u[:, :LANES], SWIGLU_LIMIT)
            lin = jnp.clip(u[:, LANES:], -SWIGLU_LIMIT, SWIGLU_LIMIT)
            parts.append((glu * jax.nn.sigmoid(SWIGLU_ALPHA * glu) * (lin + 1.0)).astype(BF16))
        a = jnp.concatenate(parts, axis=1)
        y_ref[...] = (jnp.dot(a, wdb_scr[...], preferred_element_type=F32) + bd_ref[0]).astype(BF16)

    @pl.when(tv_ref[j] == 0)
    def _():
        y_ref[...] = jnp.zeros_like(y_ref)


def _experts(te, tv, xi, x_sorted, wgu, bgu_perm, wd, bd, *, tme):
    p_rows, d_model = x_sorted.shape
    n_experts, _, d_ff2 = wgu.shape
    d_ff = d_ff2 // 2
    ex = lambda a: pl.BlockSpec((1,) + a.shape[1:], lambda j, te_, tv_, xi_: (te_[j],) + (0,) * (a.ndim - 1))
    grid_spec = pltpu.PrefetchScalarGridSpec(
        num_scalar_prefetch=3,
        grid=(p_rows // tme,),
        in_specs=[pl.BlockSpec((tme, d_model), lambda j, te_, tv_, xi_: (xi_[j], 0)),
                  ex(wgu), ex(bgu_perm), ex(wd), ex(bd)],
        out_specs=pl.BlockSpec((tme, d_model), lambda j, *_: (j, 0)),
        scratch_shapes=[pltpu.VMEM((d_model, d_ff2), BF16), pltpu.VMEM((d_ff, d_model), BF16)],
    )
    return pl.pallas_call(
        functools.partial(_expert_kernel, d_ff=d_ff),
        grid_spec=grid_spec,
        out_shape=jax.ShapeDtypeStruct((p_rows, d_model), BF16),
        compiler_params=_cparams("arbitrary"),
        name="moe_experts",
    )(te, tv, xi, x_sorted, wgu, bgu_perm, wd, bd)


def _combine_kernel(cp_ref, to_ref, dr_ref, hres_ref, slotT_ref, wT_ref, fg_ref, ys_ref, yp_ref, ysm_ref,
                    slots_scr, sems, *, n_experts, n_slots, chunk, top_k, n_prompt_tiles):
    t = pl.program_id(0)
    nt = pl.num_programs(0)
    cur = t % 2
    tm, d_model = hres_ref.shape

    def fetch(tile, buf):
        def make_copy(src, dst):
            return pltpu.make_async_copy(ys_ref.at[pl.ds(dst, SLOT_ALIGN)],
                                         slots_scr.at[buf, pl.ds(src, SLOT_ALIGN)], sems.at[buf])

        _run_copies(cp_ref, to_ref, dr_ref, tile, n_experts, make_copy)
        used = to_ref[tile * n_experts + n_experts - 1] + cp_ref[tile * n_experts + n_experts - 1]

        def zero_one(j, c):
            slots_scr[buf, pl.ds(pl.multiple_of(used + j * SLOT_ALIGN, SLOT_ALIGN), SLOT_ALIGN), :] = jnp.zeros(
                (SLOT_ALIGN, d_model), BF16)
            return c

        lax.fori_loop(0, (n_slots - used) // SLOT_ALIGN, zero_one, 0)

    @pl.when(t == 0)
    def _():
        fetch(t, cur)

    @pl.when(t + 1 < nt)
    def _():
        fetch(t + 1, 1 - cur)

    def wait_one(j, c):
        pltpu.make_async_copy(ys_ref.at[pl.ds(0, SLOT_ALIGN)], slots_scr.at[cur, pl.ds(0, SLOT_ALIGN)],
                              sems.at[cur]).wait()
        return c

    lax.fori_loop(0, _run_count(cp_ref, t, n_experts), wait_one, 0)

    sl = slotT_ref[...]
    wt = wT_ref[...]
    acc = hres_ref[...]
    for c in range(n_slots // chunk):
        s_iota = c * chunk + lax.broadcasted_iota(jnp.int32, (tm, chunk), 1)
        pw = jnp.where(sl[:, 0:1] == s_iota, wt[:, 0:1], 0.0)
        for k in range(1, top_k):
            pw = pw + jnp.where(sl[:, k:k + 1] == s_iota, wt[:, k:k + 1], 0.0)
        acc = acc + jnp.dot(pw.astype(BF16), slots_scr[cur, c * chunk:(c + 1) * chunk, :],
                            preferred_element_type=F32)
    y = _rms(acc, fg_ref[...])

    @pl.when(t < n_prompt_tiles)
    def _():
        yp_ref[...] = y

    @pl.when(t >= n_prompt_tiles)
    def _():
        ysm_ref[...] = y[:ysm_ref.shape[0], :]


def _combine(cp, to, dr, hres_all, slotT, wT, fg, y_sorted, *, tm, n_experts, n_slots, n_prompt_tiles, n_sample):
    t_pad, d_model = hres_all.shape
    npt = n_prompt_tiles
    grid_spec = pltpu.PrefetchScalarGridSpec(
        num_scalar_prefetch=3,
        grid=(t_pad // tm,),
        in_specs=[pl.BlockSpec((tm, d_model), lambda i, *_: (i, 0)),
                  pl.BlockSpec((tm, 8), lambda i, *_: (i, 0)),
                  pl.BlockSpec((tm, 8), lambda i, *_: (i, 0)),
                  pl.BlockSpec(fg.shape, lambda i, *_: (0, 0)),
                  pl.BlockSpec(memory_space=pl.ANY)],
        out_specs=[pl.BlockSpec((tm, d_model), lambda i, *_: (jnp.minimum(i, npt - 1), 0)),
                   pl.BlockSpec((n_sample, d_model), lambda i, *_: (0, 0))],
        scratch_shapes=[pltpu.VMEM((2, n_slots, d_model), BF16), pltpu.SemaphoreType.DMA((2,))],
    )
    return pl.pallas_call(
        functools.partial(_combine_kernel, n_experts=n_experts, n_slots=n_slots, chunk=256, top_k=TOP_K,
                          n_prompt_tiles=npt),
        grid_spec=grid_spec,
        out_shape=[jax.ShapeDtypeStruct((npt * tm, d_model), F32), jax.ShapeDtypeStruct((n_sample, d_model), F32)],
        compiler_params=_cparams("arbitrary"),
        name="moe_combine",
    )(cp, to, dr, hres_all, slotT, wT, fg, y_sorted)


def _route_tables(cnt, *, tme, n_row_tiles):
    n_tiles, n_experts = cnt.shape
    cnt_pad = (cnt + (SLOT_ALIGN - 1)) // SLOT_ALIGN * SLOT_ALIGN
    tile_off = jnp.cumsum(cnt_pad, axis=1) - cnt_pad
    tot = jnp.sum(cnt_pad, axis=0)
    tot_t = (tot + (tme - 1)) // tme * tme
    ends = jnp.cumsum(tot_t)
    base = ends - tot_t
    dst_row = base[None, :] + jnp.cumsum(cnt_pad, axis=0) - cnt_pad
    j = jnp.arange(n_row_tiles, dtype=jnp.int32)
    te = jnp.minimum(jnp.sum(((ends // tme)[None, :] <= j[:, None]).astype(jnp.int32), axis=1), n_experts - 1)
    n_used = ends[-1] // tme
    tv = (j < n_used).astype(jnp.int32)
    xi = jnp.minimum(j, n_used - 1)
    tail_start = base + tot
    tail_n = jnp.concatenate([(tot_t - tot) // SLOT_ALIGN, n_used[None]])
    flat = lambda a: a.reshape(-1).astype(jnp.int32)
    return (flat(cnt_pad), flat(tile_off), flat(dst_row), flat(tail_start), flat(tail_n),
            flat(te), flat(tv), flat(xi))


def kernel(x_prompt, x_sample, cache_k, cache_v, state_ret, page_table, norm1_g, w_in, ret_norm_g, w_out,
           norm2_g, router_w, router_b, w_gate_up, b_gate_up, w_down, b_down, final_norm_g):
    bp, lp, d_model = x_prompt.shape
    bs, ls, _ = x_sample.shape
    depth = w_in.shape[0]
    assert depth == 1, "single-layer step"
    n_pages = page_table.shape[1]
    page_size, mh, hd = cache_k.shape[2], cache_k.shape[3], cache_k.shape[4]
    rh, dk, dv = state_ret.shape[2], state_ret.shape[3], state_ret.shape[4]
    d_ret = rh * dk
    d_moba = mh * hd
    n_experts = router_w.shape[2]
    d_ff = w_down.shape[2]
    past_len = n_pages * page_size
    layer = 0

    w = w_in[layer]
    w_all = w.astype(BF16)
    w_main = jnp.concatenate([w[:, :4 * d_ret + d_moba], w[:, 4 * d_ret + 2 * d_moba:]], axis=1).astype(BF16)
    wkvT = w[:, 4 * d_ret + d_moba:].T.astype(BF16)
    g1 = norm1_g[layer][None, :]
    wo = w_out[layer].astype(BF16)
    rng = ret_norm_g[layer][None, :]
    n2 = norm2_g[layer][None, :]
    rwT = router_w[layer].T.astype(BF16)
    rb = router_b[layer][:, None]
    wgu = w_gate_up[layer]
    bgu_perm = b_gate_up[layer].reshape(n_experts, -1, LANES, 2).transpose(0, 1, 3, 2).reshape(n_experts, 1, 2 * d_ff)
    wd = w_down[layer]
    bd = b_down[layer][:, None, :]
    fg = final_norm_g[None, :]

    tm = _token_tile(lp)
    pos_p = jnp.arange(lp, dtype=jnp.int32)
    cos_p, sin_p = _rope_tables(pos_p, dk)
    xp = x_prompt.reshape(bp * lp, d_model)
    rq, rk, rv, rg, mq, mv, kT, vT = _inproj_prompt(
        xp, g1, w_main, wkvT, cos_p, sin_p, batch=bp, seq=lp, d_ret=d_ret, d_moba=d_moba, n_ret_heads=rh, tm=tm)
    ret_o, s_p = _ret_prompt(rq, rk, rv, batch=bp, seq=lp, n_heads=rh, dk=dk, dv=dv)
    moba_o = _moba_prompt(mq, mv, kT, batch=bp, seq=lp, n_heads=mh, hd=hd)
    n_sample = bs * ls
    assert n_sample <= tm, "sample group must fit one token tile"
    npt = bp * lp // tm
    n_tiles = npt + 1
    bufs = _merge(xp, ret_o, rg, moba_o, rng, wo, n2, rwT, rb, None,
                  n_heads=rh, dv=dv, tm=tm, n_tiles_total=n_tiles, tile0=0, n_valid=tm, n_clear_tiles=1)

    pos_s = past_len + jnp.arange(ls, dtype=jnp.int32)
    cos_s, sin_s = _rope_tables(jnp.tile(pos_s, bs), dk)
    xs = x_sample.reshape(bs * ls, d_model)
    rq_s, rk_s, rv_s, rg_s, mq_s, mk_s, mv_s = _inproj_sample(
        xs, g1, w_all, cos_s, sin_s, d_ret=d_ret, d_moba=d_moba, n_ret_heads=rh)
    r3 = lambda a: a.reshape(bs, ls, a.shape[1])
    ret_o_s, s_s = _ret_sample(r3(rq_s), r3(rk_s), r3(rv_s), state_ret[layer], n_heads=rh, dk=dk, dv=dv)
    ckT = cache_k[layer].transpose(0, 2, 3, 1)
    cvT = cache_v[layer].transpose(0, 2, 3, 1)
    idx_pad = _moba_gate(page_table, r3(mq_s), ckT, n_heads=mh, hd=hd, ls=ls, pages_per_step=16)
    idx = idx_pad[:, :, :MOBA_TOPK].reshape(-1)
    cols = lambda a: a.reshape(bs, ls, mh, hd).transpose(0, 2, 3, 1)
    o_cols = _moba_sample(page_table, idx, cols(mq_s), cols(mk_s), cols(mv_s), ckT, cvT,
                          n_heads=mh, hd=hd, ls=ls)
    moba_o_s = o_cols.transpose(0, 3, 1, 2).reshape(bs * ls, d_moba).astype(BF16)
    padt = lambda a: jnp.pad(a, ((0, tm - n_sample), (0, 0)))
    hres_all, xn_all, slot, wts, cnt = _merge(
        padt(xs), padt(ret_o_s.reshape(n_sample, d_ret)), padt(rg_s), padt(moba_o_s), rng, wo, n2, rwT, rb, bufs,
        n_heads=rh, dv=dv, tm=tm, n_tiles_total=n_tiles, tile0=npt, n_valid=n_sample)

    tme = 512
    n_slots = -(-(tm * TOP_K + n_experts * (SLOT_ALIGN - 1)) // 256) * 256
    n_assign = TOP_K * (bp * lp + n_sample)
    n_row_tiles = -(-(n_assign + (SLOT_ALIGN - 1) * n_experts * n_tiles + n_experts * (tme - 1)) // tme)
    cp, to, dr, ts, tn, te, tv, xi = _route_tables(cnt[:, :, 0], tme=tme, n_row_tiles=n_row_tiles)
    x_sorted = _dispatch(cp, to, dr, ts, tn, xn_all, slot, tm=tm, tme=tme, n_experts=n_experts, n_slots=n_slots,
                         p_rows=n_row_tiles * tme)
    y_sorted = _experts(te, tv, xi, x_sorted, wgu, bgu_perm, wd, bd, tme=tme)
    y_p, y_s = _combine(cp, to, dr, hres_all, slot.T, wts.T, fg, y_sorted, tm=tm, n_experts=n_experts,
                        n_slots=n_slots, n_prompt_tiles=npt, n_sample=n_sample)

    y_prompt = y_p.reshape(bp, lp, d_model)
    y_sample = y_s.reshape(bs, ls, d_model)
    k_prompt = kT.reshape(bp, mh, hd, lp).transpose(0, 3, 1, 2)[None]
    v_prompt = vT.reshape(bp, mh, hd, lp).transpose(0, 3, 1, 2)[None]
    k_sample = mk_s.reshape(1, bs, ls, mh, hd)
    v_sample = mv_s.reshape(1, bs, ls, mh, hd)
    return (y_prompt, y_sample, k_prompt, v_prompt, s_p[None], k_sample, v_sample, s_s[None])
```

```python
import functools

import jax
import jax.numpy as jnp
from jax import lax
from jax.experimental import pallas as pl
from jax.experimental.pallas import tpu as pltpu

F32 = jnp.float32
BF16 = jnp.bfloat16

EPS = 1e-5
ROPE_BASE = 10000.0
RET_CHUNK = 128
MOBA_BLOCK = 256
MOBA_TOPK = 3
TOP_K = 4
SWIGLU_LIMIT = 7.0
SWIGLU_ALPHA = 1.702
NEG = -1e30
LANES = 128
SLOT_ALIGN = 16
RUN_PIECE = 64
WAIT_PIECE = 256
VMEM_LIMIT = 56 * 1024 * 1024

_NT = (((1,), (1,)), ((), ()))


def _cparams(*sem):
    return pltpu.CompilerParams(dimension_semantics=sem, vmem_limit_bytes=VMEM_LIMIT)


def _token_tile(seq):
    for tm in (512, 256, 128):
        if seq % tm == 0:
            return tm
    raise ValueError(f"sequence length {seq} must be a multiple of 128")


def _rms(x, g):
    return x * lax.rsqrt(jnp.mean(x * x, axis=-1, keepdims=True) + EPS) * g


def _rope_tables(pos, dk):
    half = dk // 2
    inv = ROPE_BASE ** (-jnp.arange(half, dtype=F32) / half)
    ang = pos.astype(F32)[:, None] * inv[None, :]
    c = jnp.cos(ang)
    s = jnp.sin(ang)
    return jnp.concatenate([c, c], axis=1), jnp.concatenate([-s, s], axis=1)


def _rotary(z, cos2, sin2, n_heads, dk):
    outs = []
    for h in range(n_heads):
        s = z[:, h * dk:(h + 1) * dk]
        outs.append(s * cos2 + pltpu.roll(s, dk // 2, 1) * sin2)
    return jnp.concatenate(outs, axis=1)


def _head_norm_gate(o, g, rg):
    mu = jnp.mean(o, axis=-1, keepdims=True)
    var = jnp.mean(jnp.square(o - mu), axis=-1, keepdims=True)
    return (o - mu) * lax.rsqrt(var + EPS) * g * (rg * jax.nn.sigmoid(rg))


def _inproj_prompt_kernel(x_ref, g_ref, w_ref, wkvT_ref, cos_ref, sin_ref, rng_ref,
                          din_ref, qdec_ref, kdec_ref, cdec_ref,
                          ret_ref, mq_ref, mv_ref, kT_ref, vT_ref, s_out_ref, s_scr,
                          *, d_ret, d_moba, n_ret_heads, chunk):
    i = pl.program_id(1)
    dk = d_ret // n_ret_heads
    tm = x_ref.shape[0]
    xn = _rms(x_ref[...], g_ref[...]).astype(BF16)
    cos2 = cos_ref[...]
    sin2 = sin_ref[...]

    def proj(c0, n):
        return jnp.dot(xn, w_ref[:, c0:c0 + n], preferred_element_type=F32)

    mq_ref[...] = proj(4 * d_ret, d_moba).astype(BF16)
    mv_ref[...] = proj(4 * d_ret + d_moba, d_moba).astype(BF16)
    kT_ref[0] = lax.dot_general(wkvT_ref[0:d_moba, :], xn, _NT, preferred_element_type=F32)
    vT_ref[0] = lax.dot_general(wkvT_ref[d_moba:2 * d_moba, :], xn, _NT, preferred_element_type=F32)

    rq = _rotary(proj(0, d_ret), cos2, sin2, n_ret_heads, dk)
    rk = _rotary(proj(d_ret, d_ret), cos2, sin2, n_ret_heads, dk) * (dk ** -0.5)
    rv = proj(2 * d_ret, d_ret)
    rg = proj(3 * d_ret, d_ret)
    rng = rng_ref[...]

    @pl.when(i == 0)
    def _():
        s_scr[...] = jnp.zeros_like(s_scr)

    for h in range(n_ret_heads):
        hs = slice(h * dk, (h + 1) * dk)
        s = s_scr[h]
        for c in range(tm // chunk):
            rows = slice(c * chunk, (c + 1) * chunk)
            q = rq[rows, hs]
            k = rk[rows, hs]
            v = rv[rows, hs].astype(BF16)
            scores = lax.dot_general(q.astype(BF16), k.astype(BF16), _NT, preferred_element_type=F32) * din_ref[h]
            inner = jnp.dot(scores.astype(BF16), v, preferred_element_type=F32)
            cross = jnp.dot((q * qdec_ref[h]).astype(BF16), s.astype(BF16), preferred_element_type=F32)
            ret_ref[rows, hs] = _head_norm_gate(inner + cross, rng[:, hs], rg[rows, hs]).astype(BF16)
            kdT = (k * kdec_ref[h]).T.astype(BF16)
            s = cdec_ref[h] * s + jnp.dot(kdT, v, preferred_element_type=F32)
        s_scr[h] = s

    @pl.when(i == pl.num_programs(1) - 1)
    def _():
        s_out_ref[0] = s_scr[...]


def _inproj_prompt(x, g, w_main, wkvT, cos2, sin2, rng, *, batch, seq, d_ret, d_moba, n_ret_heads, tm):
    d_model = x.shape[1]
    nt = seq // tm
    dk = d_ret // n_ret_heads
    chunk = RET_CHUNK
    decay_in, q_dec, k_dec, chunk_dec = _ret_tables(n_ret_heads, chunk)
    qdec_b = jnp.broadcast_to(q_dec.T[:, :, None], (n_ret_heads, chunk, dk))
    kdec_b = jnp.broadcast_to(k_dec.T[:, :, None], (n_ret_heads, chunk, dk))
    cdec_b = jnp.broadcast_to(chunk_dec[:, None, None], (n_ret_heads, dk, dk))
    tok_spec = lambda n: pl.BlockSpec((tm, n), lambda b, i: (b * nt + i, 0))
    full = lambda a: pl.BlockSpec(a.shape, lambda b, i: (0,) * a.ndim)
    t = batch * seq
    out_shape = (
        [jax.ShapeDtypeStruct((t, d_ret), BF16)]
        + [jax.ShapeDtypeStruct((t, d_moba), BF16)] * 2
        + [jax.ShapeDtypeStruct((batch, d_moba, seq), F32)] * 2
        + [jax.ShapeDtypeStruct((batch, n_ret_heads, dk, dk), F32)]
    )
    out_specs = (
        [tok_spec(d_ret)] + [tok_spec(d_moba)] * 2
        + [pl.BlockSpec((1, d_moba, tm), lambda b, i: (b, 0, i))] * 2
        + [pl.BlockSpec((1, n_ret_heads, dk, dk), lambda b, i: (b, 0, 0, 0))]
    )
    return pl.pallas_call(
        functools.partial(_inproj_prompt_kernel, d_ret=d_ret, d_moba=d_moba, n_ret_heads=n_ret_heads, chunk=chunk),
        grid=(batch, nt),
        in_specs=[tok_spec(d_model), full(g), full(w_main), full(wkvT),
                  pl.BlockSpec((tm, dk), lambda b, i: (i, 0)), pl.BlockSpec((tm, dk), lambda b, i: (i, 0)),
                  full(rng), full(decay_in), full(qdec_b), full(kdec_b), full(cdec_b)],
        out_specs=out_specs,
        out_shape=out_shape,
        scratch_shapes=[pltpu.VMEM((n_ret_heads, dk, dk), F32)],
        compiler_params=_cparams("parallel", "arbitrary"),
        name="inproj_retention_prompt",
    )(x, g, w_main, wkvT, cos2, sin2, rng, decay_in, qdec_b, kdec_b, cdec_b)


def _inproj_sample_kernel(x_ref, g_ref, w_ref, cos_ref, sin_ref,
                          rq_ref, rk_ref, rv_ref, rg_ref, mq_ref, mk_ref, mv_ref,
                          *, d_ret, d_moba, n_ret_heads):
    dk = d_ret // n_ret_heads
    xn = _rms(x_ref[...], g_ref[...]).astype(BF16)
    cos2 = cos_ref[...]
    sin2 = sin_ref[...]

    def proj(c0, n):
        return jnp.dot(xn, w_ref[:, c0:c0 + n], preferred_element_type=F32)

    rq_ref[...] = _rotary(proj(0, d_ret), cos2, sin2, n_ret_heads, dk)
    rk_ref[...] = _rotary(proj(d_ret, d_ret), cos2, sin2, n_ret_heads, dk) * (dk ** -0.5)
    rv_ref[...] = proj(2 * d_ret, d_ret)
    rg_ref[...] = proj(3 * d_ret, d_ret)
    mq_ref[...] = proj(4 * d_ret, d_moba)
    mk_ref[...] = proj(4 * d_ret + d_moba, d_moba)
    mv_ref[...] = proj(4 * d_ret + 2 * d_moba, d_moba)


def _inproj_sample(x, g, w_all, cos2, sin2, *, d_ret, d_moba, n_ret_heads):
    t = x.shape[0]
    args = (x, g, w_all, cos2, sin2)
    full = lambda a: pl.BlockSpec(a.shape, lambda i: (0,) * a.ndim)
    out_shape = [jax.ShapeDtypeStruct((t, d_ret), F32)] * 4 + [jax.ShapeDtypeStruct((t, d_moba), F32)] * 3
    return pl.pallas_call(
        functools.partial(_inproj_sample_kernel, d_ret=d_ret, d_moba=d_moba, n_ret_heads=n_ret_heads),
        grid=(1,),
        in_specs=[full(a) for a in args],
        out_specs=[pl.BlockSpec(s.shape, lambda i: (0, 0)) for s in out_shape],
        out_shape=out_shape,
        compiler_params=_cparams("arbitrary"),
        name="inproj_sample",
    )(*args)


def _ret_tables(n_heads, chunk):
    log_g = jnp.log1p(-jnp.exp2(-5.0 - jnp.arange(n_heads, dtype=F32)))
    n = jnp.arange(chunk, dtype=F32)
    rel = n[:, None] - n[None, :]
    causal = rel >= 0
    decay_in = jnp.where(causal[None], jnp.exp(jnp.where(causal, rel, 0.0)[None] * log_g[:, None, None]), 0.0)
    q_dec = jnp.exp((n[:, None] + 1.0) * log_g[None, :])
    k_dec = jnp.exp((chunk - 1.0 - n[:, None]) * log_g[None, :])
    chunk_dec = jnp.exp(chunk * log_g)
    return decay_in, q_dec, k_dec, chunk_dec


def _ret_sample_kernel(rq_ref, rk_ref, rv_ref, rg_ref, rng_ref, s_ref, din_ref, qdec_ref, kdec_ref, cdec_ref,
                       o_ref, s_out_ref, *, n_heads, dk, dv):
    row = lax.broadcasted_iota(jnp.int32, (dk, dk), 0)
    col = lax.broadcasted_iota(jnp.int32, (dk, dk), 1)
    eye = (row == col).astype(BF16)
    rng = rng_ref[...]
    for h in range(n_heads):
        q = rq_ref[0, :, h * dk:(h + 1) * dk]
        k = rk_ref[0, :, h * dk:(h + 1) * dk]
        v = rv_ref[0, :, h * dv:(h + 1) * dv].astype(BF16)
        s = s_ref[0, h]
        scores = lax.dot_general(q.astype(BF16), k.astype(BF16), _NT, preferred_element_type=F32) * din_ref[h]
        inner = jnp.dot(scores.astype(BF16), v, preferred_element_type=F32)
        cross = jnp.dot((q * qdec_ref[h]).astype(BF16), s.astype(BF16), preferred_element_type=F32)
        o_ref[0, :, h * dv:(h + 1) * dv] = _head_norm_gate(
            inner + cross, rng[:, h * dv:(h + 1) * dv], rg_ref[0, :, h * dv:(h + 1) * dv])
        kd = (k * kdec_ref[h]).astype(BF16)
        kdT = lax.dot_general(eye, kd, _NT, preferred_element_type=F32).astype(BF16)
        s_out_ref[0, h] = cdec_ref[h] * s + jnp.dot(kdT, v, preferred_element_type=F32)


def _ret_sample(rq, rk, rv, rg, rng, state, *, n_heads, dk, dv):
    bs, ls, _ = rq.shape
    decay_in, q_dec, k_dec, chunk_dec = _ret_tables(n_heads, ls)
    qdec_b = jnp.broadcast_to(q_dec.T[:, :, None], (n_heads, ls, dk))
    kdec_b = jnp.broadcast_to(k_dec.T[:, :, None], (n_heads, ls, dk))
    cdec_b = jnp.broadcast_to(chunk_dec[:, None, None], (n_heads, dk, dv))
    tok = lambda n: pl.BlockSpec((1, ls, n), lambda b: (b, 0, 0))
    full = lambda a: pl.BlockSpec(a.shape, lambda b: (0,) * a.ndim)
    st = pl.BlockSpec((1, n_heads, dk, dv), lambda b: (b, 0, 0, 0))
    return pl.pallas_call(
        functools.partial(_ret_sample_kernel, n_heads=n_heads, dk=dk, dv=dv),
        grid=(bs,),
        in_specs=[tok(n_heads * dk), tok(n_heads * dk), tok(n_heads * dv), tok(n_heads * dv), full(rng), st,
                  full(decay_in), full(qdec_b), full(kdec_b), full(cdec_b)],
        out_specs=[tok(n_heads * dv), st],
        out_shape=[jax.ShapeDtypeStruct((bs, ls, n_heads * dv), F32),
                   jax.ShapeDtypeStruct((bs, n_heads, dk, dv), F32)],
        compiler_params=_cparams("parallel"),
        name="retention_sample",
    )(rq, rk, rv, rg, rng, state, decay_in, qdec_b, kdec_b, cdec_b)


def _moba_prompt_kernel(q_ref, kTd_ref, vd_ref, kT_ref, v_ref, eneg_ref, o_ref, km_scr, kaug_scr,
                        *, blk, hd, n_blocks, topk):
    c = pl.program_id(2)
    width = 2 * hd
    scale = hd ** -0.5

    @pl.when(c == 0)
    def _():
        lane = lax.broadcasted_iota(jnp.int32, (width, LANES), 1)
        km = jnp.zeros((width, LANES), F32)
        for n in range(n_blocks):
            s = jnp.sum(kT_ref[0, :, n * blk:(n + 1) * blk], axis=1, keepdims=True) * (1.0 / blk)
            km = jnp.where(lane == n, s, km)
        km_scr[...] = km.T
        kaug_scr[0:width, :] = kT_ref[0].astype(BF16)
        kaug_scr[width:width + LANES, :] = eneg_ref[...]

    q = q_ref[...]
    lane_q = lax.broadcasted_iota(jnp.int32, (blk, width), 1)
    row_s = lax.broadcasted_iota(jnp.int32, (blk, blk), 0)
    col_s = lax.broadcasted_iota(jnp.int32, (blk, blk), 1)
    kmbT = km_scr[...].astype(BF16)
    kd = kTd_ref[0].astype(BF16)
    vd = vd_ref[...]

    nbp = -(-n_blocks // 8) * 8
    row_b = lax.broadcasted_iota(jnp.int32, (nbp, blk), 0)
    eye = (row_s == col_s).astype(BF16)
    heads = []
    for j in range(2):
        qh = jnp.where((lane_q // hd) == j, q, jnp.zeros_like(q))
        gate = lax.dot_general(kmbT, qh, _NT, preferred_element_type=F32)[0:nbp, :]
        gate = jnp.where(row_b < c, gate, NEG)
        cnt = jnp.zeros((nbp, blk), jnp.int32)
        for m in range(n_blocks):
            gm = gate[m:m + 1, :]
            beats = (gm > gate) | ((gm == gate) & (m < row_b))
            cnt = cnt + beats.astype(jnp.int32)
        sel = (row_b < c) & (cnt < topk)
        penT = jnp.where(sel | (row_b >= n_blocks), 0.0, 1.0).astype(BF16)
        penT = jnp.concatenate([penT, jnp.zeros((LANES - nbp, blk), BF16)], axis=0)
        pen = lax.dot_general(eye, penT, _NT, preferred_element_type=F32).astype(BF16)
        qs = qh * scale
        s_d = jnp.dot(qs, kd, preferred_element_type=F32)
        s_d = jnp.where(col_s <= row_s, s_d, -jnp.inf)
        heads.append((jnp.concatenate([qs, pen], axis=1), s_d, jnp.max(s_d, axis=1, keepdims=True)))

    def finish(n_wide):
        outs = []
        for qa, s_d, m_d in heads:
            if n_wide:
                w = n_wide * blk
                s_w = jnp.dot(qa, kaug_scr[:, :w], preferred_element_type=F32)
                mx = jnp.maximum(m_d, jnp.max(s_w, axis=1, keepdims=True))
                p_w = jnp.exp(s_w - mx)
                p_d = jnp.exp(s_d - mx)
                den = jnp.sum(p_d, axis=1, keepdims=True) + jnp.sum(p_w, axis=1, keepdims=True)
                acc = (jnp.dot(p_d.astype(BF16), vd, preferred_element_type=F32)
                       + jnp.dot(p_w.astype(BF16), v_ref[0:w, :], preferred_element_type=F32))
            else:
                p_d = jnp.exp(s_d - m_d)
                den = jnp.sum(p_d, axis=1, keepdims=True)
                acc = jnp.dot(p_d.astype(BF16), vd, preferred_element_type=F32)
            outs.append(acc / den)
        o_ref[...] = jnp.where(lane_q < hd, outs[0], outs[1]).astype(BF16)

    widths = sorted({w for w in (2, 4) if w < n_blocks - 1} | ({n_blocks - 1} if n_blocks > 1 else set()))
    lo = 1
    for n_wide in widths:
        @pl.when((c >= lo) & (c <= n_wide))
        def _(n_wide=n_wide):
            finish(n_wide)
        lo = n_wide + 1

    @pl.when(c == 0)
    def _():
        finish(0)


def _moba_prompt(mq, mv, kT, *, batch, seq, n_heads, hd):
    blk = MOBA_BLOCK
    nb = seq // blk
    width = 2 * hd
    npair = n_heads // 2
    assert nb <= LANES and width == LANES
    eneg = jnp.where(jnp.arange(LANES)[:, None] == (jnp.arange(seq) // blk)[None, :], NEG, 0.0).astype(BF16)
    return pl.pallas_call(
        functools.partial(_moba_prompt_kernel, blk=blk, hd=hd, n_blocks=nb, topk=MOBA_TOPK),
        grid=(batch, npair, nb),
        in_specs=[
            pl.BlockSpec((blk, width), lambda b, hp, c: (b * nb + c, hp)),
            pl.BlockSpec((1, width, blk), lambda b, hp, c: (b, hp, c)),
            pl.BlockSpec((blk, width), lambda b, hp, c: (b * nb + c, hp)),
            pl.BlockSpec((1, width, seq), lambda b, hp, c: (b, hp, 0)),
            pl.BlockSpec((seq, width), lambda b, hp, c: (b, hp)),
            pl.BlockSpec((LANES, seq), lambda b, hp, c: (0, 0)),
        ],
        out_specs=pl.BlockSpec((blk, width), lambda b, hp, c: (b * nb + c, hp)),
        out_shape=jax.ShapeDtypeStruct((batch * seq, n_heads * hd), BF16),
        scratch_shapes=[pltpu.VMEM((width, LANES), F32), pltpu.VMEM((width + LANES, seq), BF16)],
        compiler_params=_cparams("parallel", "parallel", "arbitrary"),
        name="moba_prompt",
    )(mq, kT, mv, kT, mv, eneg)


def _moba_gate_kernel(pt_ref, q_ref, *refs, pages_per_step, pages_per_block, n_heads, hd, ls, n_past, topk):
    del pt_ref
    pages = refs[:pages_per_step]
    idx_ref = refs[pages_per_step]
    km_scr = refs[pages_per_step + 1]
    j = pl.program_id(1)
    nj = pl.num_programs(1)
    d_moba = n_heads * hd
    page_size = pages[0].shape[-1]
    blocks_per_step = pages_per_step // pages_per_block

    @pl.when(j == 0)
    def _():
        km_scr[...] = jnp.zeros_like(km_scr)

    col = lax.broadcasted_iota(jnp.int32, (page_size, LANES), 1)
    acc = km_scr[...]
    for u in range(blocks_per_step):
        ks = pages[u * pages_per_block][0]
        for pg in range(1, pages_per_block):
            ks = ks + pages[u * pages_per_block + pg][0]
        ks = ks.reshape(d_moba, page_size).astype(BF16)
        onecol = (col == j * blocks_per_step + u).astype(BF16)
        acc = acc + jnp.dot(ks, onecol, preferred_element_type=F32)
    km_scr[...] = acc

    @pl.when(j == nj - 1)
    def _():
        kmean = (km_scr[...] * (1.0 / (pages_per_block * page_size))).astype(BF16)
        q = q_ref[0]
        rows = ls * n_heads
        qrep = jnp.concatenate([jnp.broadcast_to(q[i:i + 1, :], (n_heads, d_moba)) for i in range(ls)], axis=0)
        r = lax.broadcasted_iota(jnp.int32, (rows, d_moba), 0)
        cc = lax.broadcasted_iota(jnp.int32, (rows, d_moba), 1)
        qbd = jnp.where((cc // hd) == (r % n_heads), qrep, 0.0).astype(BF16)
        gate = jnp.dot(qbd, kmean, preferred_element_type=F32)
        lane = lax.broadcasted_iota(jnp.int32, (rows, LANES), 1)
        work = jnp.where(lane < n_past, gate, jnp.where(lane == n_past, NEG, -jnp.inf))
        out = jnp.zeros((rows, LANES), jnp.int32)
        for t in range(topk):
            mx = jnp.max(work, axis=1, keepdims=True)
            it = jnp.min(jnp.where(work == mx, lane, LANES), axis=1, keepdims=True)
            out = jnp.where(lane == t, it, out)
            work = jnp.where(lane == it, -jnp.inf, work)
        idx_ref[0] = out


def _moba_gate(page_table, mq_s, cacheT, *, n_heads, hd, ls, pages_per_step):
    bs, n_pages = page_table.shape
    page_size = cacheT.shape[-1]
    ppb = MOBA_BLOCK // page_size
    n_past = n_pages // ppb
    steps = n_pages // pages_per_step
    d_moba = n_heads * hd

    def page_spec(i):
        return pl.BlockSpec((1, n_heads, hd, page_size),
                            lambda b, j, pt: (pt[b * n_pages + j * pages_per_step + i], 0, 0, 0))

    grid_spec = pltpu.PrefetchScalarGridSpec(
        num_scalar_prefetch=1,
        grid=(bs, steps),
        in_specs=[pl.BlockSpec((1, ls, d_moba), lambda b, j, pt: (b, 0, 0))]
        + [page_spec(i) for i in range(pages_per_step)],
        out_specs=pl.BlockSpec((1, ls * n_heads, LANES), lambda b, j, pt: (b, 0, 0)),
        scratch_shapes=[pltpu.VMEM((d_moba, LANES), F32)],
    )
    return pl.pallas_call(
        functools.partial(_moba_gate_kernel, pages_per_step=pages_per_step, pages_per_block=ppb,
                          n_heads=n_heads, hd=hd, ls=ls, n_past=n_past, topk=MOBA_TOPK),
        grid_spec=grid_spec,
        out_shape=jax.ShapeDtypeStruct((bs, ls * n_heads, LANES), jnp.int32),
        compiler_params=_cparams("parallel", "arbitrary"),
        name="moba_sample_gate",
    )(page_table.reshape(-1), mq_s, *([cacheT] * pages_per_step))


def _moba_sample_kernel(pt_ref, idx_ref, qc_ref, kc_ref, vc_ref, ck_ref, cv_ref, o_ref, kbuf, vbuf, sems,
                        *, ls, topk, ppb, n_heads, hd, n_past, n_pages):
    b = pl.program_id(0)
    nb = pl.num_programs(0)
    cur = b % 2
    per_head = ls * topk * ppb
    rows_per_head = per_head // n_heads
    scale = hd ** -0.5

    def slab_at(buf_ref, buf, h, r):
        return buf_ref.at[buf, h * rows_per_head + r // n_heads, r % n_heads]

    def issue(bb, buf):
        def per_head_body(h, c):
            for i in range(ls):
                for j in range(topk):
                    blk = jnp.minimum(idx_ref[((bb * ls + i) * n_heads + h) * topk + j], n_past - 1)
                    for pg in range(ppb):
                        page = pt_ref[bb * n_pages + blk * ppb + pg]
                        r = (i * topk + j) * ppb + pg
                        pltpu.make_async_copy(ck_ref.at[page, h], slab_at(kbuf, buf, h, r), sems.at[buf]).start()
                        pltpu.make_async_copy(cv_ref.at[page, h], slab_at(vbuf, buf, h, r), sems.at[buf]).start()
            return c

        lax.fori_loop(0, n_heads, per_head_body, 0)

    @pl.when(b == 0)
    def _():
        issue(b, cur)

    @pl.when(b + 1 < nb)
    def _():
        issue(b + 1, 1 - cur)

    pltpu.make_async_copy(ck_ref.at[pl.ds(0, per_head)], kbuf.at[cur], sems.at[cur]).wait()
    pltpu.make_async_copy(cv_ref.at[pl.ds(0, per_head)], vbuf.at[cur], sems.at[cur]).wait()

    lane_own = lax.broadcasted_iota(jnp.int32, (1, ls), 1)

    def head_body(h, c):
        qc = qc_ref[0, h]
        knew = kc_ref[0, h]
        vnew = vc_ref[0, h]
        for i in range(ls):
            qcol = qc[:, i:i + 1]
            s_list = []
            for j in range(topk):
                ok = idx_ref[((b * ls + i) * n_heads + h) * topk + j] < n_past
                for pg in range(ppb):
                    kt = slab_at(kbuf, cur, h, (i * topk + j) * ppb + pg)[...]
                    s = jnp.sum(kt * qcol, axis=0, keepdims=True) * scale
                    s_list.append(jnp.where(ok, s, -jnp.inf))
            s_own = jnp.sum(knew * qcol, axis=0, keepdims=True) * scale
            s_own = jnp.where(lane_own <= i, s_own, -jnp.inf)
            mx = jnp.max(s_own, axis=1, keepdims=True)
            for s in s_list:
                mx = jnp.maximum(mx, jnp.max(s, axis=1, keepdims=True))
            p_own = jnp.exp(s_own - mx)
            denom = jnp.sum(p_own, axis=1, keepdims=True)
            o = jnp.sum(vnew * p_own, axis=1, keepdims=True)
            pv = None
            for t, s in enumerate(s_list):
                p = jnp.exp(s - mx)
                denom = denom + jnp.sum(p, axis=1, keepdims=True)
                term = slab_at(vbuf, cur, h, i * topk * ppb + t)[...] * p
                pv = term if pv is None else pv + term
            o = o + jnp.sum(pv, axis=1, keepdims=True)
            o_ref[0, h, :, i:i + 1] = o / denom
        return c

    lax.fori_loop(0, n_heads, head_body, 0)


def _moba_sample(page_table, idx, q_cols, k_cols, v_cols, cacheT_k, cacheT_v, *, n_heads, hd, ls):
    bs, n_pages = page_table.shape
    page_size = cacheT_k.shape[-1]
    ppb = MOBA_BLOCK // page_size
    n_past = n_pages // ppb
    topk = MOBA_TOPK
    per_head = ls * topk * ppb
    assert per_head % n_heads == 0 and cacheT_k.shape[0] >= per_head
    col_spec = pl.BlockSpec((1, n_heads, hd, ls), lambda b, pt, ix: (b, 0, 0, 0))
    any_spec = pl.BlockSpec(memory_space=pl.ANY)
    slab_buf = pltpu.VMEM((2, per_head, n_heads, hd, page_size), F32)
    grid_spec = pltpu.PrefetchScalarGridSpec(
        num_scalar_prefetch=2,
        grid=(bs,),
        in_specs=[col_spec, col_spec, col_spec, any_spec, any_spec],
        out_specs=col_spec,
        scratch_shapes=[slab_buf, slab_buf, pltpu.SemaphoreType.DMA((2,))],
    )
    return pl.pallas_call(
        functools.partial(_moba_sample_kernel, ls=ls, topk=topk, ppb=ppb, n_heads=n_heads, hd=hd, n_past=n_past,
                          n_pages=n_pages),
        grid_spec=grid_spec,
        out_shape=jax.ShapeDtypeStruct((bs, n_heads, hd, ls), F32),
        compiler_params=_cparams("arbitrary"),
        name="moba_sample_attn",
    )(page_table.reshape(-1), idx, q_cols, k_cols, v_cols, cacheT_k, cacheT_v)


def _merge_kernel(*refs, n_in_tiles, **kw):
    i = pl.program_id(0)

    @pl.when(i < n_in_tiles)
    def _():
        _merge_body(*refs, **kw)

    @pl.when(i >= n_in_tiles)
    def _():
        hres_ref, xn_ref, slot_ref, w_ref, cnt_ref = refs[-5:]
        hres_ref[...] = jnp.zeros_like(hres_ref)
        xn_ref[...] = jnp.zeros_like(xn_ref)
        slot_ref[...] = jnp.full(slot_ref.shape, -1, jnp.int32)
        w_ref[...] = jnp.zeros_like(w_ref)
        cnt_ref[...] = jnp.zeros_like(cnt_ref)


def _merge_body(x_ref, ret_ref, mo_ref, wo_ref, n2_ref, rwT_ref, rb_ref, *refs, n_experts, top_k, n_valid):
    hres_ref, xn_ref, slot_ref, w_ref, cnt_ref = refs[-5:]
    tm = x_ref.shape[0]
    d_ret = ret_ref.shape[1]
    mix = (jnp.dot(ret_ref[...], wo_ref[0:d_ret, :], preferred_element_type=F32)
           + jnp.dot(mo_ref[...], wo_ref[d_ret:, :], preferred_element_type=F32))
    hres = x_ref[...] + mix
    hres_ref[...] = hres
    xn = _rms(hres, n2_ref[...]).astype(BF16)
    xn_ref[...] = xn
    logits = lax.dot_general(rwT_ref[...], xn, _NT, preferred_element_type=F32) + rb_ref[...]
    row = lax.broadcasted_iota(jnp.int32, logits.shape, 0)
    work = logits
    vals, hots = [], []
    for _ in range(top_k):
        mx = jnp.max(work, axis=0, keepdims=True)
        it = jnp.min(jnp.where(work == mx, row, n_experts), axis=0, keepdims=True)
        hot = row == it
        vals.append(mx)
        hots.append(hot)
        work = jnp.where(hot, -jnp.inf, work)
    exps = [jnp.exp(v - vals[0]) for v in vals]
    denom = exps[0]
    for e in exps[1:]:
        denom = denom + e

    valid = lax.broadcasted_iota(jnp.int32, (1, tm), 1) < n_valid
    mask = hots[0]
    for hot in hots[1:]:
        mask = mask | hot
    mask = mask & valid
    maskf = mask.astype(F32)
    r_i = lax.broadcasted_iota(jnp.int32, (tm, tm), 0)
    c_i = lax.broadcasted_iota(jnp.int32, (tm, tm), 1)
    rank = jnp.dot(maskf.astype(BF16), (r_i < c_i).astype(BF16), preferred_element_type=F32)
    cnt = jnp.sum(maskf, axis=1, keepdims=True).astype(jnp.int32)
    cnt_pad = ((cnt + (SLOT_ALIGN - 1)) // SLOT_ALIGN) * SLOT_ALIGN
    e_r = lax.broadcasted_iota(jnp.int32, (n_experts, n_experts), 0)
    e_c = lax.broadcasted_iota(jnp.int32, (n_experts, n_experts), 1)
    cpb = jnp.broadcast_to(cnt_pad.astype(F32), (n_experts, LANES)).astype(BF16)
    tile_off = jnp.dot((e_c < e_r).astype(BF16), cpb, preferred_element_type=F32)[:, 0:1]
    slot = tile_off + rank

    row8 = lax.broadcasted_iota(jnp.int32, (8, tm), 0)
    slot_out = jnp.full((8, tm), -1, jnp.int32)
    w_out = jnp.zeros((8, tm), F32)
    for k in range(top_k):
        sk = jnp.sum(jnp.where(hots[k], slot, 0.0), axis=0, keepdims=True).astype(jnp.int32)
        sk = jnp.where(valid, sk, -1)
        slot_out = jnp.where(row8 == k, sk, slot_out)
        w_out = jnp.where(row8 == k, exps[k] / denom, w_out)
    slot_ref[...] = slot_out
    w_ref[...] = w_out
    cnt_ref[0] = jnp.broadcast_to(cnt, (n_experts, LANES))


def _merge(x, ret, moba_o, w_out, norm2_g, rwT, router_b, prev, *,
           tm, n_tiles_total, tile0, n_valid, n_clear_tiles=0):
    t, d_model = x.shape
    n_experts = rwT.shape[0]
    t_pad = n_tiles_total * tm
    n_in = t // tm
    tok_in = lambda n: pl.BlockSpec((tm, n), lambda i: (jnp.minimum(i, n_in - 1), 0))
    tok = lambda n: pl.BlockSpec((tm, n), lambda i: (tile0 + i, 0))
    full = lambda a: pl.BlockSpec(a.shape, lambda i: (0,) * a.ndim)
    out_shape = [jax.ShapeDtypeStruct((t_pad, d_model), F32), jax.ShapeDtypeStruct((t_pad, d_model), BF16),
                 jax.ShapeDtypeStruct((8, t_pad), jnp.int32), jax.ShapeDtypeStruct((8, t_pad), F32),
                 jax.ShapeDtypeStruct((n_tiles_total, n_experts, LANES), jnp.int32)]
    out_specs = [tok(d_model), tok(d_model),
                 pl.BlockSpec((8, tm), lambda i: (0, tile0 + i)), pl.BlockSpec((8, tm), lambda i: (0, tile0 + i)),
                 pl.BlockSpec((1, n_experts, LANES), lambda i: (tile0 + i, 0, 0))]
    args = [x, ret, moba_o, w_out, norm2_g, rwT, router_b]
    in_specs = [tok_in(d_model), tok_in(ret.shape[1]), tok_in(moba_o.shape[1]),
                full(w_out), full(norm2_g), full(rwT), full(router_b)]
    aliases = {}
    if prev is not None:
        aliases = {len(args) + k: k for k in range(len(prev))}
        args += list(prev)
        in_specs += [pl.BlockSpec(memory_space=pl.ANY)] * len(prev)
    return pl.pallas_call(
        functools.partial(_merge_kernel, n_experts=n_experts, top_k=TOP_K, n_valid=n_valid, n_in_tiles=n_in),
        grid=(n_in + n_clear_tiles,),
        in_specs=in_specs,
        out_specs=out_specs,
        out_shape=out_shape,
        input_output_aliases=aliases,
        compiler_params=_cparams("parallel"),
        name="merge_router",
    )(*args)


def _run_copies(cp_ref, to_ref, dr_ref, t, n_experts, make_copy):
    def per_expert(e, carry):
        rows = cp_ref[t * n_experts + e]
        src0 = to_ref[t * n_experts + e]
        dst0 = dr_ref[t * n_experts + e]
        n_big = rows // RUN_PIECE
        done = n_big * RUN_PIECE

        def big(j, c):
            make_copy(pl.multiple_of(src0 + j * RUN_PIECE, SLOT_ALIGN),
                      pl.multiple_of(dst0 + j * RUN_PIECE, SLOT_ALIGN), RUN_PIECE).start()
            return c

        def small(j, c):
            make_copy(pl.multiple_of(src0 + done + j * SLOT_ALIGN, SLOT_ALIGN),
                      pl.multiple_of(dst0 + done + j * SLOT_ALIGN, SLOT_ALIGN), SLOT_ALIGN).start()
            return c

        lax.fori_loop(0, n_big, big, 0)
        lax.fori_loop(0, (rows - done) // SLOT_ALIGN, small, 0)
        return carry

    lax.fori_loop(0, n_experts, per_expert, 0)


def _run_rows(cp_ref, t, n_experts):
    return lax.fori_loop(0, n_experts, lambda e, c: c + cp_ref[t * n_experts + e], 0)


def _wait_rows(rows, make_copy):
    lax.fori_loop(0, rows // WAIT_PIECE, lambda j, c: (make_copy(0, 0, WAIT_PIECE).wait(), c)[1], 0)
    lax.fori_loop(0, (rows % WAIT_PIECE) // SLOT_ALIGN, lambda j, c: (make_copy(0, 0, SLOT_ALIGN).wait(), c)[1], 0)


def _dispatch_kernel(cp_ref, to_ref, dr_ref, ts_ref, tn_ref, xn_ref, slot_ref, xs_ref, slots_scr, zero_scr, sems,
                     *, n_experts, n_slots, chunk, top_k):
    t = pl.program_id(0)
    nt = pl.num_programs(0)
    cur = t % 2
    tm = xn_ref.shape[0]
    xn = xn_ref[...]
    sl = slot_ref[...]
    used = to_ref[t * n_experts + n_experts - 1] + cp_ref[t * n_experts + n_experts - 1]
    def sort_chunk(c):
        s_iota = c * chunk + lax.broadcasted_iota(jnp.int32, (chunk, tm), 0)
        pm = sl[0:1, :] == s_iota
        for k in range(1, top_k):
            pm = pm | (sl[k:k + 1, :] == s_iota)
        slots_scr[cur, c * chunk:(c + 1) * chunk, :] = jnp.dot(
            pm.astype(BF16), xn, preferred_element_type=F32).astype(BF16)

    for c in range(n_slots // chunk):
        if (c + 1) * chunk <= tm * top_k:
            sort_chunk(c)
        else:
            pl.when(c * chunk < used)(functools.partial(sort_chunk, c))

    def copy_from(buf):
        def make_copy(src, dst, rows):
            return pltpu.make_async_copy(slots_scr.at[buf, pl.ds(src, rows)], xs_ref.at[pl.ds(dst, rows)],
                                         sems.at[buf])
        return make_copy

    _run_copies(cp_ref, to_ref, dr_ref, t, n_experts, copy_from(cur))

    @pl.when(t > 0)
    def _():
        _wait_rows(_run_rows(cp_ref, t - 1, n_experts), copy_from(1 - cur))

    @pl.when(t == nt - 1)
    def _():
        zero_scr[...] = jnp.zeros_like(zero_scr)

        def zcopy(dst):
            return pltpu.make_async_copy(zero_scr.at[pl.ds(0, SLOT_ALIGN)], xs_ref.at[pl.ds(dst, SLOT_ALIGN)],
                                         sems.at[cur])

        def per_expert(e, tot):
            def one(j, c):
                zcopy(pl.multiple_of(ts_ref[e] + j * SLOT_ALIGN, SLOT_ALIGN)).start()
                return c
            lax.fori_loop(0, tn_ref[e], one, 0)
            return tot + tn_ref[e]

        n_tail = lax.fori_loop(0, n_experts, per_expert, 0)
        _wait_rows(_run_rows(cp_ref, t, n_experts) + n_tail * SLOT_ALIGN, copy_from(cur))

        tme = zero_scr.shape[0]

        def ztile(j):
            return pltpu.make_async_copy(zero_scr, xs_ref.at[pl.ds(pl.multiple_of(j * tme, tme), tme)], sems.at[cur])

        n_used = tn_ref[n_experts]
        n_all = xs_ref.shape[0] // tme
        lax.fori_loop(n_used, n_all, lambda j, c: (ztile(j).start(), c)[1], 0)
        lax.fori_loop(n_used, n_all, lambda j, c: (ztile(j).wait(), c)[1], 0)


def _dispatch(cp, to, dr, ts, tn, xn_all, slot, *, tm, tme, n_experts, n_slots, p_rows):
    t_pad, d_model = xn_all.shape
    grid_spec = pltpu.PrefetchScalarGridSpec(
        num_scalar_prefetch=5,
        grid=(t_pad // tm,),
        in_specs=[pl.BlockSpec((tm, d_model), lambda i, *_: (i, 0)),
                  pl.BlockSpec((8, tm), lambda i, *_: (0, i))],
        out_specs=pl.BlockSpec(memory_space=pl.ANY),
        scratch_shapes=[pltpu.VMEM((2, n_slots, d_model), BF16), pltpu.VMEM((tme, d_model), BF16),
                        pltpu.SemaphoreType.DMA((2,))],
    )
    return pl.pallas_call(
        functools.partial(_dispatch_kernel, n_experts=n_experts, n_slots=n_slots, chunk=256, top_k=TOP_K),
        grid_spec=grid_spec,
        out_shape=jax.ShapeDtypeStruct((p_rows, d_model), BF16),
        compiler_params=_cparams("arbitrary"),
        name="moe_dispatch",
    )(cp, to, dr, ts, tn, xn_all, slot)


def _expert_kernel(te_ref, tv_ref, xi_ref, x_ref, wgu_ref, bgu_ref, wd_ref, bd_ref, y_ref, wp_scr, wdb_scr, *, d_ff):
    j = pl.program_id(0)
    grp = 2 * LANES
    n_grp = 2 * d_ff // grp

    @pl.when(tv_ref[j] > 0)
    def _():
        first = (j == 0) | (te_ref[j] != te_ref[jnp.maximum(j - 1, 0)])

        @pl.when(first)
        def _():
            r = lax.broadcasted_iota(jnp.int32, (grp, grp), 0)
            c = lax.broadcasted_iota(jnp.int32, (grp, grp), 1)
            perm = (((c < LANES) & (r == 2 * c)) | ((c >= LANES) & (r == 2 * (c - LANES) + 1))).astype(BF16)
            for g in range(n_grp):
                wp_scr[:, g * grp:(g + 1) * grp] = jnp.dot(
                    wgu_ref[0, :, g * grp:(g + 1) * grp].astype(BF16), perm, preferred_element_type=F32).astype(BF16)
            wdb_scr[...] = wd_ref[0].astype(BF16)

        x = x_ref[...]
        parts = []
        for g in range(n_grp):
            u = jnp.dot(x, wp_scr[:, g * grp:(g + 1) * grp], preferred_element_type=F32) + bgu_ref[0, :, g * grp:(g + 1) * grp]
            glu = jnp.minimum(u[:, :LANES], SWIGLU_LIMIT)
            lin = jnp.clip(u[:, LANES:], -SWIGLU_LIMIT, SWIGLU_LIMIT)
            parts.append((glu * jax.nn.sigmoid(SWIGLU_ALPHA * glu) * (lin + 1.0)).astype(BF16))
        a = jnp.concatenate(parts, axis=1)
        y_ref[...] = (jnp.dot(a, wdb_scr[...], preferred_element_type=F32) + bd_ref[0]).astype(BF16)

    @pl.when(tv_ref[j] == 0)
    def _():
        y_ref[...] = jnp.zeros_like(y_ref)


def _experts(te, tv, xi, x_sorted, wgu, bgu_perm, wd, bd, *, tme):
    p_rows, d_model = x_sorted.shape
    n_experts, _, d_ff2 = wgu.shape
    d_ff = d_ff2 // 2
    ex = lambda a: pl.BlockSpec((1,) + a.shape[1:], lambda j, te_, tv_, xi_: (te_[j],) + (0,) * (a.ndim - 1))
    grid_spec = pltpu.PrefetchScalarGridSpec(
        num_scalar_prefetch=3,
        grid=(p_rows // tme,),
        in_specs=[pl.BlockSpec((tme, d_model), lambda j, te_, tv_, xi_: (xi_[j], 0)),
                  ex(wgu), ex(bgu_perm), ex(wd), ex(bd)],
        out_specs=pl.BlockSpec((tme, d_model), lambda j, *_: (j, 0)),
        scratch_shapes=[pltpu.VMEM((d_model, d_ff2), BF16), pltpu.VMEM((d_ff, d_model), BF16)],
    )
    return pl.pallas_call(
        functools.partial(_expert_kernel, d_ff=d_ff),
        grid_spec=grid_spec,
        out_shape=jax.ShapeDtypeStruct((p_rows, d_model), BF16),
        compiler_params=_cparams("arbitrary"),
        name="moe_experts",
    )(te, tv, xi, x_sorted, wgu, bgu_perm, wd, bd)


def _combine_kernel(cp_ref, to_ref, dr_ref, hres_ref, slotT_ref, wT_ref, fg_ref, ys_ref, yp_ref, ysm_ref,
                    slots_scr, sems, *, n_experts, n_slots, chunk, top_k, n_prompt_tiles):
    t = pl.program_id(0)
    nt = pl.num_programs(0)
    cur = t % 2
    tm, d_model = hres_ref.shape

    def copy_into(buf):
        def make_copy(src, dst, rows):
            return pltpu.make_async_copy(ys_ref.at[pl.ds(dst, rows)], slots_scr.at[buf, pl.ds(src, rows)],
                                         sems.at[buf])
        return make_copy

    def fetch(tile, buf):
        _run_copies(cp_ref, to_ref, dr_ref, tile, n_experts, copy_into(buf))
        used = to_ref[tile * n_experts + n_experts - 1] + cp_ref[tile * n_experts + n_experts - 1]

        def zero_one(j, c):
            slots_scr[buf, pl.ds(pl.multiple_of(used + j * SLOT_ALIGN, SLOT_ALIGN), SLOT_ALIGN), :] = jnp.zeros(
                (SLOT_ALIGN, d_model), BF16)
            return c

        lax.fori_loop(0, (n_slots - used) // SLOT_ALIGN, zero_one, 0)

    @pl.when(t == 0)
    def _():
        fetch(t, cur)

    @pl.when(t + 1 < nt)
    def _():
        fetch(t + 1, 1 - cur)

    _wait_rows(_run_rows(cp_ref, t, n_experts), copy_into(cur))

    sl = slotT_ref[...]
    wt = wT_ref[...]
    acc = hres_ref[...]
    for c in range(n_slots // chunk):
        s_iota = c * chunk + lax.broadcasted_iota(jnp.int32, (tm, chunk), 1)
        pw = jnp.where(sl[:, 0:1] == s_iota, wt[:, 0:1], 0.0)
        for k in range(1, top_k):
            pw = pw + jnp.where(sl[:, k:k + 1] == s_iota, wt[:, k:k + 1], 0.0)
        acc = acc + jnp.dot(pw.astype(BF16), slots_scr[cur, c * chunk:(c + 1) * chunk, :],
                            preferred_element_type=F32)
    y = _rms(acc, fg_ref[...])

    @pl.when(t < n_prompt_tiles)
    def _():
        yp_ref[...] = y

    @pl.when(t >= n_prompt_tiles)
    def _():
        ysm_ref[...] = y[:ysm_ref.shape[0], :]


def _combine(cp, to, dr, hres_all, slotT, wT, fg, y_sorted, *, tm, n_experts, n_slots, n_prompt_tiles, n_sample):
    t_pad, d_model = hres_all.shape
    npt = n_prompt_tiles
    grid_spec = pltpu.PrefetchScalarGridSpec(
        num_scalar_prefetch=3,
        grid=(t_pad // tm,),
        in_specs=[pl.BlockSpec((tm, d_model), lambda i, *_: (i, 0)),
                  pl.BlockSpec((tm, 8), lambda i, *_: (i, 0)),
                  pl.BlockSpec((tm, 8), lambda i, *_: (i, 0)),
                  pl.BlockSpec(fg.shape, lambda i, *_: (0, 0)),
                  pl.BlockSpec(memory_space=pl.ANY)],
        out_specs=[pl.BlockSpec((tm, d_model), lambda i, *_: (jnp.minimum(i, npt - 1), 0)),
                   pl.BlockSpec((n_sample, d_model), lambda i, *_: (0, 0))],
        scratch_shapes=[pltpu.VMEM((2, n_slots, d_model), BF16), pltpu.SemaphoreType.DMA((2,))],
    )
    return pl.pallas_call(
        functools.partial(_combine_kernel, n_experts=n_experts, n_slots=n_slots, chunk=256, top_k=TOP_K,
                          n_prompt_tiles=npt),
        grid_spec=grid_spec,
        out_shape=[jax.ShapeDtypeStruct((npt * tm, d_model), F32), jax.ShapeDtypeStruct((n_sample, d_model), F32)],
        compiler_params=_cparams("arbitrary"),
        name="moe_combine",
    )(cp, to, dr, hres_all, slotT, wT, fg, y_sorted)


def _route_tables(cnt, *, tme, n_row_tiles):
    n_tiles, n_experts = cnt.shape
    cnt_pad = (cnt + (SLOT_ALIGN - 1)) // SLOT_ALIGN * SLOT_ALIGN
    tile_off = jnp.cumsum(cnt_pad, axis=1) - cnt_pad
    tot = jnp.sum(cnt_pad, axis=0)
    tot_t = (tot + (tme - 1)) // tme * tme
    ends = jnp.cumsum(tot_t)
    base = ends - tot_t
    dst_row = base[None, :] + jnp.cumsum(cnt_pad, axis=0) - cnt_pad
    j = jnp.arange(n_row_tiles, dtype=jnp.int32)
    te = jnp.minimum(jnp.sum(((ends // tme)[None, :] <= j[:, None]).astype(jnp.int32), axis=1), n_experts - 1)
    n_used = ends[-1] // tme
    tv = (j < n_used).astype(jnp.int32)
    xi = jnp.minimum(j, n_used - 1)
    tail_start = base + tot
    tail_n = jnp.concatenate([(tot_t - tot) // SLOT_ALIGN, n_used[None]])
    flat = lambda a: a.reshape(-1).astype(jnp.int32)
    return (flat(cnt_pad), flat(tile_off), flat(dst_row), flat(tail_start), flat(tail_n),
            flat(te), flat(tv), flat(xi))


def kernel(x_prompt, x_sample, cache_k, cache_v, state_ret, page_table, norm1_g, w_in, ret_norm_g, w_out,
           norm2_g, router_w, router_b, w_gate_up, b_gate_up, w_down, b_down, final_norm_g):
    bp, lp, d_model = x_prompt.shape
    bs, ls, _ = x_sample.shape
    depth = w_in.shape[0]
    assert depth == 1, "single-layer step"
    n_pages = page_table.shape[1]
    page_size, mh, hd = cache_k.shape[2], cache_k.shape[3], cache_k.shape[4]
    rh, dk, dv = state_ret.shape[2], state_ret.shape[3], state_ret.shape[4]
    d_ret = rh * dk
    d_moba = mh * hd
    n_experts = router_w.shape[2]
    d_ff = w_down.shape[2]
    past_len = n_pages * page_size
    layer = 0

    w = w_in[layer]
    w_all = w.astype(BF16)
    w_main = jnp.concatenate([w[:, :4 * d_ret + d_moba], w[:, 4 * d_ret + 2 * d_moba:]], axis=1).astype(BF16)
    wkvT = w[:, 4 * d_ret + d_moba:].T.astype(BF16)
    g1 = norm1_g[layer][None, :]
    wo = w_out[layer].astype(BF16)
    rng = ret_norm_g[layer][None, :]
    n2 = norm2_g[layer][None, :]
    rwT = router_w[layer].T.astype(BF16)
    rb = router_b[layer][:, None]
    wgu = w_gate_up[layer]
    bgu_perm = b_gate_up[layer].reshape(n_experts, -1, LANES, 2).transpose(0, 1, 3, 2).reshape(n_experts, 1, 2 * d_ff)
    wd = w_down[layer]
    bd = b_down[layer][:, None, :]
    fg = final_norm_g[None, :]

    tm = _token_tile(lp)
    pos_p = jnp.arange(lp, dtype=jnp.int32)
    cos_p, sin_p = _rope_tables(pos_p, dk)
    xp = x_prompt.reshape(bp * lp, d_model)
    assert dk == dv
    ret, mq, mv, kT, vT, s_p = _inproj_prompt(
        xp, g1, w_main, wkvT, cos_p, sin_p, rng,
        batch=bp, seq=lp, d_ret=d_ret, d_moba=d_moba, n_ret_heads=rh, tm=tm)
    moba_o = _moba_prompt(mq, mv, kT, batch=bp, seq=lp, n_heads=mh, hd=hd)
    n_sample = bs * ls
    assert n_sample <= tm, "sample group must fit one token tile"
    npt = bp * lp // tm
    n_tiles = npt + 1
    bufs = _merge(xp, ret, moba_o, wo, n2, rwT, rb, None,
                  tm=tm, n_tiles_total=n_tiles, tile0=0, n_valid=tm, n_clear_tiles=1)

    pos_s = past_len + jnp.arange(ls, dtype=jnp.int32)
    cos_s, sin_s = _rope_tables(jnp.tile(pos_s, bs), dk)
    xs = x_sample.reshape(bs * ls, d_model)
    rq_s, rk_s, rv_s, rg_s, mq_s, mk_s, mv_s = _inproj_sample(
        xs, g1, w_all, cos_s, sin_s, d_ret=d_ret, d_moba=d_moba, n_ret_heads=rh)
    r3 = lambda a: a.reshape(bs, ls, a.shape[1])
    ret_s, s_s = _ret_sample(r3(rq_s), r3(rk_s), r3(rv_s), r3(rg_s), rng, state_ret[layer],
                             n_heads=rh, dk=dk, dv=dv)
    ckT = cache_k[layer].transpose(0, 2, 3, 1)
    cvT = cache_v[layer].transpose(0, 2, 3, 1)
    idx_pad = _moba_gate(page_table, r3(mq_s), ckT, n_heads=mh, hd=hd, ls=ls, pages_per_step=min(32, n_pages))
    idx = idx_pad[:, :, :MOBA_TOPK].reshape(-1)
    cols = lambda a: a.reshape(bs, ls, mh, hd).transpose(0, 2, 3, 1)
    o_cols = _moba_sample(page_table, idx, cols(mq_s), cols(mk_s), cols(mv_s), ckT, cvT,
                          n_heads=mh, hd=hd, ls=ls)
    moba_o_s = o_cols.transpose(0, 3, 1, 2).reshape(bs * ls, d_moba).astype(BF16)
    padt = lambda a: jnp.pad(a, ((0, tm - n_sample), (0, 0)))
    hres_all, xn_all, slot, wts, cnt = _merge(
        padt(xs), padt(ret_s.reshape(n_sample, d_ret).astype(BF16)), padt(moba_o_s), wo, n2, rwT, rb, bufs,
        tm=tm, n_tiles_total=n_tiles, tile0=npt, n_valid=n_sample)

    tme = 512
    n_slots = -(-(tm * TOP_K + n_experts * (SLOT_ALIGN - 1)) // 256) * 256
    n_assign = TOP_K * (bp * lp + n_sample)
    n_row_tiles = -(-(n_assign + (SLOT_ALIGN - 1) * n_experts * n_tiles + n_experts * (tme - 1)) // tme)
    cp, to, dr, ts, tn, te, tv, xi = _route_tables(cnt[:, :, 0], tme=tme, n_row_tiles=n_row_tiles)
    x_sorted = _dispatch(cp, to, dr, ts, tn, xn_all, slot, tm=tm, tme=tme, n_experts=n_experts, n_slots=n_slots,
                         p_rows=n_row_tiles * tme)
    y_sorted = _experts(te, tv, xi, x_sorted, wgu, bgu_perm, wd, bd, tme=tme)
    y_p, y_s = _combine(cp, to, dr, hres_all, slot.T, wts.T, fg, y_sorted, tm=tm, n_experts=n_experts,
                        n_slots=n_slots, n_prompt_tiles=npt, n_sample=n_sample)

    y_prompt = y_p.reshape(bp, lp, d_model)
    y_sample = y_s.reshape(bs, ls, d_model)
    k_prompt = kT.reshape(bp, mh, hd, lp).transpose(0, 3, 1, 2)[None]
    v_prompt = vT.reshape(bp, mh, hd, lp).transpose(0, 3, 1, 2)[None]
    k_sample = mk_s.reshape(1, bs, ls, mh, hd)
    v_sample = mv_s.reshape(1, bs, ls, mh, hd)
    return (y_prompt, y_sample, k_prompt, v_prompt, s_p[None], k_sample, v_sample, s_s[None])
```

```python
import functools

import jax
import jax.numpy as jnp
from jax import lax
from jax.experimental import pallas as pl
from jax.experimental.pallas import tpu as pltpu

F32 = jnp.float32
BF16 = jnp.bfloat16

EPS = 1e-5
ROPE_BASE = 10000.0
RET_CHUNK = 128
MOBA_BLOCK = 256
MOBA_TOPK = 3
TOP_K = 4
SWIGLU_LIMIT = 7.0
SWIGLU_ALPHA = 1.702
NEG = -1e30
LANES = 128
SLOT_ALIGN = 16
RUN_PIECE = 64
WAIT_PIECE = 256
VMEM_LIMIT = 56 * 1024 * 1024

_NT = (((1,), (1,)), ((), ()))


def _cparams(*sem):
    return pltpu.CompilerParams(dimension_semantics=sem, vmem_limit_bytes=VMEM_LIMIT)


def _token_tile(seq):
    for tm in (512, 256, 128):
        if seq % tm == 0:
            return tm
    raise ValueError(f"sequence length {seq} must be a multiple of 128")


def _rms(x, g):
    return x * lax.rsqrt(jnp.mean(x * x, axis=-1, keepdims=True) + EPS) * g


def _rope_tables(pos, dk):
    half = dk // 2
    inv = ROPE_BASE ** (-jnp.arange(half, dtype=F32) / half)
    ang = pos.astype(F32)[:, None] * inv[None, :]
    c = jnp.cos(ang)
    s = jnp.sin(ang)
    return jnp.concatenate([c, c], axis=1), jnp.concatenate([-s, s], axis=1)


def _rotary(z, cos2, sin2, n_heads, dk):
    outs = []
    for h in range(n_heads):
        s = z[:, h * dk:(h + 1) * dk]
        outs.append(s * cos2 + pltpu.roll(s, dk // 2, 1) * sin2)
    return jnp.concatenate(outs, axis=1)


def _head_norm_gate(o, g, rg):
    mu = jnp.mean(o, axis=-1, keepdims=True)
    var = jnp.mean(jnp.square(o - mu), axis=-1, keepdims=True)
    return (o - mu) * lax.rsqrt(var + EPS) * g * (rg * jax.nn.sigmoid(rg))


def _inproj_prompt_kernel(x_ref, g_ref, w_ref, wkvT_ref, cos_ref, sin_ref, rng_ref,
                          din_ref, qdec_ref, kdec_ref, cdec_ref,
                          ret_ref, mq_ref, mv_ref, kT_ref, vT_ref, s_out_ref, s_scr,
                          *, d_ret, d_moba, n_ret_heads, chunk):
    i = pl.program_id(1)
    dk = d_ret // n_ret_heads
    tm = x_ref.shape[0]
    xn = _rms(x_ref[...], g_ref[...]).astype(BF16)
    cos2 = cos_ref[...]
    sin2 = sin_ref[...]

    def proj(c0, n):
        return jnp.dot(xn, w_ref[:, c0:c0 + n], preferred_element_type=F32)

    mq_ref[...] = proj(4 * d_ret, d_moba).astype(BF16)
    mv_ref[...] = proj(4 * d_ret + d_moba, d_moba).astype(BF16)
    kT_ref[0] = lax.dot_general(wkvT_ref[0:d_moba, :], xn, _NT, preferred_element_type=F32)
    vT_ref[0] = lax.dot_general(wkvT_ref[d_moba:2 * d_moba, :], xn, _NT, preferred_element_type=F32)

    rq = _rotary(proj(0, d_ret), cos2, sin2, n_ret_heads, dk)
    rk = _rotary(proj(d_ret, d_ret), cos2, sin2, n_ret_heads, dk) * (dk ** -0.5)
    rv = proj(2 * d_ret, d_ret)
    rg = proj(3 * d_ret, d_ret)
    rng = rng_ref[...]

    @pl.when(i == 0)
    def _():
        s_scr[...] = jnp.zeros_like(s_scr)

    for h in range(n_ret_heads):
        hs = slice(h * dk, (h + 1) * dk)
        s = s_scr[h]
        for c in range(tm // chunk):
            rows = slice(c * chunk, (c + 1) * chunk)
            q = rq[rows, hs]
            k = rk[rows, hs]
            v = rv[rows, hs].astype(BF16)
            scores = lax.dot_general(q.astype(BF16), k.astype(BF16), _NT, preferred_element_type=F32) * din_ref[h]
            inner = jnp.dot(scores.astype(BF16), v, preferred_element_type=F32)
            cross = jnp.dot((q * qdec_ref[h]).astype(BF16), s.astype(BF16), preferred_element_type=F32)
            ret_ref[rows, hs] = _head_norm_gate(inner + cross, rng[:, hs], rg[rows, hs]).astype(BF16)
            kdT = (k * kdec_ref[h]).T.astype(BF16)
            s = cdec_ref[h] * s + jnp.dot(kdT, v, preferred_element_type=F32)
        s_scr[h] = s

    @pl.when(i == pl.num_programs(1) - 1)
    def _():
        s_out_ref[0] = s_scr[...]


def _inproj_prompt(x, g, w_main, wkvT, cos2, sin2, rng, *, batch, seq, d_ret, d_moba, n_ret_heads, tm):
    d_model = x.shape[1]
    nt = seq // tm
    dk = d_ret // n_ret_heads
    chunk = RET_CHUNK
    decay_in, q_dec, k_dec, chunk_dec = _ret_tables(n_ret_heads, chunk)
    qdec_b = jnp.broadcast_to(q_dec.T[:, :, None], (n_ret_heads, chunk, dk))
    kdec_b = jnp.broadcast_to(k_dec.T[:, :, None], (n_ret_heads, chunk, dk))
    cdec_b = jnp.broadcast_to(chunk_dec[:, None, None], (n_ret_heads, dk, dk))
    tok_spec = lambda n: pl.BlockSpec((tm, n), lambda b, i: (b * nt + i, 0))
    full = lambda a: pl.BlockSpec(a.shape, lambda b, i: (0,) * a.ndim)
    t = batch * seq
    out_shape = (
        [jax.ShapeDtypeStruct((t, d_ret), BF16)]
        + [jax.ShapeDtypeStruct((t, d_moba), BF16)] * 2
        + [jax.ShapeDtypeStruct((batch, d_moba, seq), F32)] * 2
        + [jax.ShapeDtypeStruct((batch, n_ret_heads, dk, dk), F32)]
    )
    out_specs = (
        [tok_spec(d_ret)] + [tok_spec(d_moba)] * 2
        + [pl.BlockSpec((1, d_moba, tm), lambda b, i: (b, 0, i))] * 2
        + [pl.BlockSpec((1, n_ret_heads, dk, dk), lambda b, i: (b, 0, 0, 0))]
    )
    return pl.pallas_call(
        functools.partial(_inproj_prompt_kernel, d_ret=d_ret, d_moba=d_moba, n_ret_heads=n_ret_heads, chunk=chunk),
        grid=(batch, nt),
        in_specs=[tok_spec(d_model), full(g), full(w_main), full(wkvT),
                  pl.BlockSpec((tm, dk), lambda b, i: (i, 0)), pl.BlockSpec((tm, dk), lambda b, i: (i, 0)),
                  full(rng), full(decay_in), full(qdec_b), full(kdec_b), full(cdec_b)],
        out_specs=out_specs,
        out_shape=out_shape,
        scratch_shapes=[pltpu.VMEM((n_ret_heads, dk, dk), F32)],
        compiler_params=_cparams("parallel", "arbitrary"),
        name="inproj_retention_prompt",
    )(x, g, w_main, wkvT, cos2, sin2, rng, decay_in, qdec_b, kdec_b, cdec_b)


def _inproj_sample_kernel(x_ref, g_ref, w_ref, cos_ref, sin_ref,
                          rq_ref, rk_ref, rv_ref, rg_ref, mq_ref, mk_ref, mv_ref,
                          *, d_ret, d_moba, n_ret_heads):
    dk = d_ret // n_ret_heads
    xn = _rms(x_ref[...], g_ref[...]).astype(BF16)
    cos2 = cos_ref[...]
    sin2 = sin_ref[...]

    def proj(c0, n):
        return jnp.dot(xn, w_ref[:, c0:c0 + n], preferred_element_type=F32)

    rq_ref[...] = _rotary(proj(0, d_ret), cos2, sin2, n_ret_heads, dk)
    rk_ref[...] = _rotary(proj(d_ret, d_ret), cos2, sin2, n_ret_heads, dk) * (dk ** -0.5)
    rv_ref[...] = proj(2 * d_ret, d_ret)
    rg_ref[...] = proj(3 * d_ret, d_ret)
    mq_ref[...] = proj(4 * d_ret, d_moba)
    mk_ref[...] = proj(4 * d_ret + d_moba, d_moba)
    mv_ref[...] = proj(4 * d_ret + 2 * d_moba, d_moba)


def _inproj_sample(x, g, w_all, cos2, sin2, *, d_ret, d_moba, n_ret_heads):
    t = x.shape[0]
    args = (x, g, w_all, cos2, sin2)
    full = lambda a: pl.BlockSpec(a.shape, lambda i: (0,) * a.ndim)
    out_shape = [jax.ShapeDtypeStruct((t, d_ret), F32)] * 4 + [jax.ShapeDtypeStruct((t, d_moba), F32)] * 3
    return pl.pallas_call(
        functools.partial(_inproj_sample_kernel, d_ret=d_ret, d_moba=d_moba, n_ret_heads=n_ret_heads),
        grid=(1,),
        in_specs=[full(a) for a in args],
        out_specs=[pl.BlockSpec(s.shape, lambda i: (0, 0)) for s in out_shape],
        out_shape=out_shape,
        compiler_params=_cparams("arbitrary"),
        name="inproj_sample",
    )(*args)


def _ret_tables(n_heads, chunk):
    log_g = jnp.log1p(-jnp.exp2(-5.0 - jnp.arange(n_heads, dtype=F32)))
    n = jnp.arange(chunk, dtype=F32)
    rel = n[:, None] - n[None, :]
    causal = rel >= 0
    decay_in = jnp.where(causal[None], jnp.exp(jnp.where(causal, rel, 0.0)[None] * log_g[:, None, None]), 0.0)
    q_dec = jnp.exp((n[:, None] + 1.0) * log_g[None, :])
    k_dec = jnp.exp((chunk - 1.0 - n[:, None]) * log_g[None, :])
    chunk_dec = jnp.exp(chunk * log_g)
    return decay_in, q_dec, k_dec, chunk_dec


def _ret_sample_kernel(rq_ref, rk_ref, rv_ref, rg_ref, rng_ref, s_ref, din_ref, qdec_ref, kdec_ref, cdec_ref,
                       o_ref, s_out_ref, *, n_heads, dk, dv):
    row = lax.broadcasted_iota(jnp.int32, (dk, dk), 0)
    col = lax.broadcasted_iota(jnp.int32, (dk, dk), 1)
    eye = (row == col).astype(BF16)
    rng = rng_ref[...]
    for h in range(n_heads):
        q = rq_ref[0, :, h * dk:(h + 1) * dk]
        k = rk_ref[0, :, h * dk:(h + 1) * dk]
        v = rv_ref[0, :, h * dv:(h + 1) * dv].astype(BF16)
        s = s_ref[0, h]
        scores = lax.dot_general(q.astype(BF16), k.astype(BF16), _NT, preferred_element_type=F32) * din_ref[h]
        inner = jnp.dot(scores.astype(BF16), v, preferred_element_type=F32)
        cross = jnp.dot((q * qdec_ref[h]).astype(BF16), s.astype(BF16), preferred_element_type=F32)
        o_ref[0, :, h * dv:(h + 1) * dv] = _head_norm_gate(
            inner + cross, rng[:, h * dv:(h + 1) * dv], rg_ref[0, :, h * dv:(h + 1) * dv])
        kd = (k * kdec_ref[h]).astype(BF16)
        kdT = lax.dot_general(eye, kd, _NT, preferred_element_type=F32).astype(BF16)
        s_out_ref[0, h] = cdec_ref[h] * s + jnp.dot(kdT, v, preferred_element_type=F32)


def _ret_sample(rq, rk, rv, rg, rng, state, *, n_heads, dk, dv):
    bs, ls, _ = rq.shape
    decay_in, q_dec, k_dec, chunk_dec = _ret_tables(n_heads, ls)
    qdec_b = jnp.broadcast_to(q_dec.T[:, :, None], (n_heads, ls, dk))
    kdec_b = jnp.broadcast_to(k_dec.T[:, :, None], (n_heads, ls, dk))
    cdec_b = jnp.broadcast_to(chunk_dec[:, None, None], (n_heads, dk, dv))
    tok = lambda n: pl.BlockSpec((1, ls, n), lambda b: (b, 0, 0))
    full = lambda a: pl.BlockSpec(a.shape, lambda b: (0,) * a.ndim)
    st = pl.BlockSpec((1, n_heads, dk, dv), lambda b: (b, 0, 0, 0))
    return pl.pallas_call(
        functools.partial(_ret_sample_kernel, n_heads=n_heads, dk=dk, dv=dv),
        grid=(bs,),
        in_specs=[tok(n_heads * dk), tok(n_heads * dk), tok(n_heads * dv), tok(n_heads * dv), full(rng), st,
                  full(decay_in), full(qdec_b), full(kdec_b), full(cdec_b)],
        out_specs=[tok(n_heads * dv), st],
        out_shape=[jax.ShapeDtypeStruct((bs, ls, n_heads * dv), F32),
                   jax.ShapeDtypeStruct((bs, n_heads, dk, dv), F32)],
        compiler_params=_cparams("parallel"),
        name="retention_sample",
    )(rq, rk, rv, rg, rng, state, decay_in, qdec_b, kdec_b, cdec_b)


def _moba_prompt_kernel(q_ref, kTd_ref, vd_ref, kT_ref, v_ref, eneg_ref, cbias_ref, o_ref, km_scr, kaug_scr,
                        *, blk, hd, n_blocks, topk):
    c = pl.program_id(2)
    width = 2 * hd
    scale = hd ** -0.5

    @pl.when(c == 0)
    def _():
        lane = lax.broadcasted_iota(jnp.int32, (width, LANES), 1)
        km = jnp.zeros((width, LANES), F32)
        for n in range(n_blocks):
            s = jnp.sum(kT_ref[0, :, n * blk:(n + 1) * blk], axis=1, keepdims=True) * (1.0 / blk)
            km = jnp.where(lane == n, s, km)
        km_scr[...] = km.T
        kaug_scr[0:width, :] = kT_ref[0].astype(BF16)
        kaug_scr[width:width + LANES, :] = eneg_ref[...]

    q = q_ref[...]
    lane_q = lax.broadcasted_iota(jnp.int32, (blk, width), 1)
    zq = jnp.zeros_like(q)
    qst = jnp.concatenate([jnp.where(lane_q < hd, q, zq), jnp.where(lane_q >= hd, q, zq)], axis=0)
    kmbT = km_scr[...].astype(BF16)
    kd = kTd_ref[0].astype(BF16)
    vd = vd_ref[...]

    nbp = -(-n_blocks // 8) * 8
    row_b = lax.broadcasted_iota(jnp.int32, (nbp, 2 * blk), 0)
    gate = lax.dot_general(kmbT, qst, _NT, preferred_element_type=F32)[0:nbp, :]
    gate = jnp.where(row_b < c, gate, NEG)
    cnt = jnp.zeros((nbp, 2 * blk), jnp.int32)
    for m in range(n_blocks):
        gm = gate[m:m + 1, :]
        beats = (gm > gate) | ((gm == gate) & (m < row_b))
        cnt = cnt + beats.astype(jnp.int32)
    sel = (row_b < c) & (cnt < topk)
    penT = jnp.where(sel | (row_b >= n_blocks), 0.0, 1.0)
    penT = jnp.concatenate([penT, jnp.zeros((LANES - nbp, 2 * blk), F32)], axis=0)
    pen = penT.T.astype(BF16)
    qs = qst * scale
    qa = jnp.concatenate([qs, pen], axis=1)
    s_d = jnp.dot(qs, kd, preferred_element_type=F32) + cbias_ref[...]
    m_d = jnp.max(s_d, axis=1, keepdims=True)

    def finish(n_wide):
        outs = []
        for j in range(2):
            rows = slice(j * blk, (j + 1) * blk)
            s_dj, m_dj = s_d[rows], m_d[rows]
            if n_wide:
                w = n_wide * blk
                s_w = jnp.dot(qa[rows], kaug_scr[:, :w], preferred_element_type=F32)
                mx = jnp.maximum(m_dj, jnp.max(s_w, axis=1, keepdims=True))
                p_w = jnp.exp(s_w - mx)
                p_d = jnp.exp(s_dj - mx)
                den = jnp.sum(p_d, axis=1, keepdims=True) + jnp.sum(p_w, axis=1, keepdims=True)
                acc = (jnp.dot(p_d.astype(BF16), vd, preferred_element_type=F32)
                       + jnp.dot(p_w.astype(BF16), v_ref[0:w, :], preferred_element_type=F32))
            else:
                p_d = jnp.exp(s_dj - m_dj)
                den = jnp.sum(p_d, axis=1, keepdims=True)
                acc = jnp.dot(p_d.astype(BF16), vd, preferred_element_type=F32)
            outs.append(acc / den)
        o_ref[...] = jnp.where(lane_q < hd, outs[0], outs[1]).astype(BF16)

    widths = sorted({w for w in (2, 4) if w < n_blocks - 1} | ({n_blocks - 1} if n_blocks > 1 else set()))
    lo = 1
    for n_wide in widths:
        @pl.when((c >= lo) & (c <= n_wide))
        def _(n_wide=n_wide):
            finish(n_wide)
        lo = n_wide + 1

    @pl.when(c == 0)
    def _():
        finish(0)


def _moba_prompt(mq, mv, kT, *, batch, seq, n_heads, hd):
    blk = MOBA_BLOCK
    nb = seq // blk
    width = 2 * hd
    npair = n_heads // 2
    assert nb <= LANES and width == LANES
    eneg = jnp.where(jnp.arange(LANES)[:, None] == (jnp.arange(seq) // blk)[None, :], NEG, 0.0).astype(BF16)
    cbias = jnp.where(jnp.arange(blk)[None, :] <= (jnp.arange(2 * blk) % blk)[:, None], 0.0, -jnp.inf).astype(F32)
    return pl.pallas_call(
        functools.partial(_moba_prompt_kernel, blk=blk, hd=hd, n_blocks=nb, topk=MOBA_TOPK),
        grid=(batch, npair, nb),
        in_specs=[
            pl.BlockSpec((blk, width), lambda b, hp, c: (b * nb + c, hp)),
            pl.BlockSpec((1, width, blk), lambda b, hp, c: (b, hp, c)),
            pl.BlockSpec((blk, width), lambda b, hp, c: (b * nb + c, hp)),
            pl.BlockSpec((1, width, seq), lambda b, hp, c: (b, hp, 0)),
            pl.BlockSpec((seq, width), lambda b, hp, c: (b, hp)),
            pl.BlockSpec((LANES, seq), lambda b, hp, c: (0, 0)),
            pl.BlockSpec((2 * blk, blk), lambda b, hp, c: (0, 0)),
        ],
        out_specs=pl.BlockSpec((blk, width), lambda b, hp, c: (b * nb + c, hp)),
        out_shape=jax.ShapeDtypeStruct((batch * seq, n_heads * hd), BF16),
        scratch_shapes=[pltpu.VMEM((width, LANES), F32), pltpu.VMEM((width + LANES, seq), BF16)],
        compiler_params=_cparams("parallel", "parallel", "arbitrary"),
        name="moba_prompt",
    )(mq, kT, mv, kT, mv, eneg, cbias)


def _moba_gate_kernel(pt_ref, q_ref, *refs, pages_per_step, pages_per_block, n_heads, hd, ls, n_past, topk):
    del pt_ref
    pages = refs[:pages_per_step]
    idx_ref = refs[pages_per_step]
    km_scr = refs[pages_per_step + 1]
    j = pl.program_id(1)
    nj = pl.num_programs(1)
    d_moba = n_heads * hd
    page_size = pages[0].shape[-1]
    blocks_per_step = pages_per_step // pages_per_block

    @pl.when(j == 0)
    def _():
        km_scr[...] = jnp.zeros_like(km_scr)

    col = lax.broadcasted_iota(jnp.int32, (page_size, LANES), 1)
    acc = km_scr[...]
    for u in range(blocks_per_step):
        ks = pages[u * pages_per_block][0]
        for pg in range(1, pages_per_block):
            ks = ks + pages[u * pages_per_block + pg][0]
        ks = ks.reshape(d_moba, page_size).astype(BF16)
        onecol = (col == j * blocks_per_step + u).astype(BF16)
        acc = acc + jnp.dot(ks, onecol, preferred_element_type=F32)
    km_scr[...] = acc

    @pl.when(j == nj - 1)
    def _():
        kmean = (km_scr[...] * (1.0 / (pages_per_block * page_size))).astype(BF16)
        q = q_ref[0]
        rows = ls * n_heads
        qrep = jnp.concatenate([jnp.broadcast_to(q[i:i + 1, :], (n_heads, d_moba)) for i in range(ls)], axis=0)
        r = lax.broadcasted_iota(jnp.int32, (rows, d_moba), 0)
        cc = lax.broadcasted_iota(jnp.int32, (rows, d_moba), 1)
        qbd = jnp.where((cc // hd) == (r % n_heads), qrep, 0.0).astype(BF16)
        gate = jnp.dot(qbd, kmean, preferred_element_type=F32)
        lane = lax.broadcasted_iota(jnp.int32, (rows, LANES), 1)
        work = jnp.where(lane < n_past, gate, jnp.where(lane == n_past, NEG, -jnp.inf))
        out = jnp.zeros((rows, LANES), jnp.int32)
        for t in range(topk):
            mx = jnp.max(work, axis=1, keepdims=True)
            it = jnp.min(jnp.where(work == mx, lane, LANES), axis=1, keepdims=True)
            out = jnp.where(lane == t, it, out)
            work = jnp.where(lane == it, -jnp.inf, work)
        idx_ref[0] = out


def _moba_gate(page_table, mq_s, cacheT, *, n_heads, hd, ls, pages_per_step):
    bs, n_pages = page_table.shape
    page_size = cacheT.shape[-1]
    ppb = MOBA_BLOCK // page_size
    n_past = n_pages // ppb
    steps = n_pages // pages_per_step
    d_moba = n_heads * hd

    def page_spec(i):
        return pl.BlockSpec((1, n_heads, hd, page_size),
                            lambda b, j, pt: (pt[b * n_pages + j * pages_per_step + i], 0, 0, 0))

    grid_spec = pltpu.PrefetchScalarGridSpec(
        num_scalar_prefetch=1,
        grid=(bs, steps),
        in_specs=[pl.BlockSpec((1, ls, d_moba), lambda b, j, pt: (b, 0, 0))]
        + [page_spec(i) for i in range(pages_per_step)],
        out_specs=pl.BlockSpec((1, ls * n_heads, LANES), lambda b, j, pt: (b, 0, 0)),
        scratch_shapes=[pltpu.VMEM((d_moba, LANES), F32)],
    )
    return pl.pallas_call(
        functools.partial(_moba_gate_kernel, pages_per_step=pages_per_step, pages_per_block=ppb,
                          n_heads=n_heads, hd=hd, ls=ls, n_past=n_past, topk=MOBA_TOPK),
        grid_spec=grid_spec,
        out_shape=jax.ShapeDtypeStruct((bs, ls * n_heads, LANES), jnp.int32),
        compiler_params=_cparams("parallel", "arbitrary"),
        name="moba_sample_gate",
    )(page_table.reshape(-1), mq_s, *([cacheT] * pages_per_step))


def _moba_sample_kernel(pt_ref, idx_ref, qc_ref, kc_ref, vc_ref, ck_ref, cv_ref, o_ref, kbuf, vbuf, sems,
                        *, ls, topk, ppb, n_heads, hd, n_past, n_pages):
    b = pl.program_id(0)
    nb = pl.num_programs(0)
    cur = b % 2
    per_head = ls * topk * ppb
    rows_per_head = per_head // n_heads
    scale = hd ** -0.5

    def slab_at(buf_ref, buf, h, r):
        return buf_ref.at[buf, h * rows_per_head + r // n_heads, r % n_heads]

    def issue(bb, buf):
        def per_head_body(h, c):
            for i in range(ls):
                for j in range(topk):
                    blk = jnp.minimum(idx_ref[((bb * ls + i) * n_heads + h) * topk + j], n_past - 1)
                    for pg in range(ppb):
                        page = pt_ref[bb * n_pages + blk * ppb + pg]
                        r = (i * topk + j) * ppb + pg
                        pltpu.make_async_copy(ck_ref.at[page, h], slab_at(kbuf, buf, h, r), sems.at[buf]).start()
                        pltpu.make_async_copy(cv_ref.at[page, h], slab_at(vbuf, buf, h, r), sems.at[buf]).start()
            return c

        lax.fori_loop(0, n_heads, per_head_body, 0)

    @pl.when(b == 0)
    def _():
        issue(b, cur)

    @pl.when(b + 1 < nb)
    def _():
        issue(b + 1, 1 - cur)

    pltpu.make_async_copy(ck_ref.at[pl.ds(0, per_head)], kbuf.at[cur], sems.at[cur]).wait()
    pltpu.make_async_copy(cv_ref.at[pl.ds(0, per_head)], vbuf.at[cur], sems.at[cur]).wait()

    lane_own = lax.broadcasted_iota(jnp.int32, (1, ls), 1)

    def head_body(h, c):
        qc = qc_ref[0, h]
        knew = kc_ref[0, h]
        vnew = vc_ref[0, h]
        for i in range(ls):
            qcol = qc[:, i:i + 1]
            s_list = []
            for j in range(topk):
                ok = idx_ref[((b * ls + i) * n_heads + h) * topk + j] < n_past
                for pg in range(ppb):
                    kt = slab_at(kbuf, cur, h, (i * topk + j) * ppb + pg)[...]
                    s = jnp.sum(kt * qcol, axis=0, keepdims=True) * scale
                    s_list.append(jnp.where(ok, s, -jnp.inf))
            s_own = jnp.sum(knew * qcol, axis=0, keepdims=True) * scale
            s_own = jnp.where(lane_own <= i, s_own, -jnp.inf)
            mx = jnp.max(s_own, axis=1, keepdims=True)
            for s in s_list:
                mx = jnp.maximum(mx, jnp.max(s, axis=1, keepdims=True))
            p_own = jnp.exp(s_own - mx)
            denom = jnp.sum(p_own, axis=1, keepdims=True)
            o = jnp.sum(vnew * p_own, axis=1, keepdims=True)
            pv = None
            for t, s in enumerate(s_list):
                p = jnp.exp(s - mx)
                denom = denom + jnp.sum(p, axis=1, keepdims=True)
                term = slab_at(vbuf, cur, h, i * topk * ppb + t)[...] * p
                pv = term if pv is None else pv + term
            o = o + jnp.sum(pv, axis=1, keepdims=True)
            o_ref[0, h, :, i:i + 1] = o / denom
        return c

    lax.fori_loop(0, n_heads, head_body, 0)


def _moba_sample(page_table, idx, q_cols, k_cols, v_cols, cacheT_k, cacheT_v, *, n_heads, hd, ls):
    bs, n_pages = page_table.shape
    page_size = cacheT_k.shape[-1]
    ppb = MOBA_BLOCK // page_size
    n_past = n_pages // ppb
    topk = MOBA_TOPK
    per_head = ls * topk * ppb
    assert per_head % n_heads == 0 and cacheT_k.shape[0] >= per_head
    col_spec = pl.BlockSpec((1, n_heads, hd, ls), lambda b, pt, ix: (b, 0, 0, 0))
    any_spec = pl.BlockSpec(memory_space=pl.ANY)
    slab_buf = pltpu.VMEM((2, per_head, n_heads, hd, page_size), F32)
    grid_spec = pltpu.PrefetchScalarGridSpec(
        num_scalar_prefetch=2,
        grid=(bs,),
        in_specs=[col_spec, col_spec, col_spec, any_spec, any_spec],
        out_specs=col_spec,
        scratch_shapes=[slab_buf, slab_buf, pltpu.SemaphoreType.DMA((2,))],
    )
    return pl.pallas_call(
        functools.partial(_moba_sample_kernel, ls=ls, topk=topk, ppb=ppb, n_heads=n_heads, hd=hd, n_past=n_past,
                          n_pages=n_pages),
        grid_spec=grid_spec,
        out_shape=jax.ShapeDtypeStruct((bs, n_heads, hd, ls), F32),
        compiler_params=_cparams("arbitrary"),
        name="moba_sample_attn",
    )(page_table.reshape(-1), idx, q_cols, k_cols, v_cols, cacheT_k, cacheT_v)


def _merge_kernel(*refs, n_in_tiles, **kw):
    i = pl.program_id(0)

    @pl.when(i < n_in_tiles)
    def _():
        _merge_body(*refs, **kw)

    @pl.when(i >= n_in_tiles)
    def _():
        hres_ref, xn_ref, slot_ref, w_ref, cnt_ref = refs[-5:]
        hres_ref[...] = jnp.zeros_like(hres_ref)
        xn_ref[...] = jnp.zeros_like(xn_ref)
        slot_ref[...] = jnp.full(slot_ref.shape, -1, jnp.int32)
        w_ref[...] = jnp.zeros_like(w_ref)
        cnt_ref[...] = jnp.zeros_like(cnt_ref)


def _merge_body(x_ref, ret_ref, mo_ref, wo_ref, n2_ref, rwT_ref, rb_ref, *refs, n_experts, top_k, n_valid):
    hres_ref, xn_ref, slot_ref, w_ref, cnt_ref = refs[-5:]
    tm = x_ref.shape[0]
    d_ret = ret_ref.shape[1]
    mix = (jnp.dot(ret_ref[...], wo_ref[0:d_ret, :], preferred_element_type=F32)
           + jnp.dot(mo_ref[...], wo_ref[d_ret:, :], preferred_element_type=F32))
    hres = x_ref[...] + mix
    hres_ref[...] = hres
    xn = _rms(hres, n2_ref[...]).astype(BF16)
    xn_ref[...] = xn
    logits = lax.dot_general(rwT_ref[...], xn, _NT, preferred_element_type=F32) + rb_ref[...]
    row = lax.broadcasted_iota(jnp.int32, logits.shape, 0)
    work = logits
    vals, hots = [], []
    for _ in range(top_k):
        mx = jnp.max(work, axis=0, keepdims=True)
        it = jnp.min(jnp.where(work == mx, row, n_experts), axis=0, keepdims=True)
        hot = row == it
        vals.append(mx)
        hots.append(hot)
        work = jnp.where(hot, -jnp.inf, work)
    exps = [jnp.exp(v - vals[0]) for v in vals]
    denom = exps[0]
    for e in exps[1:]:
        denom = denom + e

    valid = lax.broadcasted_iota(jnp.int32, (1, tm), 1) < n_valid
    mask = hots[0]
    for hot in hots[1:]:
        mask = mask | hot
    mask = mask & valid
    maskf = mask.astype(F32)
    r_i = lax.broadcasted_iota(jnp.int32, (tm, tm), 0)
    c_i = lax.broadcasted_iota(jnp.int32, (tm, tm), 1)
    rank = jnp.dot(maskf.astype(BF16), (r_i < c_i).astype(BF16), preferred_element_type=F32)
    cnt = jnp.sum(maskf, axis=1, keepdims=True).astype(jnp.int32)
    cnt_pad = ((cnt + (SLOT_ALIGN - 1)) // SLOT_ALIGN) * SLOT_ALIGN
    e_r = lax.broadcasted_iota(jnp.int32, (n_experts, n_experts), 0)
    e_c = lax.broadcasted_iota(jnp.int32, (n_experts, n_experts), 1)
    cpb = jnp.broadcast_to(cnt_pad.astype(F32), (n_experts, LANES)).astype(BF16)
    tile_off = jnp.dot((e_c < e_r).astype(BF16), cpb, preferred_element_type=F32)[:, 0:1]
    slot = tile_off + rank

    row8 = lax.broadcasted_iota(jnp.int32, (8, tm), 0)
    slot_out = jnp.full((8, tm), -1, jnp.int32)
    w_out = jnp.zeros((8, tm), F32)
    for k in range(top_k):
        sk = jnp.sum(jnp.where(hots[k], slot, 0.0), axis=0, keepdims=True).astype(jnp.int32)
        sk = jnp.where(valid, sk, -1)
        slot_out = jnp.where(row8 == k, sk, slot_out)
        w_out = jnp.where(row8 == k, exps[k] / denom, w_out)
    slot_ref[...] = slot_out
    w_ref[...] = w_out
    cnt_ref[0] = jnp.broadcast_to(cnt, (n_experts, LANES))


def _merge(x, ret, moba_o, w_out, norm2_g, rwT, router_b, prev, *,
           tm, n_tiles_total, tile0, n_valid, n_clear_tiles=0):
    t, d_model = x.shape
    n_experts = rwT.shape[0]
    t_pad = n_tiles_total * tm
    n_in = t // tm
    tok_in = lambda n: pl.BlockSpec((tm, n), lambda i: (jnp.minimum(i, n_in - 1), 0))
    tok = lambda n: pl.BlockSpec((tm, n), lambda i: (tile0 + i, 0))
    full = lambda a: pl.BlockSpec(a.shape, lambda i: (0,) * a.ndim)
    out_shape = [jax.ShapeDtypeStruct((t_pad, d_model), F32), jax.ShapeDtypeStruct((t_pad, d_model), BF16),
                 jax.ShapeDtypeStruct((8, t_pad), jnp.int32), jax.ShapeDtypeStruct((8, t_pad), F32),
                 jax.ShapeDtypeStruct((n_tiles_total, n_experts, LANES), jnp.int32)]
    out_specs = [tok(d_model), tok(d_model),
                 pl.BlockSpec((8, tm), lambda i: (0, tile0 + i)), pl.BlockSpec((8, tm), lambda i: (0, tile0 + i)),
                 pl.BlockSpec((1, n_experts, LANES), lambda i: (tile0 + i, 0, 0))]
    args = [x, ret, moba_o, w_out, norm2_g, rwT, router_b]
    in_specs = [tok_in(d_model), tok_in(ret.shape[1]), tok_in(moba_o.shape[1]),
                full(w_out), full(norm2_g), full(rwT), full(router_b)]
    aliases = {}
    if prev is not None:
        aliases = {len(args) + k: k for k in range(len(prev))}
        args += list(prev)
        in_specs += [pl.BlockSpec(memory_space=pl.ANY)] * len(prev)
    return pl.pallas_call(
        functools.partial(_merge_kernel, n_experts=n_experts, top_k=TOP_K, n_valid=n_valid, n_in_tiles=n_in),
        grid=(n_in + n_clear_tiles,),
        in_specs=in_specs,
        out_specs=out_specs,
        out_shape=out_shape,
        input_output_aliases=aliases,
        compiler_params=_cparams("parallel"),
        name="merge_router",
    )(*args)


def _run_copies(cp_ref, to_ref, dr_ref, t, n_experts, make_copy):
    def per_expert(e, carry):
        rows = cp_ref[t * n_experts + e]
        src0 = to_ref[t * n_experts + e]
        dst0 = dr_ref[t * n_experts + e]
        n_big = rows // RUN_PIECE
        done = n_big * RUN_PIECE

        def big(j, c):
            make_copy(pl.multiple_of(src0 + j * RUN_PIECE, SLOT_ALIGN),
                      pl.multiple_of(dst0 + j * RUN_PIECE, SLOT_ALIGN), RUN_PIECE).start()
            return c

        def small(j, c):
            make_copy(pl.multiple_of(src0 + done + j * SLOT_ALIGN, SLOT_ALIGN),
                      pl.multiple_of(dst0 + done + j * SLOT_ALIGN, SLOT_ALIGN), SLOT_ALIGN).start()
            return c

        lax.fori_loop(0, n_big, big, 0)
        lax.fori_loop(0, (rows - done) // SLOT_ALIGN, small, 0)
        return carry

    lax.fori_loop(0, n_experts, per_expert, 0)


def _run_rows(cp_ref, t, n_experts):
    return lax.fori_loop(0, n_experts, lambda e, c: c + cp_ref[t * n_experts + e], 0)


def _wait_rows(rows, make_copy):
    lax.fori_loop(0, rows // WAIT_PIECE, lambda j, c: (make_copy(0, 0, WAIT_PIECE).wait(), c)[1], 0)
    lax.fori_loop(0, (rows % WAIT_PIECE) // SLOT_ALIGN, lambda j, c: (make_copy(0, 0, SLOT_ALIGN).wait(), c)[1], 0)


def _dispatch_kernel(cp_ref, to_ref, dr_ref, ts_ref, tn_ref, xn_ref, slot_ref, xs_ref, slots_scr, zero_scr, sems,
                     *, n_experts, n_slots, chunk, top_k):
    t = pl.program_id(0)
    nt = pl.num_programs(0)
    cur = t % 2
    tm = xn_ref.shape[0]
    xn = xn_ref[...]
    sl = slot_ref[...]
    used = to_ref[t * n_experts + n_experts - 1] + cp_ref[t * n_experts + n_experts - 1]
    def sort_chunk(c):
        s_iota = c * chunk + lax.broadcasted_iota(jnp.int32, (chunk, tm), 0)
        pm = sl[0:1, :] == s_iota
        for k in range(1, top_k):
            pm = pm | (sl[k:k + 1, :] == s_iota)
        slots_scr[cur, c * chunk:(c + 1) * chunk, :] = jnp.dot(
            pm.astype(BF16), xn, preferred_element_type=F32).astype(BF16)

    for c in range(n_slots // chunk):
        if (c + 1) * chunk <= tm * top_k:
            sort_chunk(c)
        else:
            pl.when(c * chunk < used)(functools.partial(sort_chunk, c))

    def copy_from(buf):
        def make_copy(src, dst, rows):
            return pltpu.make_async_copy(slots_scr.at[buf, pl.ds(src, rows)], xs_ref.at[pl.ds(dst, rows)],
                                         sems.at[buf])
        return make_copy

    _run_copies(cp_ref, to_ref, dr_ref, t, n_experts, copy_from(cur))

    @pl.when(t > 0)
    def _():
        _wait_rows(_run_rows(cp_ref, t - 1, n_experts), copy_from(1 - cur))

    @pl.when(t == nt - 1)
    def _():
        zero_scr[...] = jnp.zeros_like(zero_scr)

        def zcopy(dst):
            return pltpu.make_async_copy(zero_scr.at[pl.ds(0, SLOT_ALIGN)], xs_ref.at[pl.ds(dst, SLOT_ALIGN)],
                                         sems.at[cur])

        def per_expert(e, tot):
            def one(j, c):
                zcopy(pl.multiple_of(ts_ref[e] + j * SLOT_ALIGN, SLOT_ALIGN)).start()
                return c
            lax.fori_loop(0, tn_ref[e], one, 0)
            return tot + tn_ref[e]

        n_tail = lax.fori_loop(0, n_experts, per_expert, 0)
        _wait_rows(_run_rows(cp_ref, t, n_experts) + n_tail * SLOT_ALIGN, copy_from(cur))

        tme = zero_scr.shape[0]

        def ztile(j):
            return pltpu.make_async_copy(zero_scr, xs_ref.at[pl.ds(pl.multiple_of(j * tme, tme), tme)], sems.at[cur])

        n_used = tn_ref[n_experts]
        n_all = xs_ref.shape[0] // tme
        lax.fori_loop(n_used, n_all, lambda j, c: (ztile(j).start(), c)[1], 0)
        lax.fori_loop(n_used, n_all, lambda j, c: (ztile(j).wait(), c)[1], 0)


def _dispatch(cp, to, dr, ts, tn, xn_all, slot, *, tm, tme, n_experts, n_slots, p_rows):
    t_pad, d_model = xn_all.shape
    grid_spec = pltpu.PrefetchScalarGridSpec(
        num_scalar_prefetch=5,
        grid=(t_pad // tm,),
        in_specs=[pl.BlockSpec((tm, d_model), lambda i, *_: (i, 0)),
                  pl.BlockSpec((8, tm), lambda i, *_: (0, i))],
        out_specs=pl.BlockSpec(memory_space=pl.ANY),
        scratch_shapes=[pltpu.VMEM((2, n_slots, d_model), BF16), pltpu.VMEM((tme, d_model), BF16),
                        pltpu.SemaphoreType.DMA((2,))],
    )
    return pl.pallas_call(
        functools.partial(_dispatch_kernel, n_experts=n_experts, n_slots=n_slots, chunk=256, top_k=TOP_K),
        grid_spec=grid_spec,
        out_shape=jax.ShapeDtypeStruct((p_rows, d_model), BF16),
        compiler_params=_cparams("arbitrary"),
        name="moe_dispatch",
    )(cp, to, dr, ts, tn, xn_all, slot)


def _expert_kernel(te_ref, tv_ref, xi_ref, x_ref, wgu_ref, bgu_ref, wd_ref, bd_ref, y_ref, wp_scr, wdb_scr, *, d_ff):
    j = pl.program_id(0)
    grp = 2 * LANES
    n_grp = 2 * d_ff // grp

    @pl.when(tv_ref[j] > 0)
    def _():
        first = (j == 0) | (te_ref[j] != te_ref[jnp.maximum(j - 1, 0)])

        @pl.when(first)
        def _():
            r = lax.broadcasted_iota(jnp.int32, (grp, grp), 0)
            c = lax.broadcasted_iota(jnp.int32, (grp, grp), 1)
            perm = (((c < LANES) & (r == 2 * c)) | ((c >= LANES) & (r == 2 * (c - LANES) + 1))).astype(BF16)
            for g in range(n_grp):
                wp_scr[:, g * grp:(g + 1) * grp] = jnp.dot(
                    wgu_ref[0, :, g * grp:(g + 1) * grp].astype(BF16), perm, preferred_element_type=F32).astype(BF16)
            wdb_scr[...] = wd_ref[0].astype(BF16)

        def ffn(rows):
            x = x_ref[0:rows, :]
            parts = []
            for g in range(n_grp):
                u = (jnp.dot(x, wp_scr[:, g * grp:(g + 1) * grp], preferred_element_type=F32)
                     + bgu_ref[0, :, g * grp:(g + 1) * grp])
                glu = jnp.minimum(u[:, :LANES], SWIGLU_LIMIT)
                lin = jnp.clip(u[:, LANES:], -SWIGLU_LIMIT, SWIGLU_LIMIT)
                parts.append((glu * jax.nn.sigmoid(SWIGLU_ALPHA * glu) * (lin + 1.0)).astype(BF16))
            a = jnp.concatenate(parts, axis=1)
            y_ref[0:rows, :] = (jnp.dot(a, wdb_scr[...], preferred_element_type=F32) + bd_ref[0]).astype(BF16)

        tme = x_ref.shape[0]

        @pl.when(tv_ref[j] > tme // 2)
        def _():
            ffn(tme)

        @pl.when(tv_ref[j] <= tme // 2)
        def _():
            ffn(tme // 2)
            y_ref[tme // 2:, :] = jnp.zeros((tme - tme // 2, y_ref.shape[1]), y_ref.dtype)

    @pl.when(tv_ref[j] == 0)
    def _():
        y_ref[...] = jnp.zeros_like(y_ref)


def _experts(te, tv, xi, x_sorted, wgu, bgu_perm, wd, bd, *, tme):
    p_rows, d_model = x_sorted.shape
    n_experts, _, d_ff2 = wgu.shape
    d_ff = d_ff2 // 2
    ex = lambda a: pl.BlockSpec((1,) + a.shape[1:], lambda j, te_, tv_, xi_: (te_[j],) + (0,) * (a.ndim - 1))
    grid_spec = pltpu.PrefetchScalarGridSpec(
        num_scalar_prefetch=3,
        grid=(p_rows // tme,),
        in_specs=[pl.BlockSpec((tme, d_model), lambda j, te_, tv_, xi_: (xi_[j], 0)),
                  ex(wgu), ex(bgu_perm), ex(wd), ex(bd)],
        out_specs=pl.BlockSpec((tme, d_model), lambda j, *_: (j, 0)),
        scratch_shapes=[pltpu.VMEM((d_model, d_ff2), BF16), pltpu.VMEM((d_ff, d_model), BF16)],
    )
    return pl.pallas_call(
        functools.partial(_expert_kernel, d_ff=d_ff),
        grid_spec=grid_spec,
        out_shape=jax.ShapeDtypeStruct((p_rows, d_model), BF16),
        compiler_params=_cparams("arbitrary"),
        name="moe_experts",
    )(te, tv, xi, x_sorted, wgu, bgu_perm, wd, bd)


def _combine_kernel(cp_ref, to_ref, dr_ref, hres_ref, slotT_ref, wT_ref, fg_ref, ys_ref, yp_ref, ysm_ref,
                    slots_scr, sems, *, n_experts, n_slots, chunk, top_k, n_prompt_tiles):
    t = pl.program_id(0)
    nt = pl.num_programs(0)
    cur = t % 2
    tm, d_model = hres_ref.shape

    def copy_into(buf):
        def make_copy(src, dst, rows):
            return pltpu.make_async_copy(ys_ref.at[pl.ds(dst, rows)], slots_scr.at[buf, pl.ds(src, rows)],
                                         sems.at[buf])
        return make_copy

    def fetch(tile, buf):
        _run_copies(cp_ref, to_ref, dr_ref, tile, n_experts, copy_into(buf))
        used = to_ref[tile * n_experts + n_experts - 1] + cp_ref[tile * n_experts + n_experts - 1]

        def zero_one(j, c):
            slots_scr[buf, pl.ds(pl.multiple_of(used + j * SLOT_ALIGN, SLOT_ALIGN), SLOT_ALIGN), :] = jnp.zeros(
                (SLOT_ALIGN, d_model), BF16)
            return c

        lax.fori_loop(0, (n_slots - used) // SLOT_ALIGN, zero_one, 0)

    @pl.when(t == 0)
    def _():
        fetch(t, cur)

    @pl.when(t + 1 < nt)
    def _():
        fetch(t + 1, 1 - cur)

    _wait_rows(_run_rows(cp_ref, t, n_experts), copy_into(cur))

    sl = slotT_ref[...]
    wt = wT_ref[...]
    acc = hres_ref[...]
    for c in range(n_slots // chunk):
        s_iota = c * chunk + lax.broadcasted_iota(jnp.int32, (tm, chunk), 1)
        pw = jnp.where(sl[:, 0:1] == s_iota, wt[:, 0:1], 0.0)
        for k in range(1, top_k):
            pw = pw + jnp.where(sl[:, k:k + 1] == s_iota, wt[:, k:k + 1], 0.0)
        acc = acc + jnp.dot(pw.astype(BF16), slots_scr[cur, c * chunk:(c + 1) * chunk, :],
                            preferred_element_type=F32)
    y = _rms(acc, fg_ref[...])

    @pl.when(t < n_prompt_tiles)
    def _():
        yp_ref[...] = y

    @pl.when(t >= n_prompt_tiles)
    def _():
        ysm_ref[...] = y[:ysm_ref.shape[0], :]


def _combine(cp, to, dr, hres_all, slotT, wT, fg, y_sorted, *, tm, n_experts, n_slots, n_prompt_tiles, n_sample):
    t_pad, d_model = hres_all.shape
    npt = n_prompt_tiles
    grid_spec = pltpu.PrefetchScalarGridSpec(
        num_scalar_prefetch=3,
        grid=(t_pad // tm,),
        in_specs=[pl.BlockSpec((tm, d_model), lambda i, *_: (i, 0)),
                  pl.BlockSpec((tm, 8), lambda i, *_: (i, 0)),
                  pl.BlockSpec((tm, 8), lambda i, *_: (i, 0)),
                  pl.BlockSpec(fg.shape, lambda i, *_: (0, 0)),
                  pl.BlockSpec(memory_space=pl.ANY)],
        out_specs=[pl.BlockSpec((tm, d_model), lambda i, *_: (jnp.minimum(i, npt - 1), 0)),
                   pl.BlockSpec((n_sample, d_model), lambda i, *_: (0, 0))],
        scratch_shapes=[pltpu.VMEM((2, n_slots, d_model), BF16), pltpu.SemaphoreType.DMA((2,))],
    )
    return pl.pallas_call(
        functools.partial(_combine_kernel, n_experts=n_experts, n_slots=n_slots, chunk=256, top_k=TOP_K,
                          n_prompt_tiles=npt),
        grid_spec=grid_spec,
        out_shape=[jax.ShapeDtypeStruct((npt * tm, d_model), F32), jax.ShapeDtypeStruct((n_sample, d_model), F32)],
        compiler_params=_cparams("arbitrary"),
        name="moe_combine",
    )(cp, to, dr, hres_all, slotT, wT, fg, y_sorted)


def _route_tables(cnt, *, tme, n_row_tiles):
    n_tiles, n_experts = cnt.shape
    cnt_pad = (cnt + (SLOT_ALIGN - 1)) // SLOT_ALIGN * SLOT_ALIGN
    tile_off = jnp.cumsum(cnt_pad, axis=1) - cnt_pad
    tot = jnp.sum(cnt_pad, axis=0)
    tot_t = (tot + (tme - 1)) // tme * tme
    ends = jnp.cumsum(tot_t)
    base = ends - tot_t
    dst_row = base[None, :] + jnp.cumsum(cnt_pad, axis=0) - cnt_pad
    j = jnp.arange(n_row_tiles, dtype=jnp.int32)
    te = jnp.minimum(jnp.sum(((ends // tme)[None, :] <= j[:, None]).astype(jnp.int32), axis=1), n_experts - 1)
    n_used = ends[-1] // tme
    run_end = jnp.sum(jnp.where(te[:, None] == jnp.arange(n_experts)[None, :], (base + tot)[None, :], 0), axis=1)
    tv = jnp.where(j < n_used, jnp.clip(run_end - j * tme, 0, tme), 0)
    xi = jnp.minimum(j, n_used - 1)
    tail_start = base + tot
    tail_n = jnp.concatenate([(tot_t - tot) // SLOT_ALIGN, n_used[None]])
    flat = lambda a: a.reshape(-1).astype(jnp.int32)
    return (flat(cnt_pad), flat(tile_off), flat(dst_row), flat(tail_start), flat(tail_n),
            flat(te), flat(tv), flat(xi))


def kernel(x_prompt, x_sample, cache_k, cache_v, state_ret, page_table, norm1_g, w_in, ret_norm_g, w_out,
           norm2_g, router_w, router_b, w_gate_up, b_gate_up, w_down, b_down, final_norm_g):
    bp, lp, d_model = x_prompt.shape
    bs, ls, _ = x_sample.shape
    depth = w_in.shape[0]
    assert depth == 1, "single-layer step"
    n_pages = page_table.shape[1]
    page_size, mh, hd = cache_k.shape[2], cache_k.shape[3], cache_k.shape[4]
    rh, dk, dv = state_ret.shape[2], state_ret.shape[3], state_ret.shape[4]
    d_ret = rh * dk
    d_moba = mh * hd
    n_experts = router_w.shape[2]
    d_ff = w_down.shape[2]
    past_len = n_pages * page_size
    layer = 0

    w = w_in[layer]
    w_all = w.astype(BF16)
    w_main = jnp.concatenate([w[:, :4 * d_ret + d_moba], w[:, 4 * d_ret + 2 * d_moba:]], axis=1).astype(BF16)
    wkvT = w[:, 4 * d_ret + d_moba:].T.astype(BF16)
    g1 = norm1_g[layer][None, :]
    wo = w_out[layer].astype(BF16)
    rng = ret_norm_g[layer][None, :]
    n2 = norm2_g[layer][None, :]
    rwT = router_w[layer].T.astype(BF16)
    rb = router_b[layer][:, None]
    wgu = w_gate_up[layer]
    bgu_perm = b_gate_up[layer].reshape(n_experts, -1, LANES, 2).transpose(0, 1, 3, 2).reshape(n_experts, 1, 2 * d_ff)
    wd = w_down[layer]
    bd = b_down[layer][:, None, :]
    fg = final_norm_g[None, :]

    tm = _token_tile(lp)
    pos_p = jnp.arange(lp, dtype=jnp.int32)
    cos_p, sin_p = _rope_tables(pos_p, dk)
    xp = x_prompt.reshape(bp * lp, d_model)
    assert dk == dv
    ret, mq, mv, kT, vT, s_p = _inproj_prompt(
        xp, g1, w_main, wkvT, cos_p, sin_p, rng,
        batch=bp, seq=lp, d_ret=d_ret, d_moba=d_moba, n_ret_heads=rh, tm=tm)
    moba_o = _moba_prompt(mq, mv, kT, batch=bp, seq=lp, n_heads=mh, hd=hd)
    n_sample = bs * ls
    assert n_sample <= tm, "sample group must fit one token tile"
    npt = bp * lp // tm
    n_tiles = npt + 1
    bufs = _merge(xp, ret, moba_o, wo, n2, rwT, rb, None,
                  tm=tm, n_tiles_total=n_tiles, tile0=0, n_valid=tm, n_clear_tiles=1)

    pos_s = past_len + jnp.arange(ls, dtype=jnp.int32)
    cos_s, sin_s = _rope_tables(jnp.tile(pos_s, bs), dk)
    xs = x_sample.reshape(bs * ls, d_model)
    rq_s, rk_s, rv_s, rg_s, mq_s, mk_s, mv_s = _inproj_sample(
        xs, g1, w_all, cos_s, sin_s, d_ret=d_ret, d_moba=d_moba, n_ret_heads=rh)
    r3 = lambda a: a.reshape(bs, ls, a.shape[1])
    ret_s, s_s = _ret_sample(r3(rq_s), r3(rk_s), r3(rv_s), r3(rg_s), rng, state_ret[layer],
                             n_heads=rh, dk=dk, dv=dv)
    ckT = cache_k[layer].transpose(0, 2, 3, 1)
    cvT = cache_v[layer].transpose(0, 2, 3, 1)
    idx_pad = _moba_gate(page_table, r3(mq_s), ckT, n_heads=mh, hd=hd, ls=ls, pages_per_step=min(32, n_pages))
    idx = idx_pad[:, :, :MOBA_TOPK].reshape(-1)
    cols = lambda a: a.reshape(bs, ls, mh, hd).transpose(0, 2, 3, 1)
    o_cols = _moba_sample(page_table, idx, cols(mq_s), cols(mk_s), cols(mv_s), ckT, cvT,
                          n_heads=mh, hd=hd, ls=ls)
    moba_o_s = o_cols.transpose(0, 3, 1, 2).reshape(bs * ls, d_moba).astype(BF16)
    padt = lambda a: jnp.pad(a, ((0, tm - n_sample), (0, 0)))
    hres_all, xn_all, slot, wts, cnt = _merge(
        padt(xs), padt(ret_s.reshape(n_sample, d_ret).astype(BF16)), padt(moba_o_s), wo, n2, rwT, rb, bufs,
        tm=tm, n_tiles_total=n_tiles, tile0=npt, n_valid=n_sample)

    tme = 512
    n_slots = -(-(tm * TOP_K + n_experts * (SLOT_ALIGN - 1)) // 256) * 256
    n_assign = TOP_K * (bp * lp + n_sample)
    n_row_tiles = -(-(n_assign + (SLOT_ALIGN - 1) * n_experts * n_tiles + n_experts * (tme - 1)) // tme)
    cp, to, dr, ts, tn, te, tv, xi = _route_tables(cnt[:, :, 0], tme=tme, n_row_tiles=n_row_tiles)
    x_sorted = _dispatch(cp, to, dr, ts, tn, xn_all, slot, tm=tm, tme=tme, n_experts=n_experts, n_slots=n_slots,
                         p_rows=n_row_tiles * tme)
    y_sorted = _experts(te, tv, xi, x_sorted, wgu, bgu_perm, wd, bd, tme=tme)
    y_p, y_s = _combine(cp, to, dr, hres_all, slot.T, wts.T, fg, y_sorted, tm=tm, n_experts=n_experts,
                        n_slots=n_slots, n_prompt_tiles=npt, n_sample=n_sample)

    y_prompt = y_p.reshape(bp, lp, d_model)
    y_sample = y_s.reshape(bs, ls, d_model)
    k_prompt = kT.reshape(bp, mh, hd, lp).transpose(0, 3, 1, 2)[None]
    v_prompt = vT.reshape(bp, mh, hd, lp).transpose(0, 3, 1, 2)[None]
    k_sample = mk_s.reshape(1, bs, ls, mh, hd)
    v_sample = mv_s.reshape(1, bs, ls, mh, hd)
    return (y_prompt, y_sample, k_prompt, v_prompt, s_p[None], k_sample, v_sample, s_s[None])
```

```python
import functools

import jax
import jax.numpy as jnp
from jax import lax
from jax.experimental import pallas as pl
from jax.experimental.pallas import tpu as pltpu

F32 = jnp.float32
BF16 = jnp.bfloat16

EPS = 1e-5
ROPE_BASE = 10000.0
RET_CHUNK = 128
MOBA_BLOCK = 256
MOBA_TOPK = 3
TOP_K = 4
SWIGLU_LIMIT = 7.0
SWIGLU_ALPHA = 1.702
NEG = -1e30
LANES = 128
SLOT_ALIGN = 16
RUN_PIECE = 64
WAIT_PIECE = 256
VMEM_LIMIT = 56 * 1024 * 1024

_NT = (((1,), (1,)), ((), ()))


def _cparams(*sem):
    return pltpu.CompilerParams(dimension_semantics=sem, vmem_limit_bytes=VMEM_LIMIT)


def _token_tile(seq):
    for tm in (512, 256, 128):
        if seq % tm == 0:
            return tm
    raise ValueError(f"sequence length {seq} must be a multiple of 128")


def _rms(x, g):
    return x * lax.rsqrt(jnp.mean(x * x, axis=-1, keepdims=True) + EPS) * g


def _rope_tables(pos, dk):
    half = dk // 2
    inv = ROPE_BASE ** (-jnp.arange(half, dtype=F32) / half)
    ang = pos.astype(F32)[:, None] * inv[None, :]
    c = jnp.cos(ang)
    s = jnp.sin(ang)
    return jnp.concatenate([c, c], axis=1), jnp.concatenate([-s, s], axis=1)


def _rotary(z, cos2, sin2, n_heads, dk):
    outs = []
    for h in range(n_heads):
        s = z[:, h * dk:(h + 1) * dk]
        outs.append(s * cos2 + pltpu.roll(s, dk // 2, 1) * sin2)
    return jnp.concatenate(outs, axis=1)


def _head_norm_gate(o, g, rg):
    mu = jnp.mean(o, axis=-1, keepdims=True)
    var = jnp.mean(jnp.square(o - mu), axis=-1, keepdims=True)
    return (o - mu) * lax.rsqrt(var + EPS) * g * (rg * jax.nn.sigmoid(rg))


def _inproj_prompt_kernel(x_ref, g_ref, w_ref, wkvT_ref, cos_ref, sin_ref, rng_ref,
                          din_ref, qdec_ref, kdec_ref, cdec_ref,
                          ret_ref, mq_ref, mv_ref, kT_ref, vT_ref, s_out_ref, s_scr,
                          *, d_ret, d_moba, n_ret_heads, chunk):
    i = pl.program_id(1)
    dk = d_ret // n_ret_heads
    tm = x_ref.shape[0]
    xn = _rms(x_ref[...], g_ref[...]).astype(BF16)
    cos2 = cos_ref[...]
    sin2 = sin_ref[...]

    def proj(c0, n):
        return jnp.dot(xn, w_ref[:, c0:c0 + n], preferred_element_type=F32)

    mq_ref[...] = proj(4 * d_ret, d_moba).astype(BF16)
    mv_ref[...] = proj(4 * d_ret + 2 * d_moba, d_moba).astype(BF16)
    kT_ref[0] = lax.dot_general(wkvT_ref[0:d_moba, :], xn, _NT, preferred_element_type=F32)
    vT_ref[0] = lax.dot_general(wkvT_ref[d_moba:2 * d_moba, :], xn, _NT, preferred_element_type=F32)

    rq = _rotary(proj(0, d_ret), cos2, sin2, n_ret_heads, dk)
    rk = _rotary(proj(d_ret, d_ret), cos2, sin2, n_ret_heads, dk) * (dk ** -0.5)
    rv = proj(2 * d_ret, d_ret)
    rg = proj(3 * d_ret, d_ret)
    rng = rng_ref[...]

    @pl.when(i == 0)
    def _():
        s_scr[...] = jnp.zeros_like(s_scr)

    for h in range(n_ret_heads):
        hs = slice(h * dk, (h + 1) * dk)
        s = s_scr[h]
        for c in range(tm // chunk):
            rows = slice(c * chunk, (c + 1) * chunk)
            q = rq[rows, hs]
            k = rk[rows, hs]
            v = rv[rows, hs].astype(BF16)
            scores = lax.dot_general(q.astype(BF16), k.astype(BF16), _NT, preferred_element_type=F32) * din_ref[h]
            inner = jnp.dot(scores.astype(BF16), v, preferred_element_type=F32)
            cross = jnp.dot((q * qdec_ref[h]).astype(BF16), s.astype(BF16), preferred_element_type=F32)
            ret_ref[rows, hs] = _head_norm_gate(inner + cross, rng[:, hs], rg[rows, hs]).astype(BF16)
            kdT = (k * kdec_ref[h]).T.astype(BF16)
            s = cdec_ref[h] * s + jnp.dot(kdT, v, preferred_element_type=F32)
        s_scr[h] = s

    @pl.when(i == pl.num_programs(1) - 1)
    def _():
        s_out_ref[0] = s_scr[...]


def _inproj_prompt(x, g, w_main, wkvT, cos2, sin2, rng, *, batch, seq, d_ret, d_moba, n_ret_heads, tm):
    d_model = x.shape[1]
    nt = seq // tm
    dk = d_ret // n_ret_heads
    chunk = RET_CHUNK
    decay_in, q_dec, k_dec, chunk_dec = _ret_tables(n_ret_heads, chunk)
    qdec_b = jnp.broadcast_to(q_dec.T[:, :, None], (n_ret_heads, chunk, dk))
    kdec_b = jnp.broadcast_to(k_dec.T[:, :, None], (n_ret_heads, chunk, dk))
    cdec_b = jnp.broadcast_to(chunk_dec[:, None, None], (n_ret_heads, dk, dk))
    tok_spec = lambda n: pl.BlockSpec((tm, n), lambda b, i: (b * nt + i, 0))
    full = lambda a: pl.BlockSpec(a.shape, lambda b, i: (0,) * a.ndim)
    t = batch * seq
    out_shape = (
        [jax.ShapeDtypeStruct((t, d_ret), BF16)]
        + [jax.ShapeDtypeStruct((t, d_moba), BF16)] * 2
        + [jax.ShapeDtypeStruct((batch, d_moba, seq), F32)] * 2
        + [jax.ShapeDtypeStruct((batch, n_ret_heads, dk, dk), F32)]
    )
    out_specs = (
        [tok_spec(d_ret)] + [tok_spec(d_moba)] * 2
        + [pl.BlockSpec((1, d_moba, tm), lambda b, i: (b, 0, i))] * 2
        + [pl.BlockSpec((1, n_ret_heads, dk, dk), lambda b, i: (b, 0, 0, 0))]
    )
    return pl.pallas_call(
        functools.partial(_inproj_prompt_kernel, d_ret=d_ret, d_moba=d_moba, n_ret_heads=n_ret_heads, chunk=chunk),
        grid=(batch, nt),
        in_specs=[tok_spec(d_model), full(g), full(w_main), full(wkvT),
                  pl.BlockSpec((tm, dk), lambda b, i: (i, 0)), pl.BlockSpec((tm, dk), lambda b, i: (i, 0)),
                  full(rng), full(decay_in), full(qdec_b), full(kdec_b), full(cdec_b)],
        out_specs=out_specs,
        out_shape=out_shape,
        scratch_shapes=[pltpu.VMEM((n_ret_heads, dk, dk), F32)],
        compiler_params=_cparams("parallel", "arbitrary"),
        name="inproj_retention_prompt",
    )(x, g, w_main, wkvT, cos2, sin2, rng, decay_in, qdec_b, kdec_b, cdec_b)


def _inproj_sample_kernel(x_ref, g_ref, w_ref, cos_ref, sin_ref,
                          rq_ref, rk_ref, rv_ref, rg_ref, mq_ref, mk_ref, mv_ref,
                          *, d_ret, d_moba, n_ret_heads):
    dk = d_ret // n_ret_heads
    xn = _rms(x_ref[...], g_ref[...]).astype(BF16)
    cos2 = cos_ref[...]
    sin2 = sin_ref[...]

    def proj(c0, n):
        return jnp.dot(xn, w_ref[:, c0:c0 + n], preferred_element_type=F32)

    rq_ref[...] = _rotary(proj(0, d_ret), cos2, sin2, n_ret_heads, dk)
    rk_ref[...] = _rotary(proj(d_ret, d_ret), cos2, sin2, n_ret_heads, dk) * (dk ** -0.5)
    rv_ref[...] = proj(2 * d_ret, d_ret)
    rg_ref[...] = proj(3 * d_ret, d_ret)
    mq_ref[...] = proj(4 * d_ret, d_moba)
    mk_ref[...] = proj(4 * d_ret + d_moba, d_moba)
    mv_ref[...] = proj(4 * d_ret + 2 * d_moba, d_moba)


def _inproj_sample(x, g, w_all, cos2, sin2, *, d_ret, d_moba, n_ret_heads):
    t = x.shape[0]
    args = (x, g, w_all, cos2, sin2)
    full = lambda a: pl.BlockSpec(a.shape, lambda i: (0,) * a.ndim)
    out_shape = [jax.ShapeDtypeStruct((t, d_ret), F32)] * 4 + [jax.ShapeDtypeStruct((t, d_moba), F32)] * 3
    return pl.pallas_call(
        functools.partial(_inproj_sample_kernel, d_ret=d_ret, d_moba=d_moba, n_ret_heads=n_ret_heads),
        grid=(1,),
        in_specs=[full(a) for a in args],
        out_specs=[pl.BlockSpec(s.shape, lambda i: (0, 0)) for s in out_shape],
        out_shape=out_shape,
        compiler_params=_cparams("arbitrary"),
        name="inproj_sample",
    )(*args)


def _ret_tables(n_heads, chunk):
    log_g = jnp.log1p(-jnp.exp2(-5.0 - jnp.arange(n_heads, dtype=F32)))
    n = jnp.arange(chunk, dtype=F32)
    rel = n[:, None] - n[None, :]
    causal = rel >= 0
    decay_in = jnp.where(causal[None], jnp.exp(jnp.where(causal, rel, 0.0)[None] * log_g[:, None, None]), 0.0)
    q_dec = jnp.exp((n[:, None] + 1.0) * log_g[None, :])
    k_dec = jnp.exp((chunk - 1.0 - n[:, None]) * log_g[None, :])
    chunk_dec = jnp.exp(chunk * log_g)
    return decay_in, q_dec, k_dec, chunk_dec


def _ret_sample_kernel(rq_ref, rk_ref, rv_ref, rg_ref, rng_ref, s_ref, din_ref, qdec_ref, kdec_ref, cdec_ref,
                       o_ref, s_out_ref, *, n_heads, dk, dv):
    row = lax.broadcasted_iota(jnp.int32, (dk, dk), 0)
    col = lax.broadcasted_iota(jnp.int32, (dk, dk), 1)
    eye = (row == col).astype(BF16)
    rng = rng_ref[...]
    for h in range(n_heads):
        q = rq_ref[0, :, h * dk:(h + 1) * dk]
        k = rk_ref[0, :, h * dk:(h + 1) * dk]
        v = rv_ref[0, :, h * dv:(h + 1) * dv].astype(BF16)
        s = s_ref[0, h]
        scores = lax.dot_general(q.astype(BF16), k.astype(BF16), _NT, preferred_element_type=F32) * din_ref[h]
        inner = jnp.dot(scores.astype(BF16), v, preferred_element_type=F32)
        cross = jnp.dot((q * qdec_ref[h]).astype(BF16), s.astype(BF16), preferred_element_type=F32)
        o_ref[0, :, h * dv:(h + 1) * dv] = _head_norm_gate(
            inner + cross, rng[:, h * dv:(h + 1) * dv], rg_ref[0, :, h * dv:(h + 1) * dv])
        kd = (k * kdec_ref[h]).astype(BF16)
        kdT = lax.dot_general(eye, kd, _NT, preferred_element_type=F32).astype(BF16)
        s_out_ref[0, h] = cdec_ref[h] * s + jnp.dot(kdT, v, preferred_element_type=F32)


def _ret_sample(rq, rk, rv, rg, rng, state, *, n_heads, dk, dv):
    bs, ls, _ = rq.shape
    decay_in, q_dec, k_dec, chunk_dec = _ret_tables(n_heads, ls)
    qdec_b = jnp.broadcast_to(q_dec.T[:, :, None], (n_heads, ls, dk))
    kdec_b = jnp.broadcast_to(k_dec.T[:, :, None], (n_heads, ls, dk))
    cdec_b = jnp.broadcast_to(chunk_dec[:, None, None], (n_heads, dk, dv))
    tok = lambda n: pl.BlockSpec((1, ls, n), lambda b: (b, 0, 0))
    full = lambda a: pl.BlockSpec(a.shape, lambda b: (0,) * a.ndim)
    st = pl.BlockSpec((1, n_heads, dk, dv), lambda b: (b, 0, 0, 0))
    return pl.pallas_call(
        functools.partial(_ret_sample_kernel, n_heads=n_heads, dk=dk, dv=dv),
        grid=(bs,),
        in_specs=[tok(n_heads * dk), tok(n_heads * dk), tok(n_heads * dv), tok(n_heads * dv), full(rng), st,
                  full(decay_in), full(qdec_b), full(kdec_b), full(cdec_b)],
        out_specs=[tok(n_heads * dv), st],
        out_shape=[jax.ShapeDtypeStruct((bs, ls, n_heads * dv), F32),
                   jax.ShapeDtypeStruct((bs, n_heads, dk, dv), F32)],
        compiler_params=_cparams("parallel"),
        name="retention_sample",
    )(rq, rk, rv, rg, rng, state, decay_in, qdec_b, kdec_b, cdec_b)


def _stream_block_sums(pt_ref, ck_ref, ksum_ref, pbuf, acc_scr, sems, *, pps, ppb, gps, n_pages):
    step = (pl.program_id(0) * pl.num_programs(1) + pl.program_id(1)) * pl.num_programs(2) + pl.program_id(2)
    n_steps = pl.num_programs(0) * pl.num_programs(1) * pl.num_programs(2)
    cur = step % 2

    def issue(u, buf):
        base = (u // gps) * n_pages + (u % gps) * pps
        for i in range(pps):
            pltpu.make_async_copy(ck_ref.at[pt_ref[base + i]], pbuf.at[buf, i], sems.at[buf]).start()

    @pl.when(step == 0)
    def _():
        issue(step, cur)

    @pl.when(step + 1 < n_steps)
    def _():
        issue(step + 1, 1 - cur)

    pltpu.make_async_copy(ck_ref.at[pl.ds(0, pps)], pbuf.at[cur], sems.at[cur]).wait()

    group = step % gps
    d_moba, page_size = acc_scr.shape[0], pbuf.shape[-1]
    lane = lax.broadcasted_iota(jnp.int32, acc_scr.shape, 1)
    acc = jnp.where(group == 0, 0.0, acc_scr[...])
    for u in range(pps // ppb):
        ks = pbuf[cur, u * ppb]
        for pg in range(1, ppb):
            ks = ks + pbuf[cur, u * ppb + pg]
        tok_sum = jnp.sum(ks.reshape(d_moba, page_size), axis=1, keepdims=True)
        acc = jnp.where(lane == group * (pps // ppb) + u, tok_sum, acc)
    acc_scr[...] = acc

    @pl.when(group == gps - 1)
    def _():
        ksum_ref[0] = acc


def _moba_prompt_kernel(*refs, blk, hd, n_blocks, topk, stream):
    if stream:
        (pt_ref, q_ref, kTd_ref, vd_ref, kT_ref, v_ref, eneg_ref, cbias_ref, ck_ref,
         o_ref, ksum_ref, km_scr, kaug_scr, pbuf, acc_scr, sems) = refs
        _stream_block_sums(pt_ref, ck_ref, ksum_ref, pbuf, acc_scr, sems, **stream)
    else:
        q_ref, kTd_ref, vd_ref, kT_ref, v_ref, eneg_ref, cbias_ref, o_ref, km_scr, kaug_scr = refs
    c = pl.program_id(2)
    width = 2 * hd
    scale = hd ** -0.5

    @pl.when(c == 0)
    def _():
        lane = lax.broadcasted_iota(jnp.int32, (width, LANES), 1)
        km = jnp.zeros((width, LANES), F32)
        for n in range(n_blocks):
            s = jnp.sum(kT_ref[0, :, n * blk:(n + 1) * blk], axis=1, keepdims=True) * (1.0 / blk)
            km = jnp.where(lane == n, s, km)
        km_scr[...] = km.T
        kaug_scr[0:width, :] = kT_ref[0].astype(BF16)
        kaug_scr[width:width + LANES, :] = eneg_ref[...]

    q = q_ref[...]
    lane_q = lax.broadcasted_iota(jnp.int32, (blk, width), 1)
    zq = jnp.zeros_like(q)
    qst = jnp.concatenate([jnp.where(lane_q < hd, q, zq), jnp.where(lane_q >= hd, q, zq)], axis=0)
    kmbT = km_scr[...].astype(BF16)
    kd = kTd_ref[0].astype(BF16)
    vd = vd_ref[...]

    nbp = -(-n_blocks // 8) * 8
    row_b = lax.broadcasted_iota(jnp.int32, (nbp, 2 * blk), 0)
    gate = lax.dot_general(kmbT, qst, _NT, preferred_element_type=F32)[0:nbp, :]
    gate = jnp.where(row_b < c, gate, NEG)
    cnt = jnp.zeros((nbp, 2 * blk), jnp.int32)
    for m in range(n_blocks):
        gm = gate[m:m + 1, :]
        beats = (gm > gate) | ((gm == gate) & (m < row_b))
        cnt = cnt + beats.astype(jnp.int32)
    sel = (row_b < c) & (cnt < topk)
    penT = jnp.where(sel | (row_b >= n_blocks), 0.0, 1.0)
    penT = jnp.concatenate([penT, jnp.zeros((LANES - nbp, 2 * blk), F32)], axis=0)
    pen = penT.T.astype(BF16)
    qs = qst * scale
    qa = jnp.concatenate([qs, pen], axis=1)
    s_d = jnp.dot(qs, kd, preferred_element_type=F32) + cbias_ref[...]
    m_d = jnp.max(s_d, axis=1, keepdims=True)

    def finish(n_wide):
        outs = []
        for j in range(2):
            rows = slice(j * blk, (j + 1) * blk)
            s_dj, m_dj = s_d[rows], m_d[rows]
            if n_wide:
                w = n_wide * blk
                s_w = jnp.dot(qa[rows], kaug_scr[:, :w], preferred_element_type=F32)
                mx = jnp.maximum(m_dj, jnp.max(s_w, axis=1, keepdims=True))
                p_w = jnp.exp(s_w - mx)
                p_d = jnp.exp(s_dj - mx)
                den = jnp.sum(p_d, axis=1, keepdims=True) + jnp.sum(p_w, axis=1, keepdims=True)
                acc = (jnp.dot(p_d.astype(BF16), vd, preferred_element_type=F32)
                       + jnp.dot(p_w.astype(BF16), v_ref[0:w, :], preferred_element_type=F32))
            else:
                p_d = jnp.exp(s_dj - m_dj)
                den = jnp.sum(p_d, axis=1, keepdims=True)
                acc = jnp.dot(p_d.astype(BF16), vd, preferred_element_type=F32)
            outs.append(acc / den)
        o_ref[...] = jnp.where(lane_q < hd, outs[0], outs[1]).astype(BF16)

    widths = sorted({w for w in (2, 4) if w < n_blocks - 1} | ({n_blocks - 1} if n_blocks > 1 else set()))
    lo = 1
    for n_wide in widths:
        @pl.when((c >= lo) & (c <= n_wide))
        def _(n_wide=n_wide):
            finish(n_wide)
        lo = n_wide + 1

    @pl.when(c == 0)
    def _():
        finish(0)


def _stream_plan(n_steps, page_table, cacheT):
    bs, n_pages = page_table.shape
    ppb = MOBA_BLOCK // cacheT.shape[-1]
    total = bs * n_pages
    if total % n_steps:
        return None
    pps = total // n_steps
    if pps % ppb or n_pages % pps or cacheT.shape[0] < pps:
        return None
    return dict(pps=pps, ppb=ppb, gps=n_pages // pps, n_pages=n_pages)


def _moba_prompt(mq, mv, kT, *, batch, seq, n_heads, hd, page_table=None, cacheT=None):
    blk = MOBA_BLOCK
    nb = seq // blk
    width = 2 * hd
    npair = n_heads // 2
    assert nb <= LANES and width == LANES
    eneg = jnp.where(jnp.arange(LANES)[:, None] == (jnp.arange(seq) // blk)[None, :], NEG, 0.0).astype(BF16)
    cbias = jnp.where(jnp.arange(blk)[None, :] <= (jnp.arange(2 * blk) % blk)[:, None], 0.0, -jnp.inf).astype(F32)
    stream = None if cacheT is None else _stream_plan(batch * npair * nb, page_table, cacheT)
    in_specs = [
        pl.BlockSpec((blk, width), lambda b, hp, c, *_: (b * nb + c, hp)),
        pl.BlockSpec((1, width, blk), lambda b, hp, c, *_: (b, hp, c)),
        pl.BlockSpec((blk, width), lambda b, hp, c, *_: (b * nb + c, hp)),
        pl.BlockSpec((1, width, seq), lambda b, hp, c, *_: (b, hp, 0)),
        pl.BlockSpec((seq, width), lambda b, hp, c, *_: (b, hp)),
        pl.BlockSpec((LANES, seq), lambda b, hp, c, *_: (0, 0)),
        pl.BlockSpec((2 * blk, blk), lambda b, hp, c, *_: (0, 0)),
    ]
    out_specs = [pl.BlockSpec((blk, width), lambda b, hp, c, *_: (b * nb + c, hp))]
    out_shape = [jax.ShapeDtypeStruct((batch * seq, n_heads * hd), BF16)]
    scratch = [pltpu.VMEM((width, LANES), F32), pltpu.VMEM((width + LANES, seq), BF16)]
    args = [mq, kT, mv, kT, mv, eneg, cbias]
    if stream:
        bs = page_table.shape[0]
        mh_all, hd_all, page_size = cacheT.shape[1:]
        gps = stream["gps"]
        in_specs.append(pl.BlockSpec(memory_space=pl.ANY))
        args.append(cacheT)
        out_specs.append(pl.BlockSpec((1, mh_all * hd_all, LANES),
                                      lambda b, hp, c, *_: (((b * npair + hp) * nb + c) // gps, 0, 0)))
        out_shape.append(jax.ShapeDtypeStruct((bs, mh_all * hd_all, LANES), F32))
        scratch += [pltpu.VMEM((2, stream["pps"], mh_all, hd_all, page_size), F32),
                    pltpu.VMEM((mh_all * hd_all, LANES), F32), pltpu.SemaphoreType.DMA((2,))]
        args = [page_table.reshape(-1)] + args
    grid_spec = pltpu.PrefetchScalarGridSpec(
        num_scalar_prefetch=1 if stream else 0,
        grid=(batch, npair, nb),
        in_specs=in_specs,
        out_specs=out_specs,
        scratch_shapes=scratch,
    )
    outs = pl.pallas_call(
        functools.partial(_moba_prompt_kernel, blk=blk, hd=hd, n_blocks=nb, topk=MOBA_TOPK, stream=stream),
        grid_spec=grid_spec,
        out_shape=out_shape,
        compiler_params=_cparams("arbitrary", "arbitrary", "arbitrary"),
        name="moba_prompt",
    )(*args)
    return (outs[0], outs[1]) if stream else (outs[0], None)


def _moba_gate_kernel(pt_ref, q_ref, *refs, pages_per_step, pages_per_block, n_heads, hd, ls, n_past, topk):
    del pt_ref
    pages = refs[:pages_per_step]
    idx_ref = refs[pages_per_step]
    km_scr = refs[pages_per_step + 1]
    j = pl.program_id(1)
    nj = pl.num_programs(1)
    d_moba = n_heads * hd
    page_size = pages[0].shape[-1]
    blocks_per_step = pages_per_step // pages_per_block

    @pl.when(j == 0)
    def _():
        km_scr[...] = jnp.zeros_like(km_scr)

    col = lax.broadcasted_iota(jnp.int32, (page_size, LANES), 1)
    acc = km_scr[...]
    for u in range(blocks_per_step):
        ks = pages[u * pages_per_block][0]
        for pg in range(1, pages_per_block):
            ks = ks + pages[u * pages_per_block + pg][0]
        ks = ks.reshape(d_moba, page_size).astype(BF16)
        onecol = (col == j * blocks_per_step + u).astype(BF16)
        acc = acc + jnp.dot(ks, onecol, preferred_element_type=F32)
    km_scr[...] = acc

    @pl.when(j == nj - 1)
    def _():
        idx_ref[0] = _gate_topk(km_scr[...], q_ref[0], block_len=pages_per_block * page_size,
                                n_heads=n_heads, hd=hd, n_past=n_past, topk=topk)


def _gate_topk(ksum, q, *, block_len, n_heads, hd, n_past, topk):
    ls, d_moba = q.shape
    kmean = (ksum * (1.0 / block_len)).astype(BF16)
    rows = ls * n_heads
    qrep = jnp.concatenate([jnp.broadcast_to(q[i:i + 1, :], (n_heads, d_moba)) for i in range(ls)], axis=0)
    r = lax.broadcasted_iota(jnp.int32, (rows, d_moba), 0)
    cc = lax.broadcasted_iota(jnp.int32, (rows, d_moba), 1)
    qbd = jnp.where((cc // hd) == (r % n_heads), qrep, 0.0).astype(BF16)
    gate = jnp.dot(qbd, kmean, preferred_element_type=F32)
    lane = lax.broadcasted_iota(jnp.int32, (rows, LANES), 1)
    work = jnp.where(lane < n_past, gate, jnp.where(lane == n_past, NEG, -jnp.inf))
    out = jnp.zeros((rows, LANES), jnp.int32)
    for t in range(topk):
        mx = jnp.max(work, axis=1, keepdims=True)
        it = jnp.min(jnp.where(work == mx, lane, LANES), axis=1, keepdims=True)
        out = jnp.where(lane == t, it, out)
        work = jnp.where(lane == it, -jnp.inf, work)
    return out


def _moba_gate_top_kernel(ksum_ref, q_ref, idx_ref, **kw):
    idx_ref[0] = _gate_topk(ksum_ref[0], q_ref[0], **kw)


def _moba_gate_top(ksum, mq_s, *, n_heads, hd, n_past):
    bs, ls, d_moba = mq_s.shape
    return pl.pallas_call(
        functools.partial(_moba_gate_top_kernel, block_len=MOBA_BLOCK, n_heads=n_heads, hd=hd, n_past=n_past,
                          topk=MOBA_TOPK),
        grid=(bs,),
        in_specs=[pl.BlockSpec((1, d_moba, LANES), lambda b: (b, 0, 0)),
                  pl.BlockSpec((1, ls, d_moba), lambda b: (b, 0, 0))],
        out_specs=pl.BlockSpec((1, ls * n_heads, LANES), lambda b: (b, 0, 0)),
        out_shape=jax.ShapeDtypeStruct((bs, ls * n_heads, LANES), jnp.int32),
        compiler_params=_cparams("parallel"),
        name="moba_sample_gate_top",
    )(ksum, mq_s)


def _moba_gate(page_table, mq_s, cacheT, *, n_heads, hd, ls, pages_per_step):
    bs, n_pages = page_table.shape
    page_size = cacheT.shape[-1]
    ppb = MOBA_BLOCK // page_size
    n_past = n_pages // ppb
    steps = n_pages // pages_per_step
    d_moba = n_heads * hd

    def page_spec(i):
        return pl.BlockSpec((1, n_heads, hd, page_size),
                            lambda b, j, pt: (pt[b * n_pages + j * pages_per_step + i], 0, 0, 0))

    grid_spec = pltpu.PrefetchScalarGridSpec(
        num_scalar_prefetch=1,
        grid=(bs, steps),
        in_specs=[pl.BlockSpec((1, ls, d_moba), lambda b, j, pt: (b, 0, 0))]
        + [page_spec(i) for i in range(pages_per_step)],
        out_specs=pl.BlockSpec((1, ls * n_heads, LANES), lambda b, j, pt: (b, 0, 0)),
        scratch_shapes=[pltpu.VMEM((d_moba, LANES), F32)],
    )
    return pl.pallas_call(
        functools.partial(_moba_gate_kernel, pages_per_step=pages_per_step, pages_per_block=ppb,
                          n_heads=n_heads, hd=hd, ls=ls, n_past=n_past, topk=MOBA_TOPK),
        grid_spec=grid_spec,
        out_shape=jax.ShapeDtypeStruct((bs, ls * n_heads, LANES), jnp.int32),
        compiler_params=_cparams("parallel", "arbitrary"),
        name="moba_sample_gate",
    )(page_table.reshape(-1), mq_s, *([cacheT] * pages_per_step))


def _moba_sample_kernel(pt_ref, idx_ref, qc_ref, kc_ref, vc_ref, ck_ref, cv_ref, o_ref, kbuf, vbuf, sems,
                        *, ls, topk, ppb, n_heads, hd, n_past, n_pages):
    b = pl.program_id(0)
    nb = pl.num_programs(0)
    cur = b % 2
    per_head = ls * topk * ppb
    rows_per_head = per_head // n_heads
    scale = hd ** -0.5

    def slab_at(buf_ref, buf, h, r):
        return buf_ref.at[buf, h * rows_per_head + r // n_heads, r % n_heads]

    def issue(bb, buf):
        def per_head_body(h, c):
            for i in range(ls):
                for j in range(topk):
                    blk = jnp.minimum(idx_ref[((bb * ls + i) * n_heads + h) * topk + j], n_past - 1)
                    for pg in range(ppb):
                        page = pt_ref[bb * n_pages + blk * ppb + pg]
                        r = (i * topk + j) * ppb + pg
                        pltpu.make_async_copy(ck_ref.at[page, h], slab_at(kbuf, buf, h, r), sems.at[buf]).start()
                        pltpu.make_async_copy(cv_ref.at[page, h], slab_at(vbuf, buf, h, r), sems.at[buf]).start()
            return c

        lax.fori_loop(0, n_heads, per_head_body, 0)

    @pl.when(b == 0)
    def _():
        issue(b, cur)

    @pl.when(b + 1 < nb)
    def _():
        issue(b + 1, 1 - cur)

    pltpu.make_async_copy(ck_ref.at[pl.ds(0, per_head)], kbuf.at[cur], sems.at[cur]).wait()
    pltpu.make_async_copy(cv_ref.at[pl.ds(0, per_head)], vbuf.at[cur], sems.at[cur]).wait()

    lane_own = lax.broadcasted_iota(jnp.int32, (1, ls), 1)

    def head_body(h, c):
        qc = qc_ref[0, h]
        knew = kc_ref[0, h]
        vnew = vc_ref[0, h]
        for i in range(ls):
            qcol = qc[:, i:i + 1]
            s_list = []
            for j in range(topk):
                ok = idx_ref[((b * ls + i) * n_heads + h) * topk + j] < n_past
                for pg in range(ppb):
                    kt = slab_at(kbuf, cur, h, (i * topk + j) * ppb + pg)[...]
                    s = jnp.sum(kt * qcol, axis=0, keepdims=True) * scale
                    s_list.append(jnp.where(ok, s, -jnp.inf))
            s_own = jnp.sum(knew * qcol, axis=0, keepdims=True) * scale
            s_own = jnp.where(lane_own <= i, s_own, -jnp.inf)
            mx = jnp.max(s_own, axis=1, keepdims=True)
            for s in s_list:
                mx = jnp.maximum(mx, jnp.max(s, axis=1, keepdims=True))
            p_own = jnp.exp(s_own - mx)
            denom = jnp.sum(p_own, axis=1, keepdims=True)
            o = jnp.sum(vnew * p_own, axis=1, keepdims=True)
            pv = None
            for t, s in enumerate(s_list):
                p = jnp.exp(s - mx)
                denom = denom + jnp.sum(p, axis=1, keepdims=True)
                term = slab_at(vbuf, cur, h, i * topk * ppb + t)[...] * p
                pv = term if pv is None else pv + term
            o = o + jnp.sum(pv, axis=1, keepdims=True)
            o_ref[0, h, :, i:i + 1] = o / denom
        return c

    lax.fori_loop(0, n_heads, head_body, 0)


def _moba_sample(page_table, idx, q_cols, k_cols, v_cols, cacheT_k, cacheT_v, *, n_heads, hd, ls):
    bs, n_pages = page_table.shape
    page_size = cacheT_k.shape[-1]
    ppb = MOBA_BLOCK // page_size
    n_past = n_pages // ppb
    topk = MOBA_TOPK
    per_head = ls * topk * ppb
    assert per_head % n_heads == 0 and cacheT_k.shape[0] >= per_head
    col_spec = pl.BlockSpec((1, n_heads, hd, ls), lambda b, pt, ix: (b, 0, 0, 0))
    any_spec = pl.BlockSpec(memory_space=pl.ANY)
    slab_buf = pltpu.VMEM((2, per_head, n_heads, hd, page_size), F32)
    grid_spec = pltpu.PrefetchScalarGridSpec(
        num_scalar_prefetch=2,
        grid=(bs,),
        in_specs=[col_spec, col_spec, col_spec, any_spec, any_spec],
        out_specs=col_spec,
        scratch_shapes=[slab_buf, slab_buf, pltpu.SemaphoreType.DMA((2,))],
    )
    return pl.pallas_call(
        functools.partial(_moba_sample_kernel, ls=ls, topk=topk, ppb=ppb, n_heads=n_heads, hd=hd, n_past=n_past,
                          n_pages=n_pages),
        grid_spec=grid_spec,
        out_shape=jax.ShapeDtypeStruct((bs, n_heads, hd, ls), F32),
        compiler_params=_cparams("arbitrary"),
        name="moba_sample_attn",
    )(page_table.reshape(-1), idx, q_cols, k_cols, v_cols, cacheT_k, cacheT_v)


def _merge_kernel(*refs, n_in_tiles, **kw):
    i = pl.program_id(0)

    @pl.when(i < n_in_tiles)
    def _():
        _merge_body(*refs, **kw)

    @pl.when(i >= n_in_tiles)
    def _():
        hres_ref, xn_ref, slot_ref, w_ref, cnt_ref = refs[-5:]
        hres_ref[...] = jnp.zeros_like(hres_ref)
        xn_ref[...] = jnp.zeros_like(xn_ref)
        slot_ref[...] = jnp.full(slot_ref.shape, -1, jnp.int32)
        w_ref[...] = jnp.zeros_like(w_ref)
        cnt_ref[...] = jnp.zeros_like(cnt_ref)


def _merge_body(x_ref, ret_ref, mo_ref, wo_ref, n2_ref, rwT_ref, rb_ref, *refs, n_experts, top_k, n_valid):
    hres_ref, xn_ref, slot_ref, w_ref, cnt_ref = refs[-5:]
    tm = x_ref.shape[0]
    d_ret = ret_ref.shape[1]
    mix = (jnp.dot(ret_ref[...], wo_ref[0:d_ret, :], preferred_element_type=F32)
           + jnp.dot(mo_ref[...], wo_ref[d_ret:, :], preferred_element_type=F32))
    hres = x_ref[...] + mix
    hres_ref[...] = hres
    xn = _rms(hres, n2_ref[...]).astype(BF16)
    xn_ref[...] = xn
    logits = lax.dot_general(rwT_ref[...], xn, _NT, preferred_element_type=F32) + rb_ref[...]
    row = lax.broadcasted_iota(jnp.int32, logits.shape, 0)
    work = logits
    vals, hots = [], []
    for _ in range(top_k):
        mx = jnp.max(work, axis=0, keepdims=True)
        it = jnp.min(jnp.where(work == mx, row, n_experts), axis=0, keepdims=True)
        hot = row == it
        vals.append(mx)
        hots.append(hot)
        work = jnp.where(hot, -jnp.inf, work)
    exps = [jnp.exp(v - vals[0]) for v in vals]
    denom = exps[0]
    for e in exps[1:]:
        denom = denom + e

    valid = lax.broadcasted_iota(jnp.int32, (1, tm), 1) < n_valid
    mask = hots[0]
    for hot in hots[1:]:
        mask = mask | hot
    mask = mask & valid
    maskf = mask.astype(F32)
    r_i = lax.broadcasted_iota(jnp.int32, (tm, tm), 0)
    c_i = lax.broadcasted_iota(jnp.int32, (tm, tm), 1)
    rank = jnp.dot(maskf.astype(BF16), (r_i < c_i).astype(BF16), preferred_element_type=F32)
    cnt = jnp.sum(maskf, axis=1, keepdims=True).astype(jnp.int32)
    cnt_pad = ((cnt + (SLOT_ALIGN - 1)) // SLOT_ALIGN) * SLOT_ALIGN
    e_r = lax.broadcasted_iota(jnp.int32, (n_experts, n_experts), 0)
    e_c = lax.broadcasted_iota(jnp.int32, (n_experts, n_experts), 1)
    cpb = jnp.broadcast_to(cnt_pad.astype(F32), (n_experts, LANES)).astype(BF16)
    tile_off = jnp.dot((e_c < e_r).astype(BF16), cpb, preferred_element_type=F32)[:, 0:1]
    slot = tile_off + rank

    row8 = lax.broadcasted_iota(jnp.int32, (8, tm), 0)
    slot_out = jnp.full((8, tm), -1, jnp.int32)
    w_out = jnp.zeros((8, tm), F32)
    for k in range(top_k):
        sk = jnp.sum(jnp.where(hots[k], slot, 0.0), axis=0, keepdims=True).astype(jnp.int32)
        sk = jnp.where(valid, sk, -1)
        slot_out = jnp.where(row8 == k, sk, slot_out)
        w_out = jnp.where(row8 == k, exps[k] / denom, w_out)
    slot_ref[...] = slot_out
    w_ref[...] = w_out
    cnt_ref[0] = jnp.broadcast_to(cnt, (n_experts, LANES))


def _merge(x, ret, moba_o, w_out, norm2_g, rwT, router_b, prev, *,
           tm, n_tiles_total, tile0, n_valid, n_clear_tiles=0):
    t, d_model = x.shape
    n_experts = rwT.shape[0]
    t_pad = n_tiles_total * tm
    n_in = t // tm
    tok_in = lambda n: pl.BlockSpec((tm, n), lambda i: (jnp.minimum(i, n_in - 1), 0))
    tok = lambda n: pl.BlockSpec((tm, n), lambda i: (tile0 + i, 0))
    full = lambda a: pl.BlockSpec(a.shape, lambda i: (0,) * a.ndim)
    out_shape = [jax.ShapeDtypeStruct((t_pad, d_model), F32), jax.ShapeDtypeStruct((t_pad, d_model), BF16),
                 jax.ShapeDtypeStruct((8, t_pad), jnp.int32), jax.ShapeDtypeStruct((8, t_pad), F32),
                 jax.ShapeDtypeStruct((n_tiles_total, n_experts, LANES), jnp.int32)]
    out_specs = [tok(d_model), tok(d_model),
                 pl.BlockSpec((8, tm), lambda i: (0, tile0 + i)), pl.BlockSpec((8, tm), lambda i: (0, tile0 + i)),
                 pl.BlockSpec((1, n_experts, LANES), lambda i: (tile0 + i, 0, 0))]
    args = [x, ret, moba_o, w_out, norm2_g, rwT, router_b]
    in_specs = [tok_in(d_model), tok_in(ret.shape[1]), tok_in(moba_o.shape[1]),
                full(w_out), full(norm2_g), full(rwT), full(router_b)]
    aliases = {}
    if prev is not None:
        aliases = {len(args) + k: k for k in range(len(prev))}
        args += list(prev)
        in_specs += [pl.BlockSpec(memory_space=pl.ANY)] * len(prev)
    return pl.pallas_call(
        functools.partial(_merge_kernel, n_experts=n_experts, top_k=TOP_K, n_valid=n_valid, n_in_tiles=n_in),
        grid=(n_in + n_clear_tiles,),
        in_specs=in_specs,
        out_specs=out_specs,
        out_shape=out_shape,
        input_output_aliases=aliases,
        compiler_params=_cparams("parallel"),
        name="merge_router",
    )(*args)


def _run_copies(cp_ref, to_ref, dr_ref, t, n_experts, make_copy):
    def per_expert(e, carry):
        rows = cp_ref[t * n_experts + e]
        src0 = to_ref[t * n_experts + e]
        dst0 = dr_ref[t * n_experts + e]
        n_big = rows // RUN_PIECE
        done = n_big * RUN_PIECE

        def big(j, c):
            make_copy(pl.multiple_of(src0 + j * RUN_PIECE, SLOT_ALIGN),
                      pl.multiple_of(dst0 + j * RUN_PIECE, SLOT_ALIGN), RUN_PIECE).start()
            return c

        def small(j, c):
            make_copy(pl.multiple_of(src0 + done + j * SLOT_ALIGN, SLOT_ALIGN),
                      pl.multiple_of(dst0 + done + j * SLOT_ALIGN, SLOT_ALIGN), SLOT_ALIGN).start()
            return c

        lax.fori_loop(0, n_big, big, 0)
        lax.fori_loop(0, (rows - done) // SLOT_ALIGN, small, 0)
        return carry

    lax.fori_loop(0, n_experts, per_expert, 0)


def _run_rows(cp_ref, t, n_experts):
    return lax.fori_loop(0, n_experts, lambda e, c: c + cp_ref[t * n_experts + e], 0)


def _wait_rows(rows, make_copy):
    lax.fori_loop(0, rows // WAIT_PIECE, lambda j, c: (make_copy(0, 0, WAIT_PIECE).wait(), c)[1], 0)
    lax.fori_loop(0, (rows % WAIT_PIECE) // SLOT_ALIGN, lambda j, c: (make_copy(0, 0, SLOT_ALIGN).wait(), c)[1], 0)


def _dispatch_kernel(cp_ref, to_ref, dr_ref, ts_ref, tn_ref, xn_ref, slot_ref, xs_ref, slots_scr, zero_scr, sems,
                     *, n_experts, n_slots, chunk, top_k):
    t = pl.program_id(0)
    nt = pl.num_programs(0)
    cur = t % 2
    tm = xn_ref.shape[0]
    xn = xn_ref[...]
    sl = slot_ref[...]
    used = to_ref[t * n_experts + n_experts - 1] + cp_ref[t * n_experts + n_experts - 1]
    def sort_chunk(c):
        s_iota = c * chunk + lax.broadcasted_iota(jnp.int32, (chunk, tm), 0)
        pm = sl[0:1, :] == s_iota
        for k in range(1, top_k):
            pm = pm | (sl[k:k + 1, :] == s_iota)
        slots_scr[cur, c * chunk:(c + 1) * chunk, :] = jnp.dot(
            pm.astype(BF16), xn, preferred_element_type=F32).astype(BF16)

    for c in range(n_slots // chunk):
        if (c + 1) * chunk <= tm * top_k:
            sort_chunk(c)
        else:
            pl.when(c * chunk < used)(functools.partial(sort_chunk, c))

    def copy_from(buf):
        def make_copy(src, dst, rows):
            return pltpu.make_async_copy(slots_scr.at[buf, pl.ds(src, rows)], xs_ref.at[pl.ds(dst, rows)],
                                         sems.at[buf])
        return make_copy

    _run_copies(cp_ref, to_ref, dr_ref, t, n_experts, copy_from(cur))

    @pl.when(t > 0)
    def _():
        _wait_rows(_run_rows(cp_ref, t - 1, n_experts), copy_from(1 - cur))

    @pl.when(t == nt - 1)
    def _():
        zero_scr[...] = jnp.zeros_like(zero_scr)

        def zcopy(dst):
            return pltpu.make_async_copy(zero_scr.at[pl.ds(0, SLOT_ALIGN)], xs_ref.at[pl.ds(dst, SLOT_ALIGN)],
                                         sems.at[cur])

        def per_expert(e, tot):
            def one(j, c):
                zcopy(pl.multiple_of(ts_ref[e] + j * SLOT_ALIGN, SLOT_ALIGN)).start()
                return c
            lax.fori_loop(0, tn_ref[e], one, 0)
            return tot + tn_ref[e]

        n_tail = lax.fori_loop(0, n_experts, per_expert, 0)
        _wait_rows(_run_rows(cp_ref, t, n_experts) + n_tail * SLOT_ALIGN, copy_from(cur))

        tme = zero_scr.shape[0]

        def ztile(j):
            return pltpu.make_async_copy(zero_scr, xs_ref.at[pl.ds(pl.multiple_of(j * tme, tme), tme)], sems.at[cur])

        n_used = tn_ref[n_experts]
        n_all = xs_ref.shape[0] // tme
        lax.fori_loop(n_used, n_all, lambda j, c: (ztile(j).start(), c)[1], 0)
        lax.fori_loop(n_used, n_all, lambda j, c: (ztile(j).wait(), c)[1], 0)


def _dispatch(cp, to, dr, ts, tn, xn_all, slot, *, tm, tme, n_experts, n_slots, p_rows):
    t_pad, d_model = xn_all.shape
    grid_spec = pltpu.PrefetchScalarGridSpec(
        num_scalar_prefetch=5,
        grid=(t_pad // tm,),
        in_specs=[pl.BlockSpec((tm, d_model), lambda i, *_: (i, 0)),
                  pl.BlockSpec((8, tm), lambda i, *_: (0, i))],
        out_specs=pl.BlockSpec(memory_space=pl.ANY),
        scratch_shapes=[pltpu.VMEM((2, n_slots, d_model), BF16), pltpu.VMEM((tme, d_model), BF16),
                        pltpu.SemaphoreType.DMA((2,))],
    )
    return pl.pallas_call(
        functools.partial(_dispatch_kernel, n_experts=n_experts, n_slots=n_slots, chunk=256, top_k=TOP_K),
        grid_spec=grid_spec,
        out_shape=jax.ShapeDtypeStruct((p_rows, d_model), BF16),
        compiler_params=_cparams("arbitrary"),
        name="moe_dispatch",
    )(cp, to, dr, ts, tn, xn_all, slot)


def _expert_kernel(te_ref, tv_ref, xi_ref, x_ref, wgu_ref, bgu_ref, wd_ref, bd_ref, y_ref, wp_scr, wdb_scr, *, d_ff):
    j = pl.program_id(0)
    grp = 2 * LANES
    n_grp = 2 * d_ff // grp

    @pl.when(tv_ref[j] > 0)
    def _():
        first = (j == 0) | (te_ref[j] != te_ref[jnp.maximum(j - 1, 0)])

        @pl.when(first)
        def _():
            r = lax.broadcasted_iota(jnp.int32, (grp, grp), 0)
            c = lax.broadcasted_iota(jnp.int32, (grp, grp), 1)
            perm = (((c < LANES) & (r == 2 * c)) | ((c >= LANES) & (r == 2 * (c - LANES) + 1))).astype(BF16)
            for g in range(n_grp):
                wp_scr[:, g * grp:(g + 1) * grp] = jnp.dot(
                    wgu_ref[0, :, g * grp:(g + 1) * grp].astype(BF16), perm, preferred_element_type=F32).astype(BF16)
            wdb_scr[...] = wd_ref[0].astype(BF16)

        def ffn(rows):
            x = x_ref[0:rows, :]
            parts = []
            for g in range(n_grp):
                u = (jnp.dot(x, wp_scr[:, g * grp:(g + 1) * grp], preferred_element_type=F32)
                     + bgu_ref[0, :, g * grp:(g + 1) * grp])
                glu = jnp.minimum(u[:, :LANES], SWIGLU_LIMIT)
                lin = jnp.clip(u[:, LANES:], -SWIGLU_LIMIT, SWIGLU_LIMIT)
                parts.append((glu * jax.nn.sigmoid(SWIGLU_ALPHA * glu) * (lin + 1.0)).astype(BF16))
            a = jnp.concatenate(parts, axis=1)
            y_ref[0:rows, :] = (jnp.dot(a, wdb_scr[...], preferred_element_type=F32) + bd_ref[0]).astype(BF16)

        tme = x_ref.shape[0]

        @pl.when(tv_ref[j] > tme // 2)
        def _():
            ffn(tme)

        @pl.when(tv_ref[j] <= tme // 2)
        def _():
            ffn(tme // 2)
            y_ref[tme // 2:, :] = jnp.zeros((tme - tme // 2, y_ref.shape[1]), y_ref.dtype)

    @pl.when(tv_ref[j] == 0)
    def _():
        y_ref[...] = jnp.zeros_like(y_ref)


def _experts(te, tv, xi, x_sorted, wgu, bgu_perm, wd, bd, *, tme):
    p_rows, d_model = x_sorted.shape
    n_experts, _, d_ff2 = wgu.shape
    d_ff = d_ff2 // 2
    ex = lambda a: pl.BlockSpec((1,) + a.shape[1:], lambda j, te_, tv_, xi_: (te_[j],) + (0,) * (a.ndim - 1))
    grid_spec = pltpu.PrefetchScalarGridSpec(
        num_scalar_prefetch=3,
        grid=(p_rows // tme,),
        in_specs=[pl.BlockSpec((tme, d_model), lambda j, te_, tv_, xi_: (xi_[j], 0)),
                  ex(wgu), ex(bgu_perm), ex(wd), ex(bd)],
        out_specs=pl.BlockSpec((tme, d_model), lambda j, *_: (j, 0)),
        scratch_shapes=[pltpu.VMEM((d_model, d_ff2), BF16), pltpu.VMEM((d_ff, d_model), BF16)],
    )
    return pl.pallas_call(
        functools.partial(_expert_kernel, d_ff=d_ff),
        grid_spec=grid_spec,
        out_shape=jax.ShapeDtypeStruct((p_rows, d_model), BF16),
        compiler_params=_cparams("arbitrary"),
        name="moe_experts",
    )(te, tv, xi, x_sorted, wgu, bgu_perm, wd, bd)


def _combine_kernel(cp_ref, to_ref, dr_ref, hres_ref, slotT_ref, wT_ref, fg_ref, ys_ref, yp_ref, ysm_ref,
                    slots_scr, sems, *, n_experts, n_slots, chunk, top_k, n_prompt_tiles):
    t = pl.program_id(0)
    nt = pl.num_programs(0)
    cur = t % 2
    tm, d_model = hres_ref.shape

    def copy_into(buf):
        def make_copy(src, dst, rows):
            return pltpu.make_async_copy(ys_ref.at[pl.ds(dst, rows)], slots_scr.at[buf, pl.ds(src, rows)],
                                         sems.at[buf])
        return make_copy

    def fetch(tile, buf):
        _run_copies(cp_ref, to_ref, dr_ref, tile, n_experts, copy_into(buf))
        used = to_ref[tile * n_experts + n_experts - 1] + cp_ref[tile * n_experts + n_experts - 1]

        def zero_one(j, c):
            slots_scr[buf, pl.ds(pl.multiple_of(used + j * SLOT_ALIGN, SLOT_ALIGN), SLOT_ALIGN), :] = jnp.zeros(
                (SLOT_ALIGN, d_model), BF16)
            return c

        lax.fori_loop(0, (n_slots - used) // SLOT_ALIGN, zero_one, 0)

    @pl.when(t == 0)
    def _():
        fetch(t, cur)

    @pl.when(t + 1 < nt)
    def _():
        fetch(t + 1, 1 - cur)

    _wait_rows(_run_rows(cp_ref, t, n_experts), copy_into(cur))

    sl = slotT_ref[...]
    wt = wT_ref[...]
    acc = hres_ref[...]
    for c in range(n_slots // chunk):
        s_iota = c * chunk + lax.broadcasted_iota(jnp.int32, (tm, chunk), 1)
        pw = jnp.where(sl[:, 0:1] == s_iota, wt[:, 0:1], 0.0)
        for k in range(1, top_k):
            pw = pw + jnp.where(sl[:, k:k + 1] == s_iota, wt[:, k:k + 1], 0.0)
        acc = acc + jnp.dot(pw.astype(BF16), slots_scr[cur, c * chunk:(c + 1) * chunk, :],
                            preferred_element_type=F32)
    y = _rms(acc, fg_ref[...])

    @pl.when(t < n_prompt_tiles)
    def _():
        yp_ref[...] = y

    @pl.when(t >= n_prompt_tiles)
    def _():
        ysm_ref[...] = y[:ysm_ref.shape[0], :]


def _combine(cp, to, dr, hres_all, slotT, wT, fg, y_sorted, *, tm, n_experts, n_slots, n_prompt_tiles, n_sample):
    t_pad, d_model = hres_all.shape
    npt = n_prompt_tiles
    grid_spec = pltpu.PrefetchScalarGridSpec(
        num_scalar_prefetch=3,
        grid=(t_pad // tm,),
        in_specs=[pl.BlockSpec((tm, d_model), lambda i, *_: (i, 0)),
                  pl.BlockSpec((tm, 8), lambda i, *_: (i, 0)),
                  pl.BlockSpec((tm, 8), lambda i, *_: (i, 0)),
                  pl.BlockSpec(fg.shape, lambda i, *_: (0, 0)),
                  pl.BlockSpec(memory_space=pl.ANY)],
        out_specs=[pl.BlockSpec((tm, d_model), lambda i, *_: (jnp.minimum(i, npt - 1), 0)),
                   pl.BlockSpec((n_sample, d_model), lambda i, *_: (0, 0))],
        scratch_shapes=[pltpu.VMEM((2, n_slots, d_model), BF16), pltpu.SemaphoreType.DMA((2,))],
    )
    return pl.pallas_call(
        functools.partial(_combine_kernel, n_experts=n_experts, n_slots=n_slots, chunk=256, top_k=TOP_K,
                          n_prompt_tiles=npt),
        grid_spec=grid_spec,
        out_shape=[jax.ShapeDtypeStruct((npt * tm, d_model), F32), jax.ShapeDtypeStruct((n_sample, d_model), F32)],
        compiler_params=_cparams("arbitrary"),
        name="moe_combine",
    )(cp, to, dr, hres_all, slotT, wT, fg, y_sorted)


def _route_tables(cnt, *, tme, n_row_tiles):
    n_tiles, n_experts = cnt.shape
    cnt_pad = (cnt + (SLOT_ALIGN - 1)) // SLOT_ALIGN * SLOT_ALIGN
    tile_off = jnp.cumsum(cnt_pad, axis=1) - cnt_pad
    tot = jnp.sum(cnt_pad, axis=0)
    tot_t = (tot + (tme - 1)) // tme * tme
    ends = jnp.cumsum(tot_t)
    base = ends - tot_t
    dst_row = base[None, :] + jnp.cumsum(cnt_pad, axis=0) - cnt_pad
    j = jnp.arange(n_row_tiles, dtype=jnp.int32)
    te = jnp.minimum(jnp.sum(((ends // tme)[None, :] <= j[:, None]).astype(jnp.int32), axis=1), n_experts - 1)
    n_used = ends[-1] // tme
    run_end = jnp.sum(jnp.where(te[:, None] == jnp.arange(n_experts)[None, :], (base + tot)[None, :], 0), axis=1)
    tv = jnp.where(j < n_used, jnp.clip(run_end - j * tme, 0, tme), 0)
    xi = jnp.minimum(j, n_used - 1)
    tail_start = base + tot
    tail_n = jnp.concatenate([(tot_t - tot) // SLOT_ALIGN, n_used[None]])
    flat = lambda a: a.reshape(-1).astype(jnp.int32)
    return (flat(cnt_pad), flat(tile_off), flat(dst_row), flat(tail_start), flat(tail_n),
            flat(te), flat(tv), flat(xi))


def kernel(x_prompt, x_sample, cache_k, cache_v, state_ret, page_table, norm1_g, w_in, ret_norm_g, w_out,
           norm2_g, router_w, router_b, w_gate_up, b_gate_up, w_down, b_down, final_norm_g):
    bp, lp, d_model = x_prompt.shape
    bs, ls, _ = x_sample.shape
    depth = w_in.shape[0]
    assert depth == 1, "single-layer step"
    n_pages = page_table.shape[1]
    page_size, mh, hd = cache_k.shape[2], cache_k.shape[3], cache_k.shape[4]
    rh, dk, dv = state_ret.shape[2], state_ret.shape[3], state_ret.shape[4]
    d_ret = rh * dk
    d_moba = mh * hd
    n_experts = router_w.shape[2]
    d_ff = w_down.shape[2]
    past_len = n_pages * page_size
    layer = 0

    w = w_in[layer]
    w_all = w.astype(BF16)
    wkvT = w_all[:, 4 * d_ret + d_moba:].T
    g1 = norm1_g[layer][None, :]
    wo = w_out[layer].astype(BF16)
    rng = ret_norm_g[layer][None, :]
    n2 = norm2_g[layer][None, :]
    rwT = router_w[layer].T.astype(BF16)
    rb = router_b[layer][:, None]
    wgu = w_gate_up[layer]
    bgu_perm = b_gate_up[layer].reshape(n_experts, -1, LANES, 2).transpose(0, 1, 3, 2).reshape(n_experts, 1, 2 * d_ff)
    wd = w_down[layer]
    bd = b_down[layer][:, None, :]
    fg = final_norm_g[None, :]

    tm = _token_tile(lp)
    pos_p = jnp.arange(lp, dtype=jnp.int32)
    cos_p, sin_p = _rope_tables(pos_p, dk)
    xp = x_prompt.reshape(bp * lp, d_model)
    assert dk == dv
    ret, mq, mv, kT, vT, s_p = _inproj_prompt(
        xp, g1, w_all, wkvT, cos_p, sin_p, rng,
        batch=bp, seq=lp, d_ret=d_ret, d_moba=d_moba, n_ret_heads=rh, tm=tm)
    ckT = cache_k[layer].transpose(0, 2, 3, 1)
    cvT = cache_v[layer].transpose(0, 2, 3, 1)
    moba_o, ksum = _moba_prompt(mq, mv, kT, batch=bp, seq=lp, n_heads=mh, hd=hd, page_table=page_table, cacheT=ckT)
    n_sample = bs * ls
    assert n_sample <= tm, "sample group must fit one token tile"
    npt = bp * lp // tm
    n_tiles = npt + 1
    bufs = _merge(xp, ret, moba_o, wo, n2, rwT, rb, None,
                  tm=tm, n_tiles_total=n_tiles, tile0=0, n_valid=tm, n_clear_tiles=1)

    pos_s = past_len + jnp.arange(ls, dtype=jnp.int32)
    cos_s, sin_s = _rope_tables(jnp.tile(pos_s, bs), dk)
    xs = x_sample.reshape(bs * ls, d_model)
    rq_s, rk_s, rv_s, rg_s, mq_s, mk_s, mv_s = _inproj_sample(
        xs, g1, w_all, cos_s, sin_s, d_ret=d_ret, d_moba=d_moba, n_ret_heads=rh)
    r3 = lambda a: a.reshape(bs, ls, a.shape[1])
    ret_s, s_s = _ret_sample(r3(rq_s), r3(rk_s), r3(rv_s), r3(rg_s), rng, state_ret[layer],
                             n_heads=rh, dk=dk, dv=dv)
    if ksum is None:
        idx_pad = _moba_gate(page_table, r3(mq_s), ckT, n_heads=mh, hd=hd, ls=ls, pages_per_step=min(32, n_pages))
    else:
        idx_pad = _moba_gate_top(ksum, r3(mq_s), n_heads=mh, hd=hd, n_past=n_pages // (MOBA_BLOCK // page_size))
    idx = idx_pad[:, :, :MOBA_TOPK].reshape(-1)
    cols = lambda a: a.reshape(bs, ls, mh, hd).transpose(0, 2, 3, 1)
    o_cols = _moba_sample(page_table, idx, cols(mq_s), cols(mk_s), cols(mv_s), ckT, cvT,
                          n_heads=mh, hd=hd, ls=ls)
    moba_o_s = o_cols.transpose(0, 3, 1, 2).reshape(bs * ls, d_moba).astype(BF16)
    padt = lambda a: jnp.pad(a, ((0, tm - n_sample), (0, 0)))
    hres_all, xn_all, slot, wts, cnt = _merge(
        padt(xs), padt(ret_s.reshape(n_sample, d_ret).astype(BF16)), padt(moba_o_s), wo, n2, rwT, rb, bufs,
        tm=tm, n_tiles_total=n_tiles, tile0=npt, n_valid=n_sample)

    tme = 512
    n_slots = -(-(tm * TOP_K + n_experts * (SLOT_ALIGN - 1)) // 256) * 256
    n_assign = TOP_K * (bp * lp + n_sample)
    n_row_tiles = -(-(n_assign + (SLOT_ALIGN - 1) * n_experts * n_tiles + n_experts * (tme - 1)) // tme)
    cp, to, dr, ts, tn, te, tv, xi = _route_tables(cnt[:, :, 0], tme=tme, n_row_tiles=n_row_tiles)
    x_sorted = _dispatch(cp, to, dr, ts, tn, xn_all, slot, tm=tm, tme=tme, n_experts=n_experts, n_slots=n_slots,
                         p_rows=n_row_tiles * tme)
    y_sorted = _experts(te, tv, xi, x_sorted, wgu, bgu_perm, wd, bd, tme=tme)
    y_p, y_s = _combine(cp, to, dr, hres_all, slot.T, wts.T, fg, y_sorted, tm=tm, n_experts=n_experts,
                        n_slots=n_slots, n_prompt_tiles=npt, n_sample=n_sample)

    y_prompt = y_p.reshape(bp, lp, d_model)
    y_sample = y_s.reshape(bs, ls, d_model)
    k_prompt = kT.reshape(bp, mh, hd, lp).transpose(0, 3, 1, 2)[None]
    v_prompt = vT.reshape(bp, mh, hd, lp).transpose(0, 3, 1, 2)[None]
    k_sample = mk_s.reshape(1, bs, ls, mh, hd)
    v_sample = mv_s.reshape(1, bs, ls, mh, hd)
    return (y_prompt, y_sample, k_prompt, v_prompt, s_p[None], k_sample, v_sample, s_s[None])
```

```python
import functools

import jax
import jax.numpy as jnp
from jax import lax
from jax.experimental import pallas as pl
from jax.experimental.pallas import tpu as pltpu

F32 = jnp.float32
BF16 = jnp.bfloat16

EPS = 1e-5
ROPE_BASE = 10000.0
RET_CHUNK = 128
MOBA_BLOCK = 256
MOBA_TOPK = 3
TOP_K = 4
SWIGLU_LIMIT = 7.0
SWIGLU_ALPHA = 1.702
NEG = -1e30
LANES = 128
SLOT_ALIGN = 16
RUN_PIECE = 64
WAIT_PIECE = 256
VMEM_LIMIT = 56 * 1024 * 1024

_NT = (((1,), (1,)), ((), ()))


def _cparams(*sem):
    return pltpu.CompilerParams(dimension_semantics=sem, vmem_limit_bytes=VMEM_LIMIT)


def _token_tile(seq):
    for tm in (512, 256, 128):
        if seq % tm == 0:
            return tm
    raise ValueError(f"sequence length {seq} must be a multiple of 128")


def _rms(x, g):
    return x * lax.rsqrt(jnp.mean(x * x, axis=-1, keepdims=True) + EPS) * g


def _rope_tables(pos, dk):
    half = dk // 2
    inv = ROPE_BASE ** (-jnp.arange(half, dtype=F32) / half)
    ang = pos.astype(F32)[:, None] * inv[None, :]
    c = jnp.cos(ang)
    s = jnp.sin(ang)
    return jnp.concatenate([c, c], axis=1), jnp.concatenate([-s, s], axis=1)


def _rotary(z, cos2, sin2, n_heads, dk):
    outs = []
    for h in range(n_heads):
        s = z[:, h * dk:(h + 1) * dk]
        outs.append(s * cos2 + pltpu.roll(s, dk // 2, 1) * sin2)
    return jnp.concatenate(outs, axis=1)


def _head_norm_gate(o, g, rg):
    mu = jnp.mean(o, axis=-1, keepdims=True)
    var = jnp.mean(jnp.square(o - mu), axis=-1, keepdims=True)
    return (o - mu) * lax.rsqrt(var + EPS) * g * (rg * jax.nn.sigmoid(rg))


def _inproj_prompt_kernel(x_ref, g_ref, w_ref, cos_ref, sin_ref, rng_ref,
                          din_ref, qdec_ref, kdec_ref, cdec_ref,
                          ret_ref, mq_ref, mv_ref, kT_ref, vT_ref, s_out_ref, s_scr, wkvT_ref,
                          *, d_ret, d_moba, n_ret_heads, chunk):
    i = pl.program_id(1)
    dk = d_ret // n_ret_heads
    tm = x_ref.shape[0]
    xn = _rms(x_ref[...], g_ref[...]).astype(BF16)
    cos2 = cos_ref[...]
    sin2 = sin_ref[...]

    @pl.when((pl.program_id(0) == 0) & (i == 0))
    def _():
        wkvT_ref[...] = w_ref[:, 4 * d_ret + d_moba:].astype(F32).T.astype(BF16)

    def proj(c0, n):
        return jnp.dot(xn, w_ref[:, c0:c0 + n], preferred_element_type=F32)

    mq_ref[...] = proj(4 * d_ret, d_moba).astype(BF16)
    mv_ref[...] = proj(4 * d_ret + 2 * d_moba, d_moba).astype(BF16)
    kT_ref[0] = lax.dot_general(wkvT_ref[0:d_moba, :], xn, _NT, preferred_element_type=F32)
    vT_ref[0] = lax.dot_general(wkvT_ref[d_moba:2 * d_moba, :], xn, _NT, preferred_element_type=F32)

    rq = _rotary(proj(0, d_ret), cos2, sin2, n_ret_heads, dk)
    rk = _rotary(proj(d_ret, d_ret), cos2, sin2, n_ret_heads, dk) * (dk ** -0.5)
    rv = proj(2 * d_ret, d_ret)
    rg = proj(3 * d_ret, d_ret)
    rng = rng_ref[...]

    @pl.when(i == 0)
    def _():
        s_scr[...] = jnp.zeros_like(s_scr)

    for h in range(n_ret_heads):
        hs = slice(h * dk, (h + 1) * dk)
        s = s_scr[h]
        for c in range(tm // chunk):
            rows = slice(c * chunk, (c + 1) * chunk)
            q = rq[rows, hs]
            k = rk[rows, hs]
            v = rv[rows, hs].astype(BF16)
            scores = lax.dot_general(q.astype(BF16), k.astype(BF16), _NT, preferred_element_type=F32) * din_ref[h]
            inner = jnp.dot(scores.astype(BF16), v, preferred_element_type=F32)
            cross = jnp.dot((q * qdec_ref[h]).astype(BF16), s.astype(BF16), preferred_element_type=F32)
            ret_ref[rows, hs] = _head_norm_gate(inner + cross, rng[:, hs], rg[rows, hs]).astype(BF16)
            kdT = (k * kdec_ref[h]).T.astype(BF16)
            s = cdec_ref[h] * s + jnp.dot(kdT, v, preferred_element_type=F32)
        s_scr[h] = s

    @pl.when(i == pl.num_programs(1) - 1)
    def _():
        s_out_ref[0] = s_scr[...]


def _inproj_prompt(x, g, w_all, cos2, sin2, rng, *, batch, seq, d_ret, d_moba, n_ret_heads, tm):
    d_model = x.shape[1]
    nt = seq // tm
    dk = d_ret // n_ret_heads
    chunk = RET_CHUNK
    decay_in, q_dec, k_dec, chunk_dec = _ret_tables(n_ret_heads, chunk)
    qdec_b = jnp.broadcast_to(q_dec.T[:, :, None], (n_ret_heads, chunk, dk))
    kdec_b = jnp.broadcast_to(k_dec.T[:, :, None], (n_ret_heads, chunk, dk))
    cdec_b = jnp.broadcast_to(chunk_dec[:, None, None], (n_ret_heads, dk, dk))
    tok_spec = lambda n: pl.BlockSpec((tm, n), lambda b, i: (b * nt + i, 0))
    full = lambda a: pl.BlockSpec(a.shape, lambda b, i: (0,) * a.ndim)
    t = batch * seq
    out_shape = (
        [jax.ShapeDtypeStruct((t, d_ret), BF16)]
        + [jax.ShapeDtypeStruct((t, d_moba), BF16)] * 2
        + [jax.ShapeDtypeStruct((batch, d_moba, seq), F32)] * 2
        + [jax.ShapeDtypeStruct((batch, n_ret_heads, dk, dk), F32)]
    )
    out_specs = (
        [tok_spec(d_ret)] + [tok_spec(d_moba)] * 2
        + [pl.BlockSpec((1, d_moba, tm), lambda b, i: (b, 0, i))] * 2
        + [pl.BlockSpec((1, n_ret_heads, dk, dk), lambda b, i: (b, 0, 0, 0))]
    )
    return pl.pallas_call(
        functools.partial(_inproj_prompt_kernel, d_ret=d_ret, d_moba=d_moba, n_ret_heads=n_ret_heads, chunk=chunk),
        grid=(batch, nt),
        in_specs=[tok_spec(d_model), full(g), full(w_all),
                  pl.BlockSpec((tm, dk), lambda b, i: (i, 0)), pl.BlockSpec((tm, dk), lambda b, i: (i, 0)),
                  full(rng), full(decay_in), full(qdec_b), full(kdec_b), full(cdec_b)],
        out_specs=out_specs,
        out_shape=out_shape,
        scratch_shapes=[pltpu.VMEM((n_ret_heads, dk, dk), F32), pltpu.VMEM((2 * d_moba, d_model), BF16)],
        compiler_params=_cparams("arbitrary", "arbitrary"),
        name="inproj_retention_prompt",
    )(x, g, w_all, cos2, sin2, rng, decay_in, qdec_b, kdec_b, cdec_b)


def _inproj_sample_kernel(x_ref, g_ref, w_ref, cos_ref, sin_ref,
                          rq_ref, rk_ref, rv_ref, rg_ref, mq_ref, mk_ref, mv_ref,
                          *, d_ret, d_moba, n_ret_heads):
    dk = d_ret // n_ret_heads
    xn = _rms(x_ref[...], g_ref[...]).astype(BF16)
    cos2 = cos_ref[...]
    sin2 = sin_ref[...]

    def proj(c0, n):
        return jnp.dot(xn, w_ref[:, c0:c0 + n], preferred_element_type=F32)

    rq_ref[...] = _rotary(proj(0, d_ret), cos2, sin2, n_ret_heads, dk)
    rk_ref[...] = _rotary(proj(d_ret, d_ret), cos2, sin2, n_ret_heads, dk) * (dk ** -0.5)
    rv_ref[...] = proj(2 * d_ret, d_ret)
    rg_ref[...] = proj(3 * d_ret, d_ret)
    mq_ref[...] = proj(4 * d_ret, d_moba)
    mk_ref[...] = proj(4 * d_ret + d_moba, d_moba)
    mv_ref[...] = proj(4 * d_ret + 2 * d_moba, d_moba)


def _inproj_sample(x, g, w_all, cos2, sin2, *, d_ret, d_moba, n_ret_heads):
    t = x.shape[0]
    args = (x, g, w_all, cos2, sin2)
    full = lambda a: pl.BlockSpec(a.shape, lambda i: (0,) * a.ndim)
    out_shape = [jax.ShapeDtypeStruct((t, d_ret), F32)] * 4 + [jax.ShapeDtypeStruct((t, d_moba), F32)] * 3
    return pl.pallas_call(
        functools.partial(_inproj_sample_kernel, d_ret=d_ret, d_moba=d_moba, n_ret_heads=n_ret_heads),
        grid=(1,),
        in_specs=[full(a) for a in args],
        out_specs=[pl.BlockSpec(s.shape, lambda i: (0, 0)) for s in out_shape],
        out_shape=out_shape,
        compiler_params=_cparams("arbitrary"),
        name="inproj_sample",
    )(*args)


def _ret_tables(n_heads, chunk):
    log_g = jnp.log1p(-jnp.exp2(-5.0 - jnp.arange(n_heads, dtype=F32)))
    n = jnp.arange(chunk, dtype=F32)
    rel = n[:, None] - n[None, :]
    causal = rel >= 0
    decay_in = jnp.where(causal[None], jnp.exp(jnp.where(causal, rel, 0.0)[None] * log_g[:, None, None]), 0.0)
    q_dec = jnp.exp((n[:, None] + 1.0) * log_g[None, :])
    k_dec = jnp.exp((chunk - 1.0 - n[:, None]) * log_g[None, :])
    chunk_dec = jnp.exp(chunk * log_g)
    return decay_in, q_dec, k_dec, chunk_dec


def _ret_sample_kernel(rq_ref, rk_ref, rv_ref, rg_ref, rng_ref, s_ref, din_ref, qdec_ref, kdec_ref, cdec_ref,
                       o_ref, s_out_ref, *, n_heads, dk, dv):
    row = lax.broadcasted_iota(jnp.int32, (dk, dk), 0)
    col = lax.broadcasted_iota(jnp.int32, (dk, dk), 1)
    eye = (row == col).astype(BF16)
    rng = rng_ref[...]
    for h in range(n_heads):
        q = rq_ref[0, :, h * dk:(h + 1) * dk]
        k = rk_ref[0, :, h * dk:(h + 1) * dk]
        v = rv_ref[0, :, h * dv:(h + 1) * dv].astype(BF16)
        s = s_ref[0, h]
        scores = lax.dot_general(q.astype(BF16), k.astype(BF16), _NT, preferred_element_type=F32) * din_ref[h]
        inner = jnp.dot(scores.astype(BF16), v, preferred_element_type=F32)
        cross = jnp.dot((q * qdec_ref[h]).astype(BF16), s.astype(BF16), preferred_element_type=F32)
        o_ref[0, :, h * dv:(h + 1) * dv] = _head_norm_gate(
            inner + cross, rng[:, h * dv:(h + 1) * dv], rg_ref[0, :, h * dv:(h + 1) * dv])
        kd = (k * kdec_ref[h]).astype(BF16)
        kdT = lax.dot_general(eye, kd, _NT, preferred_element_type=F32).astype(BF16)
        s_out_ref[0, h] = cdec_ref[h] * s + jnp.dot(kdT, v, preferred_element_type=F32)


def _ret_sample(rq, rk, rv, rg, rng, state, *, n_heads, dk, dv):
    bs, ls, _ = rq.shape
    decay_in, q_dec, k_dec, chunk_dec = _ret_tables(n_heads, ls)
    qdec_b = jnp.broadcast_to(q_dec.T[:, :, None], (n_heads, ls, dk))
    kdec_b = jnp.broadcast_to(k_dec.T[:, :, None], (n_heads, ls, dk))
    cdec_b = jnp.broadcast_to(chunk_dec[:, None, None], (n_heads, dk, dv))
    tok = lambda n: pl.BlockSpec((1, ls, n), lambda b: (b, 0, 0))
    full = lambda a: pl.BlockSpec(a.shape, lambda b: (0,) * a.ndim)
    st = pl.BlockSpec((1, n_heads, dk, dv), lambda b: (b, 0, 0, 0))
    return pl.pallas_call(
        functools.partial(_ret_sample_kernel, n_heads=n_heads, dk=dk, dv=dv),
        grid=(bs,),
        in_specs=[tok(n_heads * dk), tok(n_heads * dk), tok(n_heads * dv), tok(n_heads * dv), full(rng), st,
                  full(decay_in), full(qdec_b), full(kdec_b), full(cdec_b)],
        out_specs=[tok(n_heads * dv), st],
        out_shape=[jax.ShapeDtypeStruct((bs, ls, n_heads * dv), F32),
                   jax.ShapeDtypeStruct((bs, n_heads, dk, dv), F32)],
        compiler_params=_cparams("parallel"),
        name="retention_sample",
    )(rq, rk, rv, rg, rng, state, decay_in, qdec_b, kdec_b, cdec_b)


def _stream_block_sums(pt_ref, ck_ref, ksum_ref, pbuf, acc_scr, sems, *, pps, ppb, gps, n_pages):
    step = (pl.program_id(0) * pl.num_programs(1) + pl.program_id(1)) * pl.num_programs(2) + pl.program_id(2)
    n_steps = pl.num_programs(0) * pl.num_programs(1) * pl.num_programs(2)
    cur = step % 2

    def issue(u, buf):
        base = (u // gps) * n_pages + (u % gps) * pps
        for i in range(pps):
            pltpu.make_async_copy(ck_ref.at[pt_ref[base + i]], pbuf.at[buf, i], sems.at[buf]).start()

    @pl.when(step == 0)
    def _():
        issue(step, cur)

    @pl.when(step + 1 < n_steps)
    def _():
        issue(step + 1, 1 - cur)

    pltpu.make_async_copy(ck_ref.at[pl.ds(0, pps)], pbuf.at[cur], sems.at[cur]).wait()

    group = step % gps
    d_moba, page_size = acc_scr.shape[0], pbuf.shape[-1]
    lane = lax.broadcasted_iota(jnp.int32, acc_scr.shape, 1)
    acc = jnp.where(group == 0, 0.0, acc_scr[...])
    for u in range(pps // ppb):
        ks = pbuf[cur, u * ppb]
        for pg in range(1, ppb):
            ks = ks + pbuf[cur, u * ppb + pg]
        tok_sum = jnp.sum(ks.reshape(d_moba, page_size), axis=1, keepdims=True)
        acc = jnp.where(lane == group * (pps // ppb) + u, tok_sum, acc)
    acc_scr[...] = acc

    @pl.when(group == gps - 1)
    def _():
        ksum_ref[0] = acc


def _moba_prompt_kernel(*refs, blk, hd, n_blocks, topk, stream):
    if stream:
        (pt_ref, q_ref, kTd_ref, vd_ref, kT_ref, v_ref, eneg_ref, cbias_ref, ck_ref,
         o_ref, ksum_ref, km_scr, kaug_scr, pbuf, acc_scr, sems) = refs
        _stream_block_sums(pt_ref, ck_ref, ksum_ref, pbuf, acc_scr, sems, **stream)
    else:
        q_ref, kTd_ref, vd_ref, kT_ref, v_ref, eneg_ref, cbias_ref, o_ref, km_scr, kaug_scr = refs
    c = pl.program_id(2)
    width = 2 * hd
    scale = hd ** -0.5

    @pl.when(c == 0)
    def _():
        lane = lax.broadcasted_iota(jnp.int32, (width, LANES), 1)
        km = jnp.zeros((width, LANES), F32)
        for n in range(n_blocks):
            s = jnp.sum(kT_ref[0, :, n * blk:(n + 1) * blk], axis=1, keepdims=True) * (1.0 / blk)
            km = jnp.where(lane == n, s, km)
        km_scr[...] = km.T
        kaug_scr[0:width, :] = kT_ref[0].astype(BF16)
        kaug_scr[width:width + LANES, :] = eneg_ref[...]

    q = q_ref[...]
    lane_q = lax.broadcasted_iota(jnp.int32, (blk, width), 1)
    zq = jnp.zeros_like(q)
    qst = jnp.concatenate([jnp.where(lane_q < hd, q, zq), jnp.where(lane_q >= hd, q, zq)], axis=0)
    kmbT = km_scr[...].astype(BF16)
    kd = kTd_ref[0].astype(BF16)
    vd = vd_ref[...]

    nbp = -(-n_blocks // 8) * 8
    row_b = lax.broadcasted_iota(jnp.int32, (nbp, 2 * blk), 0)
    gate = lax.dot_general(kmbT, qst, _NT, preferred_element_type=F32)[0:nbp, :]
    gate = jnp.where(row_b < c, gate, NEG)
    cnt = jnp.zeros((nbp, 2 * blk), jnp.int32)
    for m in range(n_blocks):
        gm = gate[m:m + 1, :]
        beats = (gm > gate) | ((gm == gate) & (m < row_b))
        cnt = cnt + beats.astype(jnp.int32)
    sel = (row_b < c) & (cnt < topk)
    penT = jnp.where(sel | (row_b >= n_blocks), 0.0, 1.0)
    penT = jnp.concatenate([penT, jnp.zeros((LANES - nbp, 2 * blk), F32)], axis=0)
    pen = penT.T.astype(BF16)
    qs = qst * scale
    qa = jnp.concatenate([qs, pen], axis=1)
    s_d = jnp.dot(qs, kd, preferred_element_type=F32) + cbias_ref[...]
    m_d = jnp.max(s_d, axis=1, keepdims=True)

    def finish(n_wide):
        outs = []
        for j in range(2):
            rows = slice(j * blk, (j + 1) * blk)
            s_dj, m_dj = s_d[rows], m_d[rows]
            if n_wide:
                w = n_wide * blk
                s_w = jnp.dot(qa[rows], kaug_scr[:, :w], preferred_element_type=F32)
                mx = jnp.maximum(m_dj, jnp.max(s_w, axis=1, keepdims=True))
                p_w = jnp.exp(s_w - mx)
                p_d = jnp.exp(s_dj - mx)
                den = jnp.sum(p_d, axis=1, keepdims=True) + jnp.sum(p_w, axis=1, keepdims=True)
                acc = (jnp.dot(p_d.astype(BF16), vd, preferred_element_type=F32)
                       + jnp.dot(p_w.astype(BF16), v_ref[0:w, :], preferred_element_type=F32))
            else:
                p_d = jnp.exp(s_dj - m_dj)
                den = jnp.sum(p_d, axis=1, keepdims=True)
                acc = jnp.dot(p_d.astype(BF16), vd, preferred_element_type=F32)
            outs.append(acc / den)
        o_ref[...] = jnp.where(lane_q < hd, outs[0], outs[1]).astype(BF16)

    widths = list(range(1, n_blocks))
    lo = 1
    for n_wide in widths:
        @pl.when((c >= lo) & (c <= n_wide))
        def _(n_wide=n_wide):
            finish(n_wide)
        lo = n_wide + 1

    @pl.when(c == 0)
    def _():
        finish(0)


def _stream_plan(n_steps, page_table, cacheT):
    bs, n_pages = page_table.shape
    ppb = MOBA_BLOCK // cacheT.shape[-1]
    total = bs * n_pages
    if total % n_steps:
        return None
    pps = total // n_steps
    if pps % ppb or n_pages % pps or cacheT.shape[0] < pps:
        return None
    return dict(pps=pps, ppb=ppb, gps=n_pages // pps, n_pages=n_pages)


def _moba_prompt(mq, mv, kT, *, batch, seq, n_heads, hd, page_table=None, cacheT=None):
    blk = MOBA_BLOCK
    nb = seq // blk
    width = 2 * hd
    npair = n_heads // 2
    assert nb <= LANES and width == LANES
    eneg = jnp.where(jnp.arange(LANES)[:, None] == (jnp.arange(seq) // blk)[None, :], NEG, 0.0).astype(BF16)
    cbias = jnp.where(jnp.arange(blk)[None, :] <= (jnp.arange(2 * blk) % blk)[:, None], 0.0, -jnp.inf).astype(F32)
    stream = None if cacheT is None else _stream_plan(batch * npair * nb, page_table, cacheT)
    in_specs = [
        pl.BlockSpec((blk, width), lambda b, hp, c, *_: (b * nb + c, hp)),
        pl.BlockSpec((1, width, blk), lambda b, hp, c, *_: (b, hp, c)),
        pl.BlockSpec((blk, width), lambda b, hp, c, *_: (b * nb + c, hp)),
        pl.BlockSpec((1, width, seq), lambda b, hp, c, *_: (b, hp, 0)),
        pl.BlockSpec((seq, width), lambda b, hp, c, *_: (b, hp)),
        pl.BlockSpec((LANES, seq), lambda b, hp, c, *_: (0, 0)),
        pl.BlockSpec((2 * blk, blk), lambda b, hp, c, *_: (0, 0)),
    ]
    out_specs = [pl.BlockSpec((blk, width), lambda b, hp, c, *_: (b * nb + c, hp))]
    out_shape = [jax.ShapeDtypeStruct((batch * seq, n_heads * hd), BF16)]
    scratch = [pltpu.VMEM((width, LANES), F32), pltpu.VMEM((width + LANES, seq), BF16)]
    args = [mq, kT, mv, kT, mv, eneg, cbias]
    if stream:
        bs = page_table.shape[0]
        mh_all, hd_all, page_size = cacheT.shape[1:]
        gps = stream["gps"]
        in_specs.append(pl.BlockSpec(memory_space=pl.ANY))
        args.append(cacheT)
        out_specs.append(pl.BlockSpec((1, mh_all * hd_all, LANES),
                                      lambda b, hp, c, *_: (((b * npair + hp) * nb + c) // gps, 0, 0)))
        out_shape.append(jax.ShapeDtypeStruct((bs, mh_all * hd_all, LANES), F32))
        scratch += [pltpu.VMEM((2, stream["pps"], mh_all, hd_all, page_size), F32),
                    pltpu.VMEM((mh_all * hd_all, LANES), F32), pltpu.SemaphoreType.DMA((2,))]
        args = [page_table.reshape(-1)] + args
    grid_spec = pltpu.PrefetchScalarGridSpec(
        num_scalar_prefetch=1 if stream else 0,
        grid=(batch, npair, nb),
        in_specs=in_specs,
        out_specs=out_specs,
        scratch_shapes=scratch,
    )
    outs = pl.pallas_call(
        functools.partial(_moba_prompt_kernel, blk=blk, hd=hd, n_blocks=nb, topk=MOBA_TOPK, stream=stream),
        grid_spec=grid_spec,
        out_shape=out_shape,
        compiler_params=_cparams("arbitrary", "arbitrary", "arbitrary"),
        name="moba_prompt",
    )(*args)
    return (outs[0], outs[1]) if stream else (outs[0], None)


def _moba_gate_kernel(pt_ref, q_ref, *refs, pages_per_step, pages_per_block, n_heads, hd, ls, n_past, topk):
    del pt_ref
    pages = refs[:pages_per_step]
    idx_ref = refs[pages_per_step]
    km_scr = refs[pages_per_step + 1]
    j = pl.program_id(1)
    nj = pl.num_programs(1)
    d_moba = n_heads * hd
    page_size = pages[0].shape[-1]
    blocks_per_step = pages_per_step // pages_per_block

    @pl.when(j == 0)
    def _():
        km_scr[...] = jnp.zeros_like(km_scr)

    col = lax.broadcasted_iota(jnp.int32, (page_size, LANES), 1)
    acc = km_scr[...]
    for u in range(blocks_per_step):
        ks = pages[u * pages_per_block][0]
        for pg in range(1, pages_per_block):
            ks = ks + pages[u * pages_per_block + pg][0]
        ks = ks.reshape(d_moba, page_size).astype(BF16)
        onecol = (col == j * blocks_per_step + u).astype(BF16)
        acc = acc + jnp.dot(ks, onecol, preferred_element_type=F32)
    km_scr[...] = acc

    @pl.when(j == nj - 1)
    def _():
        idx_ref[0] = _gate_topk(km_scr[...], q_ref[0], block_len=pages_per_block * page_size,
                                n_heads=n_heads, hd=hd, n_past=n_past, topk=topk)


def _gate_topk(ksum, q, *, block_len, n_heads, hd, n_past, topk):
    ls, d_moba = q.shape
    kmean = (ksum * (1.0 / block_len)).astype(BF16)
    rows = ls * n_heads
    qrep = jnp.concatenate([jnp.broadcast_to(q[i:i + 1, :], (n_heads, d_moba)) for i in range(ls)], axis=0)
    r = lax.broadcasted_iota(jnp.int32, (rows, d_moba), 0)
    cc = lax.broadcasted_iota(jnp.int32, (rows, d_moba), 1)
    qbd = jnp.where((cc // hd) == (r % n_heads), qrep, 0.0).astype(BF16)
    gate = jnp.dot(qbd, kmean, preferred_element_type=F32)
    lane = lax.broadcasted_iota(jnp.int32, (rows, LANES), 1)
    work = jnp.where(lane < n_past, gate, jnp.where(lane == n_past, NEG, -jnp.inf))
    out = jnp.zeros((rows, LANES), jnp.int32)
    for t in range(topk):
        mx = jnp.max(work, axis=1, keepdims=True)
        it = jnp.min(jnp.where(work == mx, lane, LANES), axis=1, keepdims=True)
        out = jnp.where(lane == t, it, out)
        work = jnp.where(lane == it, -jnp.inf, work)
    return out


def _moba_gate_top_kernel(ksum_ref, q_ref, idx_ref, **kw):
    idx_ref[0] = _gate_topk(ksum_ref[0], q_ref[0], **kw)


def _moba_gate_top(ksum, mq_s, *, n_heads, hd, n_past):
    bs, ls, d_moba = mq_s.shape
    return pl.pallas_call(
        functools.partial(_moba_gate_top_kernel, block_len=MOBA_BLOCK, n_heads=n_heads, hd=hd, n_past=n_past,
                          topk=MOBA_TOPK),
        grid=(bs,),
        in_specs=[pl.BlockSpec((1, d_moba, LANES), lambda b: (b, 0, 0)),
                  pl.BlockSpec((1, ls, d_moba), lambda b: (b, 0, 0))],
        out_specs=pl.BlockSpec((1, ls * n_heads, LANES), lambda b: (b, 0, 0)),
        out_shape=jax.ShapeDtypeStruct((bs, ls * n_heads, LANES), jnp.int32),
        compiler_params=_cparams("parallel"),
        name="moba_sample_gate_top",
    )(ksum, mq_s)


def _moba_gate(page_table, mq_s, cacheT, *, n_heads, hd, ls, pages_per_step):
    bs, n_pages = page_table.shape
    page_size = cacheT.shape[-1]
    ppb = MOBA_BLOCK // page_size
    n_past = n_pages // ppb
    steps = n_pages // pages_per_step
    d_moba = n_heads * hd

    def page_spec(i):
        return pl.BlockSpec((1, n_heads, hd, page_size),
                            lambda b, j, pt: (pt[b * n_pages + j * pages_per_step + i], 0, 0, 0))

    grid_spec = pltpu.PrefetchScalarGridSpec(
        num_scalar_prefetch=1,
        grid=(bs, steps),
        in_specs=[pl.BlockSpec((1, ls, d_moba), lambda b, j, pt: (b, 0, 0))]
        + [page_spec(i) for i in range(pages_per_step)],
        out_specs=pl.BlockSpec((1, ls * n_heads, LANES), lambda b, j, pt: (b, 0, 0)),
        scratch_shapes=[pltpu.VMEM((d_moba, LANES), F32)],
    )
    return pl.pallas_call(
        functools.partial(_moba_gate_kernel, pages_per_step=pages_per_step, pages_per_block=ppb,
                          n_heads=n_heads, hd=hd, ls=ls, n_past=n_past, topk=MOBA_TOPK),
        grid_spec=grid_spec,
        out_shape=jax.ShapeDtypeStruct((bs, ls * n_heads, LANES), jnp.int32),
        compiler_params=_cparams("parallel", "arbitrary"),
        name="moba_sample_gate",
    )(page_table.reshape(-1), mq_s, *([cacheT] * pages_per_step))


def _moba_sample_kernel(pt_ref, idx_ref, qc_ref, kc_ref, vc_ref, ck_ref, cv_ref, o_ref, kbuf, vbuf, sems,
                        *, ls, topk, ppb, n_heads, hd, n_past, n_pages):
    b = pl.program_id(0)
    nb = pl.num_programs(0)
    cur = b % 2
    per_head = ls * topk * ppb
    rows_per_head = per_head // n_heads
    scale = hd ** -0.5

    def slab_at(buf_ref, buf, h, r):
        return buf_ref.at[buf, h * rows_per_head + r // n_heads, r % n_heads]

    def issue(bb, buf):
        def per_head_body(h, c):
            for i in range(ls):
                for j in range(topk):
                    blk = jnp.minimum(idx_ref[((bb * ls + i) * n_heads + h) * topk + j], n_past - 1)
                    for pg in range(ppb):
                        page = pt_ref[bb * n_pages + blk * ppb + pg]
                        r = (i * topk + j) * ppb + pg
                        pltpu.make_async_copy(ck_ref.at[page, h], slab_at(kbuf, buf, h, r), sems.at[buf]).start()
                        pltpu.make_async_copy(cv_ref.at[page, h], slab_at(vbuf, buf, h, r), sems.at[buf]).start()
            return c

        lax.fori_loop(0, n_heads, per_head_body, 0)

    @pl.when(b == 0)
    def _():
        issue(b, cur)

    @pl.when(b + 1 < nb)
    def _():
        issue(b + 1, 1 - cur)

    pltpu.make_async_copy(ck_ref.at[pl.ds(0, per_head)], kbuf.at[cur], sems.at[cur]).wait()
    pltpu.make_async_copy(cv_ref.at[pl.ds(0, per_head)], vbuf.at[cur], sems.at[cur]).wait()

    lane_own = lax.broadcasted_iota(jnp.int32, (1, ls), 1)

    def head_body(h, c):
        qc = qc_ref[0, h]
        knew = kc_ref[0, h]
        vnew = vc_ref[0, h]
        for i in range(ls):
            qcol = qc[:, i:i + 1]
            s_list = []
            for j in range(topk):
                ok = idx_ref[((b * ls + i) * n_heads + h) * topk + j] < n_past
                for pg in range(ppb):
                    kt = slab_at(kbuf, cur, h, (i * topk + j) * ppb + pg)[...]
                    s = jnp.sum(kt * qcol, axis=0, keepdims=True) * scale
                    s_list.append(jnp.where(ok, s, -jnp.inf))
            s_own = jnp.sum(knew * qcol, axis=0, keepdims=True) * scale
            s_own = jnp.where(lane_own <= i, s_own, -jnp.inf)
            mx = jnp.max(s_own, axis=1, keepdims=True)
            for s in s_list:
                mx = jnp.maximum(mx, jnp.max(s, axis=1, keepdims=True))
            p_own = jnp.exp(s_own - mx)
            denom = jnp.sum(p_own, axis=1, keepdims=True)
            o = jnp.sum(vnew * p_own, axis=1, keepdims=True)
            pv = None
            for t, s in enumerate(s_list):
                p = jnp.exp(s - mx)
                denom = denom + jnp.sum(p, axis=1, keepdims=True)
                term = slab_at(vbuf, cur, h, i * topk * ppb + t)[...] * p
                pv = term if pv is None else pv + term
            o = o + jnp.sum(pv, axis=1, keepdims=True)
            o_ref[0, h, :, i:i + 1] = o / denom
        return c

    lax.fori_loop(0, n_heads, head_body, 0)


def _moba_sample(page_table, idx, q_cols, k_cols, v_cols, cacheT_k, cacheT_v, *, n_heads, hd, ls):
    bs, n_pages = page_table.shape
    page_size = cacheT_k.shape[-1]
    ppb = MOBA_BLOCK // page_size
    n_past = n_pages // ppb
    topk = MOBA_TOPK
    per_head = ls * topk * ppb
    assert per_head % n_heads == 0 and cacheT_k.shape[0] >= per_head
    col_spec = pl.BlockSpec((1, n_heads, hd, ls), lambda b, pt, ix: (b, 0, 0, 0))
    any_spec = pl.BlockSpec(memory_space=pl.ANY)
    slab_buf = pltpu.VMEM((2, per_head, n_heads, hd, page_size), F32)
    grid_spec = pltpu.PrefetchScalarGridSpec(
        num_scalar_prefetch=2,
        grid=(bs,),
        in_specs=[col_spec, col_spec, col_spec, any_spec, any_spec],
        out_specs=col_spec,
        scratch_shapes=[slab_buf, slab_buf, pltpu.SemaphoreType.DMA((2,))],
    )
    return pl.pallas_call(
        functools.partial(_moba_sample_kernel, ls=ls, topk=topk, ppb=ppb, n_heads=n_heads, hd=hd, n_past=n_past,
                          n_pages=n_pages),
        grid_spec=grid_spec,
        out_shape=jax.ShapeDtypeStruct((bs, n_heads, hd, ls), F32),
        compiler_params=_cparams("arbitrary"),
        name="moba_sample_attn",
    )(page_table.reshape(-1), idx, q_cols, k_cols, v_cols, cacheT_k, cacheT_v)


def _merge_kernel(*refs, n_in_tiles, **kw):
    i = pl.program_id(0)

    @pl.when(i < n_in_tiles)
    def _():
        _merge_body(*refs, **kw)

    @pl.when(i >= n_in_tiles)
    def _():
        hres_ref, xn_ref, slot_ref, w_ref, cnt_ref = refs[-5:]
        hres_ref[...] = jnp.zeros_like(hres_ref)
        xn_ref[...] = jnp.zeros_like(xn_ref)
        slot_ref[...] = jnp.full(slot_ref.shape, -1, jnp.int32)
        w_ref[...] = jnp.zeros_like(w_ref)
        cnt_ref[...] = jnp.zeros_like(cnt_ref)


def _merge_body(x_ref, ret_ref, mo_ref, wo_ref, n2_ref, rwT_ref, rb_ref, *refs, n_experts, top_k, n_valid):
    hres_ref, xn_ref, slot_ref, w_ref, cnt_ref = refs[-5:]
    tm = x_ref.shape[0]
    d_ret = ret_ref.shape[1]
    mix = (jnp.dot(ret_ref[...], wo_ref[0:d_ret, :], preferred_element_type=F32)
           + jnp.dot(mo_ref[...], wo_ref[d_ret:, :], preferred_element_type=F32))
    hres = x_ref[...] + mix
    hres_ref[...] = hres
    xn = _rms(hres, n2_ref[...]).astype(BF16)
    xn_ref[...] = xn
    logits = lax.dot_general(rwT_ref[...], xn, _NT, preferred_element_type=F32) + rb_ref[...]
    row = lax.broadcasted_iota(jnp.int32, logits.shape, 0)
    work = logits
    vals, hots = [], []
    for _ in range(top_k):
        mx = jnp.max(work, axis=0, keepdims=True)
        it = jnp.min(jnp.where(work == mx, row, n_experts), axis=0, keepdims=True)
        hot = row == it
        vals.append(mx)
        hots.append(hot)
        work = jnp.where(hot, -jnp.inf, work)
    exps = [jnp.exp(v - vals[0]) for v in vals]
    denom = exps[0]
    for e in exps[1:]:
        denom = denom + e

    valid = lax.broadcasted_iota(jnp.int32, (1, tm), 1) < n_valid
    mask = hots[0]
    for hot in hots[1:]:
        mask = mask | hot
    mask = mask & valid
    maskf = mask.astype(F32)
    r_i = lax.broadcasted_iota(jnp.int32, (tm, tm), 0)
    c_i = lax.broadcasted_iota(jnp.int32, (tm, tm), 1)
    rank = jnp.dot(maskf.astype(BF16), (r_i < c_i).astype(BF16), preferred_element_type=F32)
    cnt = jnp.sum(maskf, axis=1, keepdims=True).astype(jnp.int32)
    cnt_pad = ((cnt + (SLOT_ALIGN - 1)) // SLOT_ALIGN) * SLOT_ALIGN
    e_r = lax.broadcasted_iota(jnp.int32, (n_experts, n_experts), 0)
    e_c = lax.broadcasted_iota(jnp.int32, (n_experts, n_experts), 1)
    cpb = jnp.broadcast_to(cnt_pad.astype(F32), (n_experts, LANES)).astype(BF16)
    tile_off = jnp.dot((e_c < e_r).astype(BF16), cpb, preferred_element_type=F32)[:, 0:1]
    slot = tile_off + rank

    row8 = lax.broadcasted_iota(jnp.int32, (8, tm), 0)
    slot_out = jnp.full((8, tm), -1, jnp.int32)
    w_out = jnp.zeros((8, tm), F32)
    for k in range(top_k):
        sk = jnp.sum(jnp.where(hots[k], slot, 0.0), axis=0, keepdims=True).astype(jnp.int32)
        sk = jnp.where(valid, sk, -1)
        slot_out = jnp.where(row8 == k, sk, slot_out)
        w_out = jnp.where(row8 == k, exps[k] / denom, w_out)
    slot_ref[...] = slot_out
    w_ref[...] = w_out
    cnt_ref[0] = jnp.broadcast_to(cnt, (n_experts, LANES))


def _merge(x, ret, moba_o, w_out, norm2_g, rwT, router_b, prev, *,
           tm, n_tiles_total, tile0, n_valid, n_clear_tiles=0):
    t, d_model = x.shape
    n_experts = rwT.shape[0]
    t_pad = n_tiles_total * tm
    n_in = t // tm
    tok_in = lambda n: pl.BlockSpec((tm, n), lambda i: (jnp.minimum(i, n_in - 1), 0))
    tok = lambda n: pl.BlockSpec((tm, n), lambda i: (tile0 + i, 0))
    full = lambda a: pl.BlockSpec(a.shape, lambda i: (0,) * a.ndim)
    out_shape = [jax.ShapeDtypeStruct((t_pad, d_model), F32), jax.ShapeDtypeStruct((t_pad, d_model), BF16),
                 jax.ShapeDtypeStruct((8, t_pad), jnp.int32), jax.ShapeDtypeStruct((8, t_pad), F32),
                 jax.ShapeDtypeStruct((n_tiles_total, n_experts, LANES), jnp.int32)]
    out_specs = [tok(d_model), tok(d_model),
                 pl.BlockSpec((8, tm), lambda i: (0, tile0 + i)), pl.BlockSpec((8, tm), lambda i: (0, tile0 + i)),
                 pl.BlockSpec((1, n_experts, LANES), lambda i: (tile0 + i, 0, 0))]
    args = [x, ret, moba_o, w_out, norm2_g, rwT, router_b]
    in_specs = [tok_in(d_model), tok_in(ret.shape[1]), tok_in(moba_o.shape[1]),
                full(w_out), full(norm2_g), full(rwT), full(router_b)]
    aliases = {}
    if prev is not None:
        aliases = {len(args) + k: k for k in range(len(prev))}
        args += list(prev)
        in_specs += [pl.BlockSpec(memory_space=pl.ANY)] * len(prev)
    return pl.pallas_call(
        functools.partial(_merge_kernel, n_experts=n_experts, top_k=TOP_K, n_valid=n_valid, n_in_tiles=n_in),
        grid=(n_in + n_clear_tiles,),
        in_specs=in_specs,
        out_specs=out_specs,
        out_shape=out_shape,
        input_output_aliases=aliases,
        compiler_params=_cparams("parallel"),
        name="merge_router",
    )(*args)


def _run_copies(cp_ref, to_ref, dr_ref, t, n_experts, make_copy):
    def per_expert(e, carry):
        rows = cp_ref[t * n_experts + e]
        src0 = to_ref[t * n_experts + e]
        dst0 = dr_ref[t * n_experts + e]
        n_big = rows // RUN_PIECE
        done = n_big * RUN_PIECE

        def big(j, c):
            make_copy(pl.multiple_of(src0 + j * RUN_PIECE, SLOT_ALIGN),
                      pl.multiple_of(dst0 + j * RUN_PIECE, SLOT_ALIGN), RUN_PIECE).start()
            return c

        def small(j, c):
            make_copy(pl.multiple_of(src0 + done + j * SLOT_ALIGN, SLOT_ALIGN),
                      pl.multiple_of(dst0 + done + j * SLOT_ALIGN, SLOT_ALIGN), SLOT_ALIGN).start()
            return c

        lax.fori_loop(0, n_big, big, 0)
        lax.fori_loop(0, (rows - done) // SLOT_ALIGN, small, 0)
        return carry

    lax.fori_loop(0, n_experts, per_expert, 0)


def _run_rows(cp_ref, t, n_experts):
    return lax.fori_loop(0, n_experts, lambda e, c: c + cp_ref[t * n_experts + e], 0)


def _wait_rows(rows, make_copy):
    lax.fori_loop(0, rows // WAIT_PIECE, lambda j, c: (make_copy(0, 0, WAIT_PIECE).wait(), c)[1], 0)
    lax.fori_loop(0, (rows % WAIT_PIECE) // SLOT_ALIGN, lambda j, c: (make_copy(0, 0, SLOT_ALIGN).wait(), c)[1], 0)


def _dispatch_kernel(cp_ref, to_ref, dr_ref, ts_ref, tn_ref, xn_ref, slot_ref, xs_ref, slots_scr, zero_scr, sems,
                     *, n_experts, n_slots, chunk, top_k):
    t = pl.program_id(0)
    nt = pl.num_programs(0)
    cur = t % 2
    tm = xn_ref.shape[0]
    xn = xn_ref[...]
    sl = slot_ref[...]
    used = to_ref[t * n_experts + n_experts - 1] + cp_ref[t * n_experts + n_experts - 1]
    def sort_chunk(c):
        s_iota = c * chunk + lax.broadcasted_iota(jnp.int32, (chunk, tm), 0)
        pm = sl[0:1, :] == s_iota
        for k in range(1, top_k):
            pm = pm | (sl[k:k + 1, :] == s_iota)
        slots_scr[cur, c * chunk:(c + 1) * chunk, :] = jnp.dot(
            pm.astype(BF16), xn, preferred_element_type=F32).astype(BF16)

    for c in range(n_slots // chunk):
        if (c + 1) * chunk <= tm * top_k:
            sort_chunk(c)
        else:
            pl.when(c * chunk < used)(functools.partial(sort_chunk, c))

    def copy_from(buf):
        def make_copy(src, dst, rows):
            return pltpu.make_async_copy(slots_scr.at[buf, pl.ds(src, rows)], xs_ref.at[pl.ds(dst, rows)],
                                         sems.at[buf])
        return make_copy

    _run_copies(cp_ref, to_ref, dr_ref, t, n_experts, copy_from(cur))

    @pl.when(t > 0)
    def _():
        _wait_rows(_run_rows(cp_ref, t - 1, n_experts), copy_from(1 - cur))

    @pl.when(t == nt - 1)
    def _():
        zero_scr[...] = jnp.zeros_like(zero_scr)

        def zcopy(dst):
            return pltpu.make_async_copy(zero_scr.at[pl.ds(0, SLOT_ALIGN)], xs_ref.at[pl.ds(dst, SLOT_ALIGN)],
                                         sems.at[cur])

        def per_expert(e, tot):
            def one(j, c):
                zcopy(pl.multiple_of(ts_ref[e] + j * SLOT_ALIGN, SLOT_ALIGN)).start()
                return c
            lax.fori_loop(0, tn_ref[e], one, 0)
            return tot + tn_ref[e]

        n_tail = lax.fori_loop(0, n_experts, per_expert, 0)
        _wait_rows(_run_rows(cp_ref, t, n_experts) + n_tail * SLOT_ALIGN, copy_from(cur))

        tme = zero_scr.shape[0]

        def ztile(j):
            return pltpu.make_async_copy(zero_scr, xs_ref.at[pl.ds(pl.multiple_of(j * tme, tme), tme)], sems.at[cur])

        n_used = tn_ref[n_experts]
        n_all = xs_ref.shape[0] // tme
        lax.fori_loop(n_used, n_all, lambda j, c: (ztile(j).start(), c)[1], 0)
        lax.fori_loop(n_used, n_all, lambda j, c: (ztile(j).wait(), c)[1], 0)


def _dispatch(cp, to, dr, ts, tn, xn_all, slot, *, tm, tme, n_experts, n_slots, p_rows):
    t_pad, d_model = xn_all.shape
    grid_spec = pltpu.PrefetchScalarGridSpec(
        num_scalar_prefetch=5,
        grid=(t_pad // tm,),
        in_specs=[pl.BlockSpec((tm, d_model), lambda i, *_: (i, 0)),
                  pl.BlockSpec((8, tm), lambda i, *_: (0, i))],
        out_specs=pl.BlockSpec(memory_space=pl.ANY),
        scratch_shapes=[pltpu.VMEM((2, n_slots, d_model), BF16), pltpu.VMEM((tme, d_model), BF16),
                        pltpu.SemaphoreType.DMA((2,))],
    )
    return pl.pallas_call(
        functools.partial(_dispatch_kernel, n_experts=n_experts, n_slots=n_slots, chunk=256, top_k=TOP_K),
        grid_spec=grid_spec,
        out_shape=jax.ShapeDtypeStruct((p_rows, d_model), BF16),
        compiler_params=_cparams("arbitrary"),
        name="moe_dispatch",
    )(cp, to, dr, ts, tn, xn_all, slot)


def _expert_kernel(te_ref, tv_ref, xi_ref, x_ref, wgu_ref, bgu_ref, wd_ref, bd_ref, y_ref, wp_scr, wdb_scr, *, d_ff):
    j = pl.program_id(0)
    grp = 2 * LANES
    n_grp = 2 * d_ff // grp

    @pl.when(tv_ref[j] > 0)
    def _():
        first = (j == 0) | (te_ref[j] != te_ref[jnp.maximum(j - 1, 0)])

        @pl.when(first)
        def _():
            r = lax.broadcasted_iota(jnp.int32, (grp, grp), 0)
            c = lax.broadcasted_iota(jnp.int32, (grp, grp), 1)
            perm = (((c < LANES) & (r == 2 * c)) | ((c >= LANES) & (r == 2 * (c - LANES) + 1))).astype(BF16)
            for g in range(n_grp):
                wp_scr[:, g * grp:(g + 1) * grp] = jnp.dot(
                    wgu_ref[0, :, g * grp:(g + 1) * grp].astype(BF16), perm, preferred_element_type=F32).astype(BF16)
            wdb_scr[...] = wd_ref[0].astype(BF16)

        def ffn(rows):
            x = x_ref[0:rows, :]
            parts = []
            for g in range(n_grp):
                u = (jnp.dot(x, wp_scr[:, g * grp:(g + 1) * grp], preferred_element_type=F32)
                     + bgu_ref[0, :, g * grp:(g + 1) * grp])
                glu = jnp.minimum(u[:, :LANES], SWIGLU_LIMIT)
                lin = jnp.clip(u[:, LANES:], -SWIGLU_LIMIT, SWIGLU_LIMIT)
                parts.append((glu * jax.nn.sigmoid(SWIGLU_ALPHA * glu) * (lin + 1.0)).astype(BF16))
            a = jnp.concatenate(parts, axis=1)
            y_ref[0:rows, :] = (jnp.dot(a, wdb_scr[...], preferred_element_type=F32) + bd_ref[0]).astype(BF16)

        tme = x_ref.shape[0]

        @pl.when(tv_ref[j] > tme // 2)
        def _():
            ffn(tme)

        @pl.when(tv_ref[j] <= tme // 2)
        def _():
            ffn(tme // 2)
            y_ref[tme // 2:, :] = jnp.zeros((tme - tme // 2, y_ref.shape[1]), y_ref.dtype)

    @pl.when(tv_ref[j] == 0)
    def _():
        y_ref[...] = jnp.zeros_like(y_ref)


def _experts(te, tv, xi, x_sorted, wgu, bgu_perm, wd, bd, *, tme):
    p_rows, d_model = x_sorted.shape
    n_experts, _, d_ff2 = wgu.shape
    d_ff = d_ff2 // 2
    ex = lambda a: pl.BlockSpec((1,) + a.shape[1:], lambda j, te_, tv_, xi_: (te_[j],) + (0,) * (a.ndim - 1))
    grid_spec = pltpu.PrefetchScalarGridSpec(
        num_scalar_prefetch=3,
        grid=(p_rows // tme,),
        in_specs=[pl.BlockSpec((tme, d_model), lambda j, te_, tv_, xi_: (xi_[j], 0)),
                  ex(wgu), ex(bgu_perm), ex(wd), ex(bd)],
        out_specs=pl.BlockSpec((tme, d_model), lambda j, *_: (j, 0)),
        scratch_shapes=[pltpu.VMEM((d_model, d_ff2), BF16), pltpu.VMEM((d_ff, d_model), BF16)],
    )
    return pl.pallas_call(
        functools.partial(_expert_kernel, d_ff=d_ff),
        grid_spec=grid_spec,
        out_shape=jax.ShapeDtypeStruct((p_rows, d_model), BF16),
        compiler_params=_cparams("arbitrary"),
        name="moe_experts",
    )(te, tv, xi, x_sorted, wgu, bgu_perm, wd, bd)


def _combine_kernel(cp_ref, to_ref, dr_ref, hres_ref, slotT_ref, wT_ref, fg_ref, ys_ref, yp_ref, ysm_ref,
                    slots_scr, sems, *, n_experts, n_slots, chunk, top_k, n_prompt_tiles):
    t = pl.program_id(0)
    nt = pl.num_programs(0)
    cur = t % 2
    tm, d_model = hres_ref.shape

    def copy_into(buf):
        def make_copy(src, dst, rows):
            return pltpu.make_async_copy(ys_ref.at[pl.ds(dst, rows)], slots_scr.at[buf, pl.ds(src, rows)],
                                         sems.at[buf])
        return make_copy

    def fetch(tile, buf):
        _run_copies(cp_ref, to_ref, dr_ref, tile, n_experts, copy_into(buf))
        used = to_ref[tile * n_experts + n_experts - 1] + cp_ref[tile * n_experts + n_experts - 1]

        def zero_one(j, c):
            slots_scr[buf, pl.ds(pl.multiple_of(used + j * SLOT_ALIGN, SLOT_ALIGN), SLOT_ALIGN), :] = jnp.zeros(
                (SLOT_ALIGN, d_model), BF16)
            return c

        lax.fori_loop(0, (n_slots - used) // SLOT_ALIGN, zero_one, 0)

    @pl.when(t == 0)
    def _():
        fetch(t, cur)

    @pl.when(t + 1 < nt)
    def _():
        fetch(t + 1, 1 - cur)

    _wait_rows(_run_rows(cp_ref, t, n_experts), copy_into(cur))

    sl = slotT_ref[...]
    wt = wT_ref[...]
    acc = hres_ref[...]
    for c in range(n_slots // chunk):
        s_iota = c * chunk + lax.broadcasted_iota(jnp.int32, (tm, chunk), 1)
        pw = jnp.where(sl[:, 0:1] == s_iota, wt[:, 0:1], 0.0)
        for k in range(1, top_k):
            pw = pw + jnp.where(sl[:, k:k + 1] == s_iota, wt[:, k:k + 1], 0.0)
        acc = acc + jnp.dot(pw.astype(BF16), slots_scr[cur, c * chunk:(c + 1) * chunk, :],
                            preferred_element_type=F32)
    y = _rms(acc, fg_ref[...])

    @pl.when(t < n_prompt_tiles)
    def _():
        yp_ref[...] = y

    @pl.when(t >= n_prompt_tiles)
    def _():
        ysm_ref[...] = y[:ysm_ref.shape[0], :]


def _combine(cp, to, dr, hres_all, slotT, wT, fg, y_sorted, *, tm, n_experts, n_slots, n_prompt_tiles, n_sample):
    t_pad, d_model = hres_all.shape
    npt = n_prompt_tiles
    grid_spec = pltpu.PrefetchScalarGridSpec(
        num_scalar_prefetch=3,
        grid=(t_pad // tm,),
        in_specs=[pl.BlockSpec((tm, d_model), lambda i, *_: (i, 0)),
                  pl.BlockSpec((tm, 8), lambda i, *_: (i, 0)),
                  pl.BlockSpec((tm, 8), lambda i, *_: (i, 0)),
                  pl.BlockSpec(fg.shape, lambda i, *_: (0, 0)),
                  pl.BlockSpec(memory_space=pl.ANY)],
        out_specs=[pl.BlockSpec((tm, d_model), lambda i, *_: (jnp.minimum(i, npt - 1), 0)),
                   pl.BlockSpec((n_sample, d_model), lambda i, *_: (0, 0))],
        scratch_shapes=[pltpu.VMEM((2, n_slots, d_model), BF16), pltpu.SemaphoreType.DMA((2,))],
    )
    return pl.pallas_call(
        functools.partial(_combine_kernel, n_experts=n_experts, n_slots=n_slots, chunk=256, top_k=TOP_K,
                          n_prompt_tiles=npt),
        grid_spec=grid_spec,
        out_shape=[jax.ShapeDtypeStruct((npt * tm, d_model), F32), jax.ShapeDtypeStruct((n_sample, d_model), F32)],
        compiler_params=_cparams("arbitrary"),
        name="moe_combine",
    )(cp, to, dr, hres_all, slotT, wT, fg, y_sorted)


def _route_tables(cnt, *, tme, n_row_tiles):
    n_tiles, n_experts = cnt.shape
    cnt_pad = (cnt + (SLOT_ALIGN - 1)) // SLOT_ALIGN * SLOT_ALIGN
    tile_off = jnp.cumsum(cnt_pad, axis=1) - cnt_pad
    tot = jnp.sum(cnt_pad, axis=0)
    tot_t = (tot + (tme - 1)) // tme * tme
    ends = jnp.cumsum(tot_t)
    base = ends - tot_t
    dst_row = base[None, :] + jnp.cumsum(cnt_pad, axis=0) - cnt_pad
    j = jnp.arange(n_row_tiles, dtype=jnp.int32)
    te = jnp.minimum(jnp.sum(((ends // tme)[None, :] <= j[:, None]).astype(jnp.int32), axis=1), n_experts - 1)
    n_used = ends[-1] // tme
    run_end = jnp.sum(jnp.where(te[:, None] == jnp.arange(n_experts)[None, :], (base + tot)[None, :], 0), axis=1)
    tv = jnp.where(j < n_used, jnp.clip(run_end - j * tme, 0, tme), 0)
    xi = jnp.minimum(j, n_used - 1)
    tail_start = base + tot
    tail_n = jnp.concatenate([(tot_t - tot) // SLOT_ALIGN, n_used[None]])
    flat = lambda a: a.reshape(-1).astype(jnp.int32)
    return (flat(cnt_pad), flat(tile_off), flat(dst_row), flat(tail_start), flat(tail_n),
            flat(te), flat(tv), flat(xi))


def kernel(x_prompt, x_sample, cache_k, cache_v, state_ret, page_table, norm1_g, w_in, ret_norm_g, w_out,
           norm2_g, router_w, router_b, w_gate_up, b_gate_up, w_down, b_down, final_norm_g):
    bp, lp, d_model = x_prompt.shape
    bs, ls, _ = x_sample.shape
    depth = w_in.shape[0]
    assert depth == 1, "single-layer step"
    n_pages = page_table.shape[1]
    page_size, mh, hd = cache_k.shape[2], cache_k.shape[3], cache_k.shape[4]
    rh, dk, dv = state_ret.shape[2], state_ret.shape[3], state_ret.shape[4]
    d_ret = rh * dk
    d_moba = mh * hd
    n_experts = router_w.shape[2]
    d_ff = w_down.shape[2]
    past_len = n_pages * page_size
    layer = 0

    w = w_in[layer]
    w_all = w.astype(BF16)
    g1 = norm1_g[layer][None, :]
    wo = w_out[layer].astype(BF16)
    rng = ret_norm_g[layer][None, :]
    n2 = norm2_g[layer][None, :]
    rwT = router_w[layer].T.astype(BF16)
    rb = router_b[layer][:, None]
    wgu = w_gate_up[layer]
    bgu_perm = b_gate_up[layer].reshape(n_experts, -1, LANES, 2).transpose(0, 1, 3, 2).reshape(n_experts, 1, 2 * d_ff)
    wd = w_down[layer]
    bd = b_down[layer][:, None, :]
    fg = final_norm_g[None, :]

    tm = _token_tile(lp)
    pos_p = jnp.arange(lp, dtype=jnp.int32)
    cos_p, sin_p = _rope_tables(pos_p, dk)
    xp = x_prompt.reshape(bp * lp, d_model)
    assert dk == dv
    ret, mq, mv, kT, vT, s_p = _inproj_prompt(
        xp, g1, w_all, cos_p, sin_p, rng,
        batch=bp, seq=lp, d_ret=d_ret, d_moba=d_moba, n_ret_heads=rh, tm=tm)
    ckT = cache_k[layer].transpose(0, 2, 3, 1)
    cvT = cache_v[layer].transpose(0, 2, 3, 1)
    moba_o, ksum = _moba_prompt(mq, mv, kT, batch=bp, seq=lp, n_heads=mh, hd=hd, page_table=page_table, cacheT=ckT)
    n_sample = bs * ls
    assert n_sample <= tm, "sample group must fit one token tile"
    npt = bp * lp // tm
    n_tiles = npt + 1
    bufs = _merge(xp, ret, moba_o, wo, n2, rwT, rb, None,
                  tm=tm, n_tiles_total=n_tiles, tile0=0, n_valid=tm, n_clear_tiles=1)

    pos_s = past_len + jnp.arange(ls, dtype=jnp.int32)
    cos_s, sin_s = _rope_tables(jnp.tile(pos_s, bs), dk)
    xs = x_sample.reshape(bs * ls, d_model)
    rq_s, rk_s, rv_s, rg_s, mq_s, mk_s, mv_s = _inproj_sample(
        xs, g1, w_all, cos_s, sin_s, d_ret=d_ret, d_moba=d_moba, n_ret_heads=rh)
    r3 = lambda a: a.reshape(bs, ls, a.shape[1])
    ret_s, s_s = _ret_sample(r3(rq_s), r3(rk_s), r3(rv_s), r3(rg_s), rng, state_ret[layer],
                             n_heads=rh, dk=dk, dv=dv)
    if ksum is None:
        idx_pad = _moba_gate(page_table, r3(mq_s), ckT, n_heads=mh, hd=hd, ls=ls, pages_per_step=min(32, n_pages))
    else:
        idx_pad = _moba_gate_top(ksum, r3(mq_s), n_heads=mh, hd=hd, n_past=n_pages // (MOBA_BLOCK // page_size))
    idx = idx_pad[:, :, :MOBA_TOPK].reshape(-1)
    cols = lambda a: a.reshape(bs, ls, mh, hd).transpose(0, 2, 3, 1)
    o_cols = _moba_sample(page_table, idx, cols(mq_s), cols(mk_s), cols(mv_s), ckT, cvT,
                          n_heads=mh, hd=hd, ls=ls)
    moba_o_s = o_cols.transpose(0, 3, 1, 2).reshape(bs * ls, d_moba).astype(BF16)
    padt = lambda a: jnp.pad(a, ((0, tm - n_sample), (0, 0)))
    hres_all, xn_all, slot, wts, cnt = _merge(
        padt(xs), padt(ret_s.reshape(n_sample, d_ret).astype(BF16)), padt(moba_o_s), wo, n2, rwT, rb, bufs,
        tm=tm, n_tiles_total=n_tiles, tile0=npt, n_valid=n_sample)

    tme = 512
    n_slots = -(-(tm * TOP_K + n_experts * (SLOT_ALIGN - 1)) // 256) * 256
    n_assign = TOP_K * (bp * lp + n_sample)
    n_row_tiles = -(-(n_assign + (SLOT_ALIGN - 1) * n_experts * n_tiles + n_experts * (tme - 1)) // tme)
    cp, to, dr, ts, tn, te, tv, xi = _route_tables(cnt[:, :, 0], tme=tme, n_row_tiles=n_row_tiles)
    x_sorted = _dispatch(cp, to, dr, ts, tn, xn_all, slot, tm=tm, tme=tme, n_experts=n_experts, n_slots=n_slots,
                         p_rows=n_row_tiles * tme)
    y_sorted = _experts(te, tv, xi, x_sorted, wgu, bgu_perm, wd, bd, tme=tme)
    y_p, y_s = _combine(cp, to, dr, hres_all, slot.T, wts.T, fg, y_sorted, tm=tm, n_experts=n_experts,
                        n_slots=n_slots, n_prompt_tiles=npt, n_sample=n_sample)

    y_prompt = y_p.reshape(bp, lp, d_model)
    y_sample = y_s.reshape(bs, ls, d_model)
    k_prompt = kT.reshape(bp, mh, hd, lp).transpose(0, 3, 1, 2)[None]
    v_prompt = vT.reshape(bp, mh, hd, lp).transpose(0, 3, 1, 2)[None]
    k_sample = mk_s.reshape(1, bs, ls, mh, hd)
    v_sample = mv_s.reshape(1, bs, ls, mh, hd)
    return (y_prompt, y_sample, k_prompt, v_prompt, s_p[None], k_sample, v_sample, s_s[None])
```

```python
import functools

import jax
import jax.numpy as jnp
from jax import lax
from jax.experimental import pallas as pl
from jax.experimental.pallas import tpu as pltpu

F32 = jnp.float32
BF16 = jnp.bfloat16

EPS = 1e-5
ROPE_BASE = 10000.0
RET_CHUNK = 128
MOBA_BLOCK = 256
MOBA_TOPK = 3
TOP_K = 4
SWIGLU_LIMIT = 7.0
SWIGLU_ALPHA = 1.702
NEG = -1e30
LANES = 128
SLOT_ALIGN = 16
RUN_PIECE = 64
WAIT_PIECE = 256
VMEM_LIMIT = 56 * 1024 * 1024

_NT = (((1,), (1,)), ((), ()))


def _cparams(*sem):
    return pltpu.CompilerParams(dimension_semantics=sem, vmem_limit_bytes=VMEM_LIMIT)


def _token_tile(seq):
    for tm in (512, 256, 128):
        if seq % tm == 0:
            return tm
    raise ValueError(f"sequence length {seq} must be a multiple of 128")


def _seqs_per_step(n_seqs):
    return 4 if n_seqs % 4 == 0 else 1


def _rms(x, g):
    return x * lax.rsqrt(jnp.mean(x * x, axis=-1, keepdims=True) + EPS) * g


def _rope_tables(pos, dk):
    half = dk // 2
    inv = ROPE_BASE ** (-jnp.arange(half, dtype=F32) / half)
    ang = pos.astype(F32)[:, None] * inv[None, :]
    c = jnp.cos(ang)
    s = jnp.sin(ang)
    return jnp.concatenate([c, c], axis=1), jnp.concatenate([-s, s], axis=1)


def _rotary(z, cos2, sin2, n_heads, dk):
    outs = []
    for h in range(n_heads):
        s = z[:, h * dk:(h + 1) * dk]
        outs.append(s * cos2 + pltpu.roll(s, dk // 2, 1) * sin2)
    return jnp.concatenate(outs, axis=1)


def _head_norm_gate(o, g, rg):
    mu = jnp.mean(o, axis=-1, keepdims=True)
    var = jnp.mean(jnp.square(o - mu), axis=-1, keepdims=True)
    return (o - mu) * lax.rsqrt(var + EPS) * g * (rg * jax.nn.sigmoid(rg))


def _inproj_prompt_kernel(x_ref, g_ref, w_ref, cos_ref, sin_ref, rng_ref,
                          din_ref, qdec_ref, kdec_ref, cdec_ref,
                          ret_ref, mq_ref, mv_ref, kT_ref, vT_ref, s_out_ref, s_scr, wkvT_ref,
                          *, d_ret, d_moba, n_ret_heads, chunk):
    i = pl.program_id(1)
    dk = d_ret // n_ret_heads
    tm = x_ref.shape[0]
    xn = _rms(x_ref[...], g_ref[...]).astype(BF16)
    cos2 = cos_ref[...]
    sin2 = sin_ref[...]

    @pl.when((pl.program_id(0) == 0) & (i == 0))
    def _():
        wkvT_ref[...] = w_ref[:, 4 * d_ret + d_moba:].astype(F32).T.astype(BF16)

    def proj(c0, n):
        return jnp.dot(xn, w_ref[:, c0:c0 + n], preferred_element_type=F32)

    mq_ref[...] = proj(4 * d_ret, d_moba).astype(BF16)
    mv_ref[...] = proj(4 * d_ret + 2 * d_moba, d_moba).astype(BF16)
    kT_ref[0] = lax.dot_general(wkvT_ref[0:d_moba, :], xn, _NT, preferred_element_type=F32)
    vT_ref[0] = lax.dot_general(wkvT_ref[d_moba:2 * d_moba, :], xn, _NT, preferred_element_type=F32)

    rq = _rotary(proj(0, d_ret), cos2, sin2, n_ret_heads, dk)
    rk = _rotary(proj(d_ret, d_ret), cos2, sin2, n_ret_heads, dk) * (dk ** -0.5)
    rv = proj(2 * d_ret, d_ret)
    rg = proj(3 * d_ret, d_ret)
    rng = rng_ref[...]

    @pl.when(i == 0)
    def _():
        s_scr[...] = jnp.zeros_like(s_scr)

    for h in range(n_ret_heads):
        hs = slice(h * dk, (h + 1) * dk)
        s = s_scr[h]
        for c in range(tm // chunk):
            rows = slice(c * chunk, (c + 1) * chunk)
            q = rq[rows, hs]
            k = rk[rows, hs]
            v = rv[rows, hs].astype(BF16)
            scores = lax.dot_general(q.astype(BF16), k.astype(BF16), _NT, preferred_element_type=F32) * din_ref[h]
            inner = jnp.dot(scores.astype(BF16), v, preferred_element_type=F32)
            cross = jnp.dot((q * qdec_ref[h]).astype(BF16), s.astype(BF16), preferred_element_type=F32)
            ret_ref[rows, hs] = _head_norm_gate(inner + cross, rng[:, hs], rg[rows, hs]).astype(BF16)
            kdT = (k * kdec_ref[h]).T.astype(BF16)
            s = cdec_ref[h] * s + jnp.dot(kdT, v, preferred_element_type=F32)
        s_scr[h] = s

    @pl.when(i == pl.num_programs(1) - 1)
    def _():
        s_out_ref[0] = s_scr[...]


def _inproj_prompt(x, g, w_all, cos2, sin2, rng, *, batch, seq, d_ret, d_moba, n_ret_heads, tm):
    d_model = x.shape[1]
    nt = seq // tm
    dk = d_ret // n_ret_heads
    chunk = RET_CHUNK
    decay_in, q_dec, k_dec, chunk_dec = _ret_tables(n_ret_heads, chunk)
    qdec_b = jnp.broadcast_to(q_dec.T[:, :, None], (n_ret_heads, chunk, dk))
    kdec_b = jnp.broadcast_to(k_dec.T[:, :, None], (n_ret_heads, chunk, dk))
    cdec_b = jnp.broadcast_to(chunk_dec[:, None, None], (n_ret_heads, dk, dk))
    tok_spec = lambda n: pl.BlockSpec((tm, n), lambda b, i: (b * nt + i, 0))
    full = lambda a: pl.BlockSpec(a.shape, lambda b, i: (0,) * a.ndim)
    t = batch * seq
    out_shape = (
        [jax.ShapeDtypeStruct((t, d_ret), BF16)]
        + [jax.ShapeDtypeStruct((t, d_moba), BF16)] * 2
        + [jax.ShapeDtypeStruct((batch, d_moba, seq), F32)] * 2
        + [jax.ShapeDtypeStruct((batch, n_ret_heads, dk, dk), F32)]
    )
    out_specs = (
        [tok_spec(d_ret)] + [tok_spec(d_moba)] * 2
        + [pl.BlockSpec((1, d_moba, tm), lambda b, i: (b, 0, i))] * 2
        + [pl.BlockSpec((1, n_ret_heads, dk, dk), lambda b, i: (b, 0, 0, 0))]
    )
    return pl.pallas_call(
        functools.partial(_inproj_prompt_kernel, d_ret=d_ret, d_moba=d_moba, n_ret_heads=n_ret_heads, chunk=chunk),
        grid=(batch, nt),
        in_specs=[tok_spec(d_model), full(g), full(w_all),
                  pl.BlockSpec((tm, dk), lambda b, i: (i, 0)), pl.BlockSpec((tm, dk), lambda b, i: (i, 0)),
                  full(rng), full(decay_in), full(qdec_b), full(kdec_b), full(cdec_b)],
        out_specs=out_specs,
        out_shape=out_shape,
        scratch_shapes=[pltpu.VMEM((n_ret_heads, dk, dk), F32), pltpu.VMEM((2 * d_moba, d_model), BF16)],
        compiler_params=_cparams("arbitrary", "arbitrary"),
        name="inproj_retention_prompt",
    )(x, g, w_all, cos2, sin2, rng, decay_in, qdec_b, kdec_b, cdec_b)


def _inproj_sample_kernel(x_ref, g_ref, w_ref, cos_ref, sin_ref,
                          rq_ref, rk_ref, rv_ref, rg_ref, mq_ref, mk_ref, mv_ref,
                          *, d_ret, d_moba, n_ret_heads):
    dk = d_ret // n_ret_heads
    xn = _rms(x_ref[...], g_ref[...]).astype(BF16)
    cos2 = cos_ref[...]
    sin2 = sin_ref[...]

    def proj(c0, n):
        return jnp.dot(xn, w_ref[:, c0:c0 + n], preferred_element_type=F32)

    rq_ref[...] = _rotary(proj(0, d_ret), cos2, sin2, n_ret_heads, dk)
    rk_ref[...] = _rotary(proj(d_ret, d_ret), cos2, sin2, n_ret_heads, dk) * (dk ** -0.5)
    rv_ref[...] = proj(2 * d_ret, d_ret)
    rg_ref[...] = proj(3 * d_ret, d_ret)
    mq_ref[...] = proj(4 * d_ret, d_moba)
    mk_ref[...] = proj(4 * d_ret + d_moba, d_moba)
    mv_ref[...] = proj(4 * d_ret + 2 * d_moba, d_moba)


def _inproj_sample(x, g, w_all, cos2, sin2, *, d_ret, d_moba, n_ret_heads):
    t = x.shape[0]
    args = (x, g, w_all, cos2, sin2)
    full = lambda a: pl.BlockSpec(a.shape, lambda i: (0,) * a.ndim)
    out_shape = [jax.ShapeDtypeStruct((t, d_ret), F32)] * 4 + [jax.ShapeDtypeStruct((t, d_moba), F32)] * 3
    return pl.pallas_call(
        functools.partial(_inproj_sample_kernel, d_ret=d_ret, d_moba=d_moba, n_ret_heads=n_ret_heads),
        grid=(1,),
        in_specs=[full(a) for a in args],
        out_specs=[pl.BlockSpec(s.shape, lambda i: (0, 0)) for s in out_shape],
        out_shape=out_shape,
        compiler_params=_cparams("arbitrary"),
        name="inproj_sample",
    )(*args)


def _ret_tables(n_heads, chunk):
    log_g = jnp.log1p(-jnp.exp2(-5.0 - jnp.arange(n_heads, dtype=F32)))
    n = jnp.arange(chunk, dtype=F32)
    rel = n[:, None] - n[None, :]
    causal = rel >= 0
    decay_in = jnp.where(causal[None], jnp.exp(jnp.where(causal, rel, 0.0)[None] * log_g[:, None, None]), 0.0)
    q_dec = jnp.exp((n[:, None] + 1.0) * log_g[None, :])
    k_dec = jnp.exp((chunk - 1.0 - n[:, None]) * log_g[None, :])
    chunk_dec = jnp.exp(chunk * log_g)
    return decay_in, q_dec, k_dec, chunk_dec


def _ret_sample_kernel(rq_ref, rk_ref, rv_ref, rg_ref, rng_ref, s_ref, din_ref, qdec_ref, kdec_ref, cdec_ref,
                       o_ref, s_out_ref, *, n_heads, dk, dv):
    row = lax.broadcasted_iota(jnp.int32, (dk, dk), 0)
    col = lax.broadcasted_iota(jnp.int32, (dk, dk), 1)
    eye = (row == col).astype(BF16)
    rng = rng_ref[...]
    for b, h in [(b, h) for b in range(rq_ref.shape[0]) for h in range(n_heads)]:
        q = rq_ref[b, :, h * dk:(h + 1) * dk]
        k = rk_ref[b, :, h * dk:(h + 1) * dk]
        v = rv_ref[b, :, h * dv:(h + 1) * dv].astype(BF16)
        s = s_ref[b, h]
        scores = lax.dot_general(q.astype(BF16), k.astype(BF16), _NT, preferred_element_type=F32) * din_ref[h]
        inner = jnp.dot(scores.astype(BF16), v, preferred_element_type=F32)
        cross = jnp.dot((q * qdec_ref[h]).astype(BF16), s.astype(BF16), preferred_element_type=F32)
        o_ref[b, :, h * dv:(h + 1) * dv] = _head_norm_gate(
            inner + cross, rng[:, h * dv:(h + 1) * dv], rg_ref[b, :, h * dv:(h + 1) * dv])
        kd = (k * kdec_ref[h]).astype(BF16)
        kdT = lax.dot_general(eye, kd, _NT, preferred_element_type=F32).astype(BF16)
        s_out_ref[b, h] = cdec_ref[h] * s + jnp.dot(kdT, v, preferred_element_type=F32)


def _ret_sample(rq, rk, rv, rg, rng, state, *, n_heads, dk, dv):
    bs, ls, _ = rq.shape
    decay_in, q_dec, k_dec, chunk_dec = _ret_tables(n_heads, ls)
    qdec_b = jnp.broadcast_to(q_dec.T[:, :, None], (n_heads, ls, dk))
    kdec_b = jnp.broadcast_to(k_dec.T[:, :, None], (n_heads, ls, dk))
    cdec_b = jnp.broadcast_to(chunk_dec[:, None, None], (n_heads, dk, dv))
    sb = _seqs_per_step(bs)
    tok = lambda n: pl.BlockSpec((sb, ls, n), lambda b: (b, 0, 0))
    full = lambda a: pl.BlockSpec(a.shape, lambda b: (0,) * a.ndim)
    st = pl.BlockSpec((sb, n_heads, dk, dv), lambda b: (b, 0, 0, 0))
    return pl.pallas_call(
        functools.partial(_ret_sample_kernel, n_heads=n_heads, dk=dk, dv=dv),
        grid=(bs // sb,),
        in_specs=[tok(n_heads * dk), tok(n_heads * dk), tok(n_heads * dv), tok(n_heads * dv), full(rng), st,
                  full(decay_in), full(qdec_b), full(kdec_b), full(cdec_b)],
        out_specs=[tok(n_heads * dv), st],
        out_shape=[jax.ShapeDtypeStruct((bs, ls, n_heads * dv), F32),
                   jax.ShapeDtypeStruct((bs, n_heads, dk, dv), F32)],
        compiler_params=_cparams("parallel"),
        name="retention_sample",
    )(rq, rk, rv, rg, rng, state, decay_in, qdec_b, kdec_b, cdec_b)


def _stream_block_sums(pt_ref, ck_ref, ksum_ref, pbuf, acc_scr, sems, *, pps, ppb, gps, n_pages):
    step = (pl.program_id(0) * pl.num_programs(1) + pl.program_id(1)) * pl.num_programs(2) + pl.program_id(2)
    n_steps = pl.num_programs(0) * pl.num_programs(1) * pl.num_programs(2)
    cur = step % 2

    def issue(u, buf):
        base = (u // gps) * n_pages + (u % gps) * pps
        for i in range(pps):
            pltpu.make_async_copy(ck_ref.at[pt_ref[base + i]], pbuf.at[buf, i], sems.at[buf]).start()

    @pl.when(step == 0)
    def _():
        issue(step, cur)

    @pl.when(step + 1 < n_steps)
    def _():
        issue(step + 1, 1 - cur)

    pltpu.make_async_copy(ck_ref.at[pl.ds(0, pps)], pbuf.at[cur], sems.at[cur]).wait()

    group = step % gps
    d_moba, page_size = acc_scr.shape[0], pbuf.shape[-1]
    lane = lax.broadcasted_iota(jnp.int32, acc_scr.shape, 1)
    acc = jnp.where(group == 0, 0.0, acc_scr[...])
    for u in range(pps // ppb):
        ks = pbuf[cur, u * ppb]
        for pg in range(1, ppb):
            ks = ks + pbuf[cur, u * ppb + pg]
        tok_sum = jnp.sum(ks.reshape(d_moba, page_size), axis=1, keepdims=True)
        acc = jnp.where(lane == group * (pps // ppb) + u, tok_sum, acc)
    acc_scr[...] = acc

    @pl.when(group == gps - 1)
    def _():
        ksum_ref[0] = acc


def _moba_prompt_kernel(*refs, blk, hd, n_blocks, topk, stream):
    if stream:
        (pt_ref, q_ref, kTd_ref, vd_ref, kT_ref, v_ref, eneg_ref, cbias_ref, ck_ref,
         o_ref, ksum_ref, km_scr, kaug_scr, pbuf, acc_scr, sems) = refs
        _stream_block_sums(pt_ref, ck_ref, ksum_ref, pbuf, acc_scr, sems, **stream)
    else:
        q_ref, kTd_ref, vd_ref, kT_ref, v_ref, eneg_ref, cbias_ref, o_ref, km_scr, kaug_scr = refs
    c = pl.program_id(2)
    width = 2 * hd
    scale = hd ** -0.5

    @pl.when(c == 0)
    def _():
        lane = lax.broadcasted_iota(jnp.int32, (width, LANES), 1)
        km = jnp.zeros((width, LANES), F32)
        for n in range(n_blocks):
            s = jnp.sum(kT_ref[0, :, n * blk:(n + 1) * blk], axis=1, keepdims=True) * (1.0 / blk)
            km = jnp.where(lane == n, s, km)
        km_scr[...] = km.T
        kaug_scr[0:width, :] = kT_ref[0].astype(BF16)
        kaug_scr[width:width + LANES, :] = eneg_ref[...]

    q = q_ref[...]
    lane_q = lax.broadcasted_iota(jnp.int32, (blk, width), 1)
    zq = jnp.zeros_like(q)
    qst = jnp.concatenate([jnp.where(lane_q < hd, q, zq), jnp.where(lane_q >= hd, q, zq)], axis=0)
    kmbT = km_scr[...].astype(BF16)
    kd = kTd_ref[0].astype(BF16)
    vd = vd_ref[...]

    nbp = -(-n_blocks // 8) * 8
    row_b = lax.broadcasted_iota(jnp.int32, (nbp, 2 * blk), 0)
    gate = lax.dot_general(kmbT, qst, _NT, preferred_element_type=F32)[0:nbp, :]
    gate = jnp.where(row_b < c, gate, NEG)
    cnt = jnp.zeros((nbp, 2 * blk), jnp.int32)
    for m in range(n_blocks):
        gm = gate[m:m + 1, :]
        beats = (gm > gate) | ((gm == gate) & (m < row_b))
        cnt = cnt + beats.astype(jnp.int32)
    sel = (row_b < c) & (cnt < topk)
    penT = jnp.where(sel | (row_b >= n_blocks), 0.0, 1.0)
    penT = jnp.concatenate([penT, jnp.zeros((LANES - nbp, 2 * blk), F32)], axis=0)
    pen = penT.T.astype(BF16)
    qs = qst * scale
    qa = jnp.concatenate([qs, pen], axis=1)
    s_d = jnp.dot(qs, kd, preferred_element_type=F32) + cbias_ref[...]
    m_d = jnp.max(s_d, axis=1, keepdims=True)

    def finish(n_wide):
        outs = []
        for j in range(2):
            rows = slice(j * blk, (j + 1) * blk)
            s_dj, m_dj = s_d[rows], m_d[rows]
            if n_wide:
                w = n_wide * blk
                s_w = jnp.dot(qa[rows], kaug_scr[:, :w], preferred_element_type=F32)
                mx = jnp.maximum(m_dj, jnp.max(s_w, axis=1, keepdims=True))
                p_w = jnp.exp(s_w - mx)
                p_d = jnp.exp(s_dj - mx)
                den = jnp.sum(p_d, axis=1, keepdims=True) + jnp.sum(p_w, axis=1, keepdims=True)
                acc = (jnp.dot(p_d.astype(BF16), vd, preferred_element_type=F32)
                       + jnp.dot(p_w.astype(BF16), v_ref[0:w, :], preferred_element_type=F32))
            else:
                p_d = jnp.exp(s_dj - m_dj)
                den = jnp.sum(p_d, axis=1, keepdims=True)
                acc = jnp.dot(p_d.astype(BF16), vd, preferred_element_type=F32)
            outs.append(acc / den)
        o_ref[...] = jnp.where(lane_q < hd, outs[0], outs[1]).astype(BF16)

    widths = list(range(1, n_blocks))
    lo = 1
    for n_wide in widths:
        @pl.when((c >= lo) & (c <= n_wide))
        def _(n_wide=n_wide):
            finish(n_wide)
        lo = n_wide + 1

    @pl.when(c == 0)
    def _():
        finish(0)


def _stream_plan(n_steps, page_table, cacheT):
    bs, n_pages = page_table.shape
    ppb = MOBA_BLOCK // cacheT.shape[-1]
    total = bs * n_pages
    if total % n_steps:
        return None
    pps = total // n_steps
    if pps % ppb or n_pages % pps or cacheT.shape[0] < pps:
        return None
    return dict(pps=pps, ppb=ppb, gps=n_pages // pps, n_pages=n_pages)


def _moba_prompt(mq, mv, kT, *, batch, seq, n_heads, hd, page_table=None, cacheT=None):
    blk = MOBA_BLOCK
    nb = seq // blk
    width = 2 * hd
    npair = n_heads // 2
    assert nb <= LANES and width == LANES
    eneg = jnp.where(jnp.arange(LANES)[:, None] == (jnp.arange(seq) // blk)[None, :], NEG, 0.0).astype(BF16)
    cbias = jnp.where(jnp.arange(blk)[None, :] <= (jnp.arange(2 * blk) % blk)[:, None], 0.0, -jnp.inf).astype(F32)
    stream = None if cacheT is None else _stream_plan(batch * npair * nb, page_table, cacheT)
    in_specs = [
        pl.BlockSpec((blk, width), lambda b, hp, c, *_: (b * nb + c, hp)),
        pl.BlockSpec((1, width, blk), lambda b, hp, c, *_: (b, hp, c)),
        pl.BlockSpec((blk, width), lambda b, hp, c, *_: (b * nb + c, hp)),
        pl.BlockSpec((1, width, seq), lambda b, hp, c, *_: (b, hp, 0)),
        pl.BlockSpec((seq, width), lambda b, hp, c, *_: (b, hp)),
        pl.BlockSpec((LANES, seq), lambda b, hp, c, *_: (0, 0)),
        pl.BlockSpec((2 * blk, blk), lambda b, hp, c, *_: (0, 0)),
    ]
    out_specs = [pl.BlockSpec((blk, width), lambda b, hp, c, *_: (b * nb + c, hp))]
    out_shape = [jax.ShapeDtypeStruct((batch * seq, n_heads * hd), BF16)]
    scratch = [pltpu.VMEM((width, LANES), F32), pltpu.VMEM((width + LANES, seq), BF16)]
    args = [mq, kT, mv, kT, mv, eneg, cbias]
    if stream:
        bs = page_table.shape[0]
        mh_all, hd_all, page_size = cacheT.shape[1:]
        gps = stream["gps"]
        in_specs.append(pl.BlockSpec(memory_space=pl.ANY))
        args.append(cacheT)
        out_specs.append(pl.BlockSpec((1, mh_all * hd_all, LANES),
                                      lambda b, hp, c, *_: (((b * npair + hp) * nb + c) // gps, 0, 0)))
        out_shape.append(jax.ShapeDtypeStruct((bs, mh_all * hd_all, LANES), F32))
        scratch += [pltpu.VMEM((2, stream["pps"], mh_all, hd_all, page_size), F32),
                    pltpu.VMEM((mh_all * hd_all, LANES), F32), pltpu.SemaphoreType.DMA((2,))]
        args = [page_table.reshape(-1)] + args
    grid_spec = pltpu.PrefetchScalarGridSpec(
        num_scalar_prefetch=1 if stream else 0,
        grid=(batch, npair, nb),
        in_specs=in_specs,
        out_specs=out_specs,
        scratch_shapes=scratch,
    )
    outs = pl.pallas_call(
        functools.partial(_moba_prompt_kernel, blk=blk, hd=hd, n_blocks=nb, topk=MOBA_TOPK, stream=stream),
        grid_spec=grid_spec,
        out_shape=out_shape,
        compiler_params=_cparams("arbitrary", "arbitrary", "arbitrary"),
        name="moba_prompt",
    )(*args)
    return (outs[0], outs[1]) if stream else (outs[0], None)


def _moba_gate_kernel(pt_ref, q_ref, *refs, pages_per_step, pages_per_block, n_heads, hd, ls, n_past, topk):
    del pt_ref
    pages = refs[:pages_per_step]
    idx_ref = refs[pages_per_step]
    km_scr = refs[pages_per_step + 1]
    j = pl.program_id(1)
    nj = pl.num_programs(1)
    d_moba = n_heads * hd
    page_size = pages[0].shape[-1]
    blocks_per_step = pages_per_step // pages_per_block

    @pl.when(j == 0)
    def _():
        km_scr[...] = jnp.zeros_like(km_scr)

    col = lax.broadcasted_iota(jnp.int32, (page_size, LANES), 1)
    acc = km_scr[...]
    for u in range(blocks_per_step):
        ks = pages[u * pages_per_block][0]
        for pg in range(1, pages_per_block):
            ks = ks + pages[u * pages_per_block + pg][0]
        ks = ks.reshape(d_moba, page_size).astype(BF16)
        onecol = (col == j * blocks_per_step + u).astype(BF16)
        acc = acc + jnp.dot(ks, onecol, preferred_element_type=F32)
    km_scr[...] = acc

    @pl.when(j == nj - 1)
    def _():
        idx_ref[0] = _gate_topk(km_scr[...], q_ref[0], block_len=pages_per_block * page_size,
                                n_heads=n_heads, hd=hd, n_past=n_past, topk=topk)


def _gate_topk(ksum, q, *, block_len, n_heads, hd, n_past, topk):
    ls, d_moba = q.shape
    kmean = (ksum * (1.0 / block_len)).astype(BF16)
    rows = ls * n_heads
    qrep = jnp.concatenate([jnp.broadcast_to(q[i:i + 1, :], (n_heads, d_moba)) for i in range(ls)], axis=0)
    r = lax.broadcasted_iota(jnp.int32, (rows, d_moba), 0)
    cc = lax.broadcasted_iota(jnp.int32, (rows, d_moba), 1)
    qbd = jnp.where((cc // hd) == (r % n_heads), qrep, 0.0).astype(BF16)
    gate = jnp.dot(qbd, kmean, preferred_element_type=F32)
    lane = lax.broadcasted_iota(jnp.int32, (rows, LANES), 1)
    work = jnp.where(lane < n_past, gate, jnp.where(lane == n_past, NEG, -jnp.inf))
    out = jnp.zeros((rows, LANES), jnp.int32)
    for t in range(topk):
        mx = jnp.max(work, axis=1, keepdims=True)
        it = jnp.min(jnp.where(work == mx, lane, LANES), axis=1, keepdims=True)
        out = jnp.where(lane == t, it, out)
        work = jnp.where(lane == it, -jnp.inf, work)
    return out


def _moba_gate_top_kernel(ksum_ref, q_ref, idx_ref, **kw):
    for b in range(q_ref.shape[0]):
        idx_ref[b] = _gate_topk(ksum_ref[b], q_ref[b], **kw)


def _moba_gate_top(ksum, mq_s, *, n_heads, hd, n_past):
    bs, ls, d_moba = mq_s.shape
    sb = _seqs_per_step(bs)
    return pl.pallas_call(
        functools.partial(_moba_gate_top_kernel, block_len=MOBA_BLOCK, n_heads=n_heads, hd=hd, n_past=n_past,
                          topk=MOBA_TOPK),
        grid=(bs // sb,),
        in_specs=[pl.BlockSpec((sb, d_moba, LANES), lambda b: (b, 0, 0)),
                  pl.BlockSpec((sb, ls, d_moba), lambda b: (b, 0, 0))],
        out_specs=pl.BlockSpec((sb, ls * n_heads, LANES), lambda b: (b, 0, 0)),
        out_shape=jax.ShapeDtypeStruct((bs, ls * n_heads, LANES), jnp.int32),
        compiler_params=_cparams("parallel"),
        name="moba_sample_gate_top",
    )(ksum, mq_s)


def _moba_gate(page_table, mq_s, cacheT, *, n_heads, hd, ls, pages_per_step):
    bs, n_pages = page_table.shape
    page_size = cacheT.shape[-1]
    ppb = MOBA_BLOCK // page_size
    n_past = n_pages // ppb
    steps = n_pages // pages_per_step
    d_moba = n_heads * hd

    def page_spec(i):
        return pl.BlockSpec((1, n_heads, hd, page_size),
                            lambda b, j, pt: (pt[b * n_pages + j * pages_per_step + i], 0, 0, 0))

    grid_spec = pltpu.PrefetchScalarGridSpec(
        num_scalar_prefetch=1,
        grid=(bs, steps),
        in_specs=[pl.BlockSpec((1, ls, d_moba), lambda b, j, pt: (b, 0, 0))]
        + [page_spec(i) for i in range(pages_per_step)],
        out_specs=pl.BlockSpec((1, ls * n_heads, LANES), lambda b, j, pt: (b, 0, 0)),
        scratch_shapes=[pltpu.VMEM((d_moba, LANES), F32)],
    )
    return pl.pallas_call(
        functools.partial(_moba_gate_kernel, pages_per_step=pages_per_step, pages_per_block=ppb,
                          n_heads=n_heads, hd=hd, ls=ls, n_past=n_past, topk=MOBA_TOPK),
        grid_spec=grid_spec,
        out_shape=jax.ShapeDtypeStruct((bs, ls * n_heads, LANES), jnp.int32),
        compiler_params=_cparams("parallel", "arbitrary"),
        name="moba_sample_gate",
    )(page_table.reshape(-1), mq_s, *([cacheT] * pages_per_step))


def _moba_sample_kernel(pt_ref, idx_ref, qc_ref, kc_ref, vc_ref, ck_ref, cv_ref, o_ref, kbuf, vbuf, sems,
                        *, ls, topk, ppb, n_heads, hd, n_past, n_pages):
    b = pl.program_id(0)
    nb = pl.num_programs(0)
    cur = b % 2
    per_head = ls * topk * ppb
    rows_per_head = per_head // n_heads
    scale = hd ** -0.5

    def slab_at(buf_ref, buf, h, r):
        return buf_ref.at[buf, h * rows_per_head + r // n_heads, r % n_heads]

    def issue(bb, buf):
        def per_head_body(h, c):
            for i in range(ls):
                for j in range(topk):
                    blk = jnp.minimum(idx_ref[((bb * ls + i) * n_heads + h) * topk + j], n_past - 1)
                    for pg in range(ppb):
                        page = pt_ref[bb * n_pages + blk * ppb + pg]
                        r = (i * topk + j) * ppb + pg
                        pltpu.make_async_copy(ck_ref.at[page, h], slab_at(kbuf, buf, h, r), sems.at[buf]).start()
                        pltpu.make_async_copy(cv_ref.at[page, h], slab_at(vbuf, buf, h, r), sems.at[buf]).start()
            return c

        lax.fori_loop(0, n_heads, per_head_body, 0)

    @pl.when(b == 0)
    def _():
        issue(b, cur)

    @pl.when(b + 1 < nb)
    def _():
        issue(b + 1, 1 - cur)

    pltpu.make_async_copy(ck_ref.at[pl.ds(0, per_head)], kbuf.at[cur], sems.at[cur]).wait()
    pltpu.make_async_copy(cv_ref.at[pl.ds(0, per_head)], vbuf.at[cur], sems.at[cur]).wait()

    lane_own = lax.broadcasted_iota(jnp.int32, (1, ls), 1)

    def head_body(h, c):
        qc = qc_ref[0, h]
        knew = kc_ref[0, h]
        vnew = vc_ref[0, h]
        for i in range(ls):
            qcol = qc[:, i:i + 1]
            s_list = []
            for j in range(topk):
                ok = idx_ref[((b * ls + i) * n_heads + h) * topk + j] < n_past
                for pg in range(ppb):
                    kt = slab_at(kbuf, cur, h, (i * topk + j) * ppb + pg)[...]
                    s = jnp.sum(kt * qcol, axis=0, keepdims=True) * scale
                    s_list.append(jnp.where(ok, s, -jnp.inf))
            s_own = jnp.sum(knew * qcol, axis=0, keepdims=True) * scale
            s_own = jnp.where(lane_own <= i, s_own, -jnp.inf)
            mx = jnp.max(s_own, axis=1, keepdims=True)
            for s in s_list:
                mx = jnp.maximum(mx, jnp.max(s, axis=1, keepdims=True))
            p_own = jnp.exp(s_own - mx)
            denom = jnp.sum(p_own, axis=1, keepdims=True)
            o = jnp.sum(vnew * p_own, axis=1, keepdims=True)
            pv = None
            for t, s in enumerate(s_list):
                p = jnp.exp(s - mx)
                denom = denom + jnp.sum(p, axis=1, keepdims=True)
                term = slab_at(vbuf, cur, h, i * topk * ppb + t)[...] * p
                pv = term if pv is None else pv + term
            o = o + jnp.sum(pv, axis=1, keepdims=True)
            o_ref[0, h, :, i:i + 1] = o / denom
        return c

    lax.fori_loop(0, n_heads, head_body, 0)


def _moba_sample(page_table, idx, q_cols, k_cols, v_cols, cacheT_k, cacheT_v, *, n_heads, hd, ls):
    bs, n_pages = page_table.shape
    page_size = cacheT_k.shape[-1]
    ppb = MOBA_BLOCK // page_size
    n_past = n_pages // ppb
    topk = MOBA_TOPK
    per_head = ls * topk * ppb
    assert per_head % n_heads == 0 and cacheT_k.shape[0] >= per_head
    col_spec = pl.BlockSpec((1, n_heads, hd, ls), lambda b, pt, ix: (b, 0, 0, 0))
    any_spec = pl.BlockSpec(memory_space=pl.ANY)
    slab_buf = pltpu.VMEM((2, per_head, n_heads, hd, page_size), F32)
    grid_spec = pltpu.PrefetchScalarGridSpec(
        num_scalar_prefetch=2,
        grid=(bs,),
        in_specs=[col_spec, col_spec, col_spec, any_spec, any_spec],
        out_specs=col_spec,
        scratch_shapes=[slab_buf, slab_buf, pltpu.SemaphoreType.DMA((2,))],
    )
    return pl.pallas_call(
        functools.partial(_moba_sample_kernel, ls=ls, topk=topk, ppb=ppb, n_heads=n_heads, hd=hd, n_past=n_past,
                          n_pages=n_pages),
        grid_spec=grid_spec,
        out_shape=jax.ShapeDtypeStruct((bs, n_heads, hd, ls), F32),
        compiler_params=_cparams("arbitrary"),
        name="moba_sample_attn",
    )(page_table.reshape(-1), idx, q_cols, k_cols, v_cols, cacheT_k, cacheT_v)


def _merge_kernel(*refs, n_in_tiles, **kw):
    i = pl.program_id(0)

    @pl.when(i < n_in_tiles)
    def _():
        _merge_body(*refs, **kw)

    @pl.when(i >= n_in_tiles)
    def _():
        hres_ref, xn_ref, slot_ref, w_ref, cnt_ref = refs[-5:]
        hres_ref[...] = jnp.zeros_like(hres_ref)
        xn_ref[...] = jnp.zeros_like(xn_ref)
        slot_ref[...] = jnp.full(slot_ref.shape, -1, jnp.int32)
        w_ref[...] = jnp.zeros_like(w_ref)
        cnt_ref[...] = jnp.zeros_like(cnt_ref)


def _merge_body(x_ref, ret_ref, mo_ref, wo_ref, n2_ref, rwT_ref, rb_ref, *refs, n_experts, top_k, n_valid):
    hres_ref, xn_ref, slot_ref, w_ref, cnt_ref = refs[-5:]
    tm = x_ref.shape[0]
    d_ret = ret_ref.shape[1]
    mix = (jnp.dot(ret_ref[...], wo_ref[0:d_ret, :], preferred_element_type=F32)
           + jnp.dot(mo_ref[...], wo_ref[d_ret:, :], preferred_element_type=F32))
    hres = x_ref[...] + mix
    hres_ref[...] = hres
    xn = _rms(hres, n2_ref[...]).astype(BF16)
    xn_ref[...] = xn
    logits = lax.dot_general(rwT_ref[...], xn, _NT, preferred_element_type=F32) + rb_ref[...]
    row = lax.broadcasted_iota(jnp.int32, logits.shape, 0)
    work = logits
    vals, hots = [], []
    for _ in range(top_k):
        mx = jnp.max(work, axis=0, keepdims=True)
        it = jnp.min(jnp.where(work == mx, row, n_experts), axis=0, keepdims=True)
        hot = row == it
        vals.append(mx)
        hots.append(hot)
        work = jnp.where(hot, -jnp.inf, work)
    exps = [jnp.exp(v - vals[0]) for v in vals]
    denom = exps[0]
    for e in exps[1:]:
        denom = denom + e

    valid = lax.broadcasted_iota(jnp.int32, (1, tm), 1) < n_valid
    mask = hots[0]
    for hot in hots[1:]:
        mask = mask | hot
    mask = mask & valid
    maskf = mask.astype(F32)
    r_i = lax.broadcasted_iota(jnp.int32, (tm, tm), 0)
    c_i = lax.broadcasted_iota(jnp.int32, (tm, tm), 1)
    rank = jnp.dot(maskf.astype(BF16), (r_i < c_i).astype(BF16), preferred_element_type=F32)
    cnt = jnp.sum(maskf, axis=1, keepdims=True).astype(jnp.int32)
    cnt_pad = ((cnt + (SLOT_ALIGN - 1)) // SLOT_ALIGN) * SLOT_ALIGN
    e_r = lax.broadcasted_iota(jnp.int32, (n_experts, n_experts), 0)
    e_c = lax.broadcasted_iota(jnp.int32, (n_experts, n_experts), 1)
    cpb = jnp.broadcast_to(cnt_pad.astype(F32), (n_experts, LANES)).astype(BF16)
    tile_off = jnp.dot((e_c < e_r).astype(BF16), cpb, preferred_element_type=F32)[:, 0:1]
    slot = tile_off + rank

    row8 = lax.broadcasted_iota(jnp.int32, (8, tm), 0)
    slot_out = jnp.full((8, tm), -1, jnp.int32)
    w_out = jnp.zeros((8, tm), F32)
    for k in range(top_k):
        sk = jnp.sum(jnp.where(hots[k], slot, 0.0), axis=0, keepdims=True).astype(jnp.int32)
        sk = jnp.where(valid, sk, -1)
        slot_out = jnp.where(row8 == k, sk, slot_out)
        w_out = jnp.where(row8 == k, exps[k] / denom, w_out)
    slot_ref[...] = slot_out
    w_ref[...] = w_out
    cnt_ref[0] = jnp.broadcast_to(cnt, (n_experts, LANES))


def _merge(x, ret, moba_o, w_out, norm2_g, rwT, router_b, prev, *,
           tm, n_tiles_total, tile0, n_valid, n_clear_tiles=0):
    t, d_model = x.shape
    n_experts = rwT.shape[0]
    t_pad = n_tiles_total * tm
    n_in = t // tm
    tok_in = lambda n: pl.BlockSpec((tm, n), lambda i: (jnp.minimum(i, n_in - 1), 0))
    tok = lambda n: pl.BlockSpec((tm, n), lambda i: (tile0 + i, 0))
    full = lambda a: pl.BlockSpec(a.shape, lambda i: (0,) * a.ndim)
    out_shape = [jax.ShapeDtypeStruct((t_pad, d_model), F32), jax.ShapeDtypeStruct((t_pad, d_model), BF16),
                 jax.ShapeDtypeStruct((8, t_pad), jnp.int32), jax.ShapeDtypeStruct((8, t_pad), F32),
                 jax.ShapeDtypeStruct((n_tiles_total, n_experts, LANES), jnp.int32)]
    out_specs = [tok(d_model), tok(d_model),
                 pl.BlockSpec((8, tm), lambda i: (0, tile0 + i)), pl.BlockSpec((8, tm), lambda i: (0, tile0 + i)),
                 pl.BlockSpec((1, n_experts, LANES), lambda i: (tile0 + i, 0, 0))]
    args = [x, ret, moba_o, w_out, norm2_g, rwT, router_b]
    in_specs = [tok_in(d_model), tok_in(ret.shape[1]), tok_in(moba_o.shape[1]),
                full(w_out), full(norm2_g), full(rwT), full(router_b)]
    aliases = {}
    if prev is not None:
        aliases = {len(args) + k: k for k in range(len(prev))}
        args += list(prev)
        in_specs += [pl.BlockSpec(memory_space=pl.ANY)] * len(prev)
    return pl.pallas_call(
        functools.partial(_merge_kernel, n_experts=n_experts, top_k=TOP_K, n_valid=n_valid, n_in_tiles=n_in),
        grid=(n_in + n_clear_tiles,),
        in_specs=in_specs,
        out_specs=out_specs,
        out_shape=out_shape,
        input_output_aliases=aliases,
        compiler_params=_cparams("parallel"),
        name="merge_router",
    )(*args)


def _run_copies(cp_ref, to_ref, dr_ref, t, n_experts, make_copy):
    def per_expert(e, carry):
        rows = cp_ref[t * n_experts + e]
        src0 = to_ref[t * n_experts + e]
        dst0 = dr_ref[t * n_experts + e]
        n_big = rows // RUN_PIECE
        done = n_big * RUN_PIECE

        def big(j, c):
            make_copy(pl.multiple_of(src0 + j * RUN_PIECE, SLOT_ALIGN),
                      pl.multiple_of(dst0 + j * RUN_PIECE, SLOT_ALIGN), RUN_PIECE).start()
            return c

        def small(j, c):
            make_copy(pl.multiple_of(src0 + done + j * SLOT_ALIGN, SLOT_ALIGN),
                      pl.multiple_of(dst0 + done + j * SLOT_ALIGN, SLOT_ALIGN), SLOT_ALIGN).start()
            return c

        lax.fori_loop(0, n_big, big, 0)
        lax.fori_loop(0, (rows - done) // SLOT_ALIGN, small, 0)
        return carry

    lax.fori_loop(0, n_experts, per_expert, 0)


def _run_rows(cp_ref, t, n_experts):
    return lax.fori_loop(0, n_experts, lambda e, c: c + cp_ref[t * n_experts + e], 0)


def _wait_rows(rows, make_copy):
    lax.fori_loop(0, rows // WAIT_PIECE, lambda j, c: (make_copy(0, 0, WAIT_PIECE).wait(), c)[1], 0)
    lax.fori_loop(0, (rows % WAIT_PIECE) // SLOT_ALIGN, lambda j, c: (make_copy(0, 0, SLOT_ALIGN).wait(), c)[1], 0)


def _dispatch_kernel(cp_ref, to_ref, dr_ref, ts_ref, tn_ref, xn_ref, slot_ref, xs_ref, slots_scr, zero_scr, sems,
                     *, n_experts, n_slots, chunk, top_k):
    t = pl.program_id(0)
    nt = pl.num_programs(0)
    cur = t % 2
    tm = xn_ref.shape[0]
    xn = xn_ref[...]
    sl = slot_ref[...]
    used = to_ref[t * n_experts + n_experts - 1] + cp_ref[t * n_experts + n_experts - 1]
    def sort_chunk(c):
        s_iota = c * chunk + lax.broadcasted_iota(jnp.int32, (chunk, tm), 0)
        pm = sl[0:1, :] == s_iota
        for k in range(1, top_k):
            pm = pm | (sl[k:k + 1, :] == s_iota)
        slots_scr[cur, c * chunk:(c + 1) * chunk, :] = jnp.dot(
            pm.astype(BF16), xn, preferred_element_type=F32).astype(BF16)

    for c in range(n_slots // chunk):
        if (c + 1) * chunk <= tm * top_k:
            sort_chunk(c)
        else:
            pl.when(c * chunk < used)(functools.partial(sort_chunk, c))

    def copy_from(buf):
        def make_copy(src, dst, rows):
            return pltpu.make_async_copy(slots_scr.at[buf, pl.ds(src, rows)], xs_ref.at[pl.ds(dst, rows)],
                                         sems.at[buf])
        return make_copy

    _run_copies(cp_ref, to_ref, dr_ref, t, n_experts, copy_from(cur))

    @pl.when(t > 0)
    def _():
        _wait_rows(_run_rows(cp_ref, t - 1, n_experts), copy_from(1 - cur))

    @pl.when(t == nt - 1)
    def _():
        zero_scr[...] = jnp.zeros_like(zero_scr)

        def zcopy(dst):
            return pltpu.make_async_copy(zero_scr.at[pl.ds(0, SLOT_ALIGN)], xs_ref.at[pl.ds(dst, SLOT_ALIGN)],
                                         sems.at[cur])

        def per_expert(e, tot):
            def one(j, c):
                zcopy(pl.multiple_of(ts_ref[e] + j * SLOT_ALIGN, SLOT_ALIGN)).start()
                return c
            lax.fori_loop(0, tn_ref[e], one, 0)
            return tot + tn_ref[e]

        n_tail = lax.fori_loop(0, n_experts, per_expert, 0)
        _wait_rows(_run_rows(cp_ref, t, n_experts) + n_tail * SLOT_ALIGN, copy_from(cur))

        tme = zero_scr.shape[0]

        def ztile(j):
            return pltpu.make_async_copy(zero_scr, xs_ref.at[pl.ds(pl.multiple_of(j * tme, tme), tme)], sems.at[cur])

        n_used = tn_ref[n_experts]
        n_all = xs_ref.shape[0] // tme
        lax.fori_loop(n_used, n_all, lambda j, c: (ztile(j).start(), c)[1], 0)
        lax.fori_loop(n_used, n_all, lambda j, c: (ztile(j).wait(), c)[1], 0)


def _dispatch(cp, to, dr, ts, tn, xn_all, slot, *, tm, tme, n_experts, n_slots, p_rows):
    t_pad, d_model = xn_all.shape
    grid_spec = pltpu.PrefetchScalarGridSpec(
        num_scalar_prefetch=5,
        grid=(t_pad // tm,),
        in_specs=[pl.BlockSpec((tm, d_model), lambda i, *_: (i, 0)),
                  pl.BlockSpec((8, tm), lambda i, *_: (0, i))],
        out_specs=pl.BlockSpec(memory_space=pl.ANY),
        scratch_shapes=[pltpu.VMEM((2, n_slots, d_model), BF16), pltpu.VMEM((tme, d_model), BF16),
                        pltpu.SemaphoreType.DMA((2,))],
    )
    return pl.pallas_call(
        functools.partial(_dispatch_kernel, n_experts=n_experts, n_slots=n_slots, chunk=256, top_k=TOP_K),
        grid_spec=grid_spec,
        out_shape=jax.ShapeDtypeStruct((p_rows, d_model), BF16),
        compiler_params=_cparams("arbitrary"),
        name="moe_dispatch",
    )(cp, to, dr, ts, tn, xn_all, slot)


def _expert_kernel(te_ref, tv_ref, xi_ref, x_ref, wgu_ref, bgu_ref, wd_ref, bd_ref, y_ref, wp_scr, wdb_scr, *, d_ff):
    j = pl.program_id(0)
    grp = 2 * LANES
    n_grp = 2 * d_ff // grp

    @pl.when(tv_ref[j] > 0)
    def _():
        first = (j == 0) | (te_ref[j] != te_ref[jnp.maximum(j - 1, 0)])

        @pl.when(first)
        def _():
            r = lax.broadcasted_iota(jnp.int32, (grp, grp), 0)
            c = lax.broadcasted_iota(jnp.int32, (grp, grp), 1)
            perm = (((c < LANES) & (r == 2 * c)) | ((c >= LANES) & (r == 2 * (c - LANES) + 1))).astype(BF16)
            for g in range(n_grp):
                wp_scr[:, g * grp:(g + 1) * grp] = jnp.dot(
                    wgu_ref[0, :, g * grp:(g + 1) * grp].astype(BF16), perm, preferred_element_type=F32).astype(BF16)
            wdb_scr[...] = wd_ref[0].astype(BF16)

        def ffn(rows):
            x = x_ref[0:rows, :]
            parts = []
            for g in range(n_grp):
                u = (jnp.dot(x, wp_scr[:, g * grp:(g + 1) * grp], preferred_element_type=F32)
                     + bgu_ref[0, :, g * grp:(g + 1) * grp])
                glu = jnp.minimum(u[:, :LANES], SWIGLU_LIMIT)
                lin = jnp.clip(u[:, LANES:], -SWIGLU_LIMIT, SWIGLU_LIMIT)
                parts.append((glu * jax.nn.sigmoid(SWIGLU_ALPHA * glu) * (lin + 1.0)).astype(BF16))
            a = jnp.concatenate(parts, axis=1)
            y_ref[0:rows, :] = (jnp.dot(a, wdb_scr[...], preferred_element_type=F32) + bd_ref[0]).astype(BF16)

        tme = x_ref.shape[0]

        @pl.when(tv_ref[j] > tme // 2)
        def _():
            ffn(tme)

        @pl.when(tv_ref[j] <= tme // 2)
        def _():
            ffn(tme // 2)
            y_ref[tme // 2:, :] = jnp.zeros((tme - tme // 2, y_ref.shape[1]), y_ref.dtype)

    @pl.when(tv_ref[j] == 0)
    def _():
        y_ref[...] = jnp.zeros_like(y_ref)


def _experts(te, tv, xi, x_sorted, wgu, bgu_perm, wd, bd, *, tme):
    p_rows, d_model = x_sorted.shape
    n_experts, _, d_ff2 = wgu.shape
    d_ff = d_ff2 // 2
    ex = lambda a: pl.BlockSpec((1,) + a.shape[1:], lambda j, te_, tv_, xi_: (te_[j],) + (0,) * (a.ndim - 1))
    grid_spec = pltpu.PrefetchScalarGridSpec(
        num_scalar_prefetch=3,
        grid=(p_rows // tme,),
        in_specs=[pl.BlockSpec((tme, d_model), lambda j, te_, tv_, xi_: (xi_[j], 0)),
                  ex(wgu), ex(bgu_perm), ex(wd), ex(bd)],
        out_specs=pl.BlockSpec((tme, d_model), lambda j, *_: (j, 0)),
        scratch_shapes=[pltpu.VMEM((d_model, d_ff2), BF16), pltpu.VMEM((d_ff, d_model), BF16)],
    )
    return pl.pallas_call(
        functools.partial(_expert_kernel, d_ff=d_ff),
        grid_spec=grid_spec,
        out_shape=jax.ShapeDtypeStruct((p_rows, d_model), BF16),
        compiler_params=_cparams("arbitrary"),
        name="moe_experts",
    )(te, tv, xi, x_sorted, wgu, bgu_perm, wd, bd)


def _combine_kernel(cp_ref, to_ref, dr_ref, hres_ref, slotT_ref, wT_ref, fg_ref, ys_ref, yp_ref, ysm_ref,
                    slots_scr, sems, *, n_experts, n_slots, chunk, top_k, n_prompt_tiles):
    t = pl.program_id(0)
    nt = pl.num_programs(0)
    cur = t % 2
    tm, d_model = hres_ref.shape

    def copy_into(buf):
        def make_copy(src, dst, rows):
            return pltpu.make_async_copy(ys_ref.at[pl.ds(dst, rows)], slots_scr.at[buf, pl.ds(src, rows)],
                                         sems.at[buf])
        return make_copy

    def fetch(tile, buf):
        _run_copies(cp_ref, to_ref, dr_ref, tile, n_experts, copy_into(buf))
        used = to_ref[tile * n_experts + n_experts - 1] + cp_ref[tile * n_experts + n_experts - 1]

        def zero_one(j, c):
            slots_scr[buf, pl.ds(pl.multiple_of(used + j * SLOT_ALIGN, SLOT_ALIGN), SLOT_ALIGN), :] = jnp.zeros(
                (SLOT_ALIGN, d_model), BF16)
            return c

        lax.fori_loop(0, (n_slots - used) // SLOT_ALIGN, zero_one, 0)

    @pl.when(t == 0)
    def _():
        fetch(t, cur)

    @pl.when(t + 1 < nt)
    def _():
        fetch(t + 1, 1 - cur)

    _wait_rows(_run_rows(cp_ref, t, n_experts), copy_into(cur))

    sl = slotT_ref[...]
    wt = wT_ref[...]
    acc = hres_ref[...]
    for c in range(n_slots // chunk):
        s_iota = c * chunk + lax.broadcasted_iota(jnp.int32, (tm, chunk), 1)
        pw = jnp.where(sl[:, 0:1] == s_iota, wt[:, 0:1], 0.0)
        for k in range(1, top_k):
            pw = pw + jnp.where(sl[:, k:k + 1] == s_iota, wt[:, k:k + 1], 0.0)
        acc = acc + jnp.dot(pw.astype(BF16), slots_scr[cur, c * chunk:(c + 1) * chunk, :],
                            preferred_element_type=F32)
    y = _rms(acc, fg_ref[...])

    @pl.when(t < n_prompt_tiles)
    def _():
        yp_ref[...] = y

    @pl.when(t >= n_prompt_tiles)
    def _():
        ysm_ref[...] = y[:ysm_ref.shape[0], :]


def _combine(cp, to, dr, hres_all, slotT, wT, fg, y_sorted, *, tm, n_experts, n_slots, n_prompt_tiles, n_sample):
    t_pad, d_model = hres_all.shape
    npt = n_prompt_tiles
    grid_spec = pltpu.PrefetchScalarGridSpec(
        num_scalar_prefetch=3,
        grid=(t_pad // tm,),
        in_specs=[pl.BlockSpec((tm, d_model), lambda i, *_: (i, 0)),
                  pl.BlockSpec((tm, 8), lambda i, *_: (i, 0)),
                  pl.BlockSpec((tm, 8), lambda i, *_: (i, 0)),
                  pl.BlockSpec(fg.shape, lambda i, *_: (0, 0)),
                  pl.BlockSpec(memory_space=pl.ANY)],
        out_specs=[pl.BlockSpec((tm, d_model), lambda i, *_: (jnp.minimum(i, npt - 1), 0)),
                   pl.BlockSpec((n_sample, d_model), lambda i, *_: (0, 0))],
        scratch_shapes=[pltpu.VMEM((2, n_slots, d_model), BF16), pltpu.SemaphoreType.DMA((2,))],
    )
    return pl.pallas_call(
        functools.partial(_combine_kernel, n_experts=n_experts, n_slots=n_slots, chunk=256, top_k=TOP_K,
                          n_prompt_tiles=npt),
        grid_spec=grid_spec,
        out_shape=[jax.ShapeDtypeStruct((npt * tm, d_model), F32), jax.ShapeDtypeStruct((n_sample, d_model), F32)],
        compiler_params=_cparams("arbitrary"),
        name="moe_combine",
    )(cp, to, dr, hres_all, slotT, wT, fg, y_sorted)


def _route_tables(cnt, *, tme, n_row_tiles):
    n_tiles, n_experts = cnt.shape
    cnt_pad = (cnt + (SLOT_ALIGN - 1)) // SLOT_ALIGN * SLOT_ALIGN
    tile_off = jnp.cumsum(cnt_pad, axis=1) - cnt_pad
    tot = jnp.sum(cnt_pad, axis=0)
    tot_t = (tot + (tme - 1)) // tme * tme
    ends = jnp.cumsum(tot_t)
    base = ends - tot_t
    dst_row = base[None, :] + jnp.cumsum(cnt_pad, axis=0) - cnt_pad
    j = jnp.arange(n_row_tiles, dtype=jnp.int32)
    te = jnp.minimum(jnp.sum(((ends // tme)[None, :] <= j[:, None]).astype(jnp.int32), axis=1), n_experts - 1)
    n_used = ends[-1] // tme
    run_end = jnp.sum(jnp.where(te[:, None] == jnp.arange(n_experts)[None, :], (base + tot)[None, :], 0), axis=1)
    tv = jnp.where(j < n_used, jnp.clip(run_end - j * tme, 0, tme), 0)
    xi = jnp.minimum(j, n_used - 1)
    tail_start = base + tot
    tail_n = jnp.concatenate([(tot_t - tot) // SLOT_ALIGN, n_used[None]])
    flat = lambda a: a.reshape(-1).astype(jnp.int32)
    return (flat(cnt_pad), flat(tile_off), flat(dst_row), flat(tail_start), flat(tail_n),
            flat(te), flat(tv), flat(xi))


def kernel(x_prompt, x_sample, cache_k, cache_v, state_ret, page_table, norm1_g, w_in, ret_norm_g, w_out,
           norm2_g, router_w, router_b, w_gate_up, b_gate_up, w_down, b_down, final_norm_g):
    bp, lp, d_model = x_prompt.shape
    bs, ls, _ = x_sample.shape
    depth = w_in.shape[0]
    assert depth == 1, "single-layer step"
    n_pages = page_table.shape[1]
    page_size, mh, hd = cache_k.shape[2], cache_k.shape[3], cache_k.shape[4]
    rh, dk, dv = state_ret.shape[2], state_ret.shape[3], state_ret.shape[4]
    d_ret = rh * dk
    d_moba = mh * hd
    n_experts = router_w.shape[2]
    d_ff = w_down.shape[2]
    past_len = n_pages * page_size
    layer = 0

    w = w_in[layer]
    w_all = w.astype(BF16)
    g1 = norm1_g[layer][None, :]
    wo = w_out[layer].astype(BF16)
    rng = ret_norm_g[layer][None, :]
    n2 = norm2_g[layer][None, :]
    rwT = router_w[layer].T.astype(BF16)
    rb = router_b[layer][:, None]
    wgu = w_gate_up[layer]
    bgu_perm = b_gate_up[layer].reshape(n_experts, -1, LANES, 2).transpose(0, 1, 3, 2).reshape(n_experts, 1, 2 * d_ff)
    wd = w_down[layer]
    bd = b_down[layer][:, None, :]
    fg = final_norm_g[None, :]

    tm = _token_tile(lp)
    pos_p = jnp.arange(lp, dtype=jnp.int32)
    cos_p, sin_p = _rope_tables(pos_p, dk)
    xp = x_prompt.reshape(bp * lp, d_model)
    assert dk == dv
    ret, mq, mv, kT, vT, s_p = _inproj_prompt(
        xp, g1, w_all, cos_p, sin_p, rng,
        batch=bp, seq=lp, d_ret=d_ret, d_moba=d_moba, n_ret_heads=rh, tm=tm)
    ckT = cache_k[layer].transpose(0, 2, 3, 1)
    cvT = cache_v[layer].transpose(0, 2, 3, 1)
    moba_o, ksum = _moba_prompt(mq, mv, kT, batch=bp, seq=lp, n_heads=mh, hd=hd, page_table=page_table, cacheT=ckT)
    n_sample = bs * ls
    assert n_sample <= tm, "sample group must fit one token tile"
    npt = bp * lp // tm
    n_tiles = npt + 1
    bufs = _merge(xp, ret, moba_o, wo, n2, rwT, rb, None,
                  tm=tm, n_tiles_total=n_tiles, tile0=0, n_valid=tm, n_clear_tiles=1)

    pos_s = past_len + jnp.arange(ls, dtype=jnp.int32)
    cos_s, sin_s = _rope_tables(jnp.tile(pos_s, bs), dk)
    xs = x_sample.reshape(bs * ls, d_model)
    rq_s, rk_s, rv_s, rg_s, mq_s, mk_s, mv_s = _inproj_sample(
        xs, g1, w_all, cos_s, sin_s, d_ret=d_ret, d_moba=d_moba, n_ret_heads=rh)
    r3 = lambda a: a.reshape(bs, ls, a.shape[1])
    ret_s, s_s = _ret_sample(r3(rq_s), r3(rk_s), r3(rv_s), r3(rg_s), rng, state_ret[layer],
                             n_heads=rh, dk=dk, dv=dv)
    if ksum is None:
        idx_pad = _moba_gate(page_table, r3(mq_s), ckT, n_heads=mh, hd=hd, ls=ls, pages_per_step=min(32, n_pages))
    else:
        idx_pad = _moba_gate_top(ksum, r3(mq_s), n_heads=mh, hd=hd, n_past=n_pages // (MOBA_BLOCK // page_size))
    idx = idx_pad[:, :, :MOBA_TOPK].reshape(-1)
    cols = lambda a: a.reshape(bs, ls, mh, hd).transpose(0, 2, 3, 1)
    o_cols = _moba_sample(page_table, idx, cols(mq_s), cols(mk_s), cols(mv_s), ckT, cvT,
                          n_heads=mh, hd=hd, ls=ls)
    moba_o_s = o_cols.transpose(0, 3, 1, 2).reshape(bs * ls, d_moba).astype(BF16)
    padt = lambda a: jnp.pad(a, ((0, tm - n_sample), (0, 0)))
    hres_all, xn_all, slot, wts, cnt = _merge(
        padt(xs), padt(ret_s.reshape(n_sample, d_ret).astype(BF16)), padt(moba_o_s), wo, n2, rwT, rb, bufs,
        tm=tm, n_tiles_total=n_tiles, tile0=npt, n_valid=n_sample)

    tme = 512
    n_slots = -(-(tm * TOP_K + n_experts * (SLOT_ALIGN - 1)) // 256) * 256
    n_assign = TOP_K * (bp * lp + n_sample)
    n_row_tiles = -(-(n_assign + (SLOT_ALIGN - 1) * n_experts * n_tiles + n_experts * (tme - 1)) // tme)
    cp, to, dr, ts, tn, te, tv, xi = _route_tables(cnt[:, :, 0], tme=tme, n_row_tiles=n_row_tiles)
    x_sorted = _dispatch(cp, to, dr, ts, tn, xn_all, slot, tm=tm, tme=tme, n_experts=n_experts, n_slots=n_slots,
                         p_rows=n_row_tiles * tme)
    y_sorted = _experts(te, tv, xi, x_sorted, wgu, bgu_perm, wd, bd, tme=tme)
    y_p, y_s = _combine(cp, to, dr, hres_all, slot.T, wts.T, fg, y_sorted, tm=tm, n_experts=n_experts,
                        n_slots=n_slots, n_prompt_tiles=npt, n_sample=n_sample)

    y_prompt = y_p.reshape(bp, lp, d_model)
    y_sample = y_s.reshape(bs, ls, d_model)
    k_prompt = kT.reshape(bp, mh, hd, lp).transpose(0, 3, 1, 2)[None]
    v_prompt = vT.reshape(bp, mh, hd, lp).transpose(0, 3, 1, 2)[None]
    k_sample = mk_s.reshape(1, bs, ls, mh, hd)
    v_sample = mv_s.reshape(1, bs, ls, mh, hd)
    return (y_prompt, y_sample, k_prompt, v_prompt, s_p[None], k_sample, v_sample, s_s[None])
```

```python
import functools

import jax
import jax.numpy as jnp
from jax import lax
from jax.experimental import pallas as pl
from jax.experimental.pallas import tpu as pltpu

F32 = jnp.float32
BF16 = jnp.bfloat16

EPS = 1e-5
ROPE_BASE = 10000.0
RET_CHUNK = 128
MOBA_BLOCK = 256
MOBA_TOPK = 3
TOP_K = 4
SWIGLU_LIMIT = 7.0
SWIGLU_ALPHA = 1.702
NEG = -1e30
LANES = 128
SLOT_ALIGN = 16
RUN_PIECE = 64
WAIT_PIECE = 256
VMEM_LIMIT = 56 * 1024 * 1024

_NT = (((1,), (1,)), ((), ()))


def _cparams(*sem):
    return pltpu.CompilerParams(dimension_semantics=sem, vmem_limit_bytes=VMEM_LIMIT)


def _token_tile(seq):
    for tm in (512, 256, 128):
        if seq % tm == 0:
            return tm
    raise ValueError(f"sequence length {seq} must be a multiple of 128")


def _seqs_per_step(n_seqs):
    return 4 if n_seqs % 4 == 0 else 1


def _rms(x, g):
    return x * lax.rsqrt(jnp.mean(x * x, axis=-1, keepdims=True) + EPS) * g


def _rope_tables(pos, dk):
    half = dk // 2
    inv = ROPE_BASE ** (-jnp.arange(half, dtype=F32) / half)
    ang = pos.astype(F32)[:, None] * inv[None, :]
    c = jnp.cos(ang)
    s = jnp.sin(ang)
    return jnp.concatenate([c, c], axis=1), jnp.concatenate([-s, s], axis=1)


def _rotary(z, cos2, sin2, n_heads, dk):
    outs = []
    for h in range(n_heads):
        s = z[:, h * dk:(h + 1) * dk]
        outs.append(s * cos2 + pltpu.roll(s, dk // 2, 1) * sin2)
    return jnp.concatenate(outs, axis=1)


def _head_norm_gate(o, g, rg):
    mu = jnp.mean(o, axis=-1, keepdims=True)
    var = jnp.mean(jnp.square(o - mu), axis=-1, keepdims=True)
    return (o - mu) * lax.rsqrt(var + EPS) * g * (rg * jax.nn.sigmoid(rg))


def _inproj_prompt_kernel(x_ref, g_ref, w_ref, cos_ref, sin_ref, rng_ref,
                          din_ref, qdec_ref, kdec_ref, cdec_ref,
                          ret_ref, mq_ref, mv_ref, kT_ref, vT_ref, s_out_ref, s_scr, wkvT_ref,
                          *, d_ret, d_moba, n_ret_heads, chunk):
    i = pl.program_id(1)
    dk = d_ret // n_ret_heads
    tm = x_ref.shape[0]
    xn = _rms(x_ref[...], g_ref[...]).astype(BF16)
    cos2 = cos_ref[...]
    sin2 = sin_ref[...]

    @pl.when((pl.program_id(0) == 0) & (i == 0))
    def _():
        wkvT_ref[...] = w_ref[:, 4 * d_ret + d_moba:].astype(F32).T.astype(BF16)

    def proj(c0, n):
        return jnp.dot(xn, w_ref[:, c0:c0 + n], preferred_element_type=F32)

    mq_ref[...] = proj(4 * d_ret, d_moba).astype(BF16)
    mv_ref[...] = proj(4 * d_ret + 2 * d_moba, d_moba).astype(BF16)
    kT_ref[0] = lax.dot_general(wkvT_ref[0:d_moba, :], xn, _NT, preferred_element_type=F32)
    vT_ref[0] = lax.dot_general(wkvT_ref[d_moba:2 * d_moba, :], xn, _NT, preferred_element_type=F32)

    rq = _rotary(proj(0, d_ret), cos2, sin2, n_ret_heads, dk)
    rk = _rotary(proj(d_ret, d_ret), cos2, sin2, n_ret_heads, dk) * (dk ** -0.5)
    rv = proj(2 * d_ret, d_ret)
    rg = proj(3 * d_ret, d_ret)
    rng = rng_ref[...]

    @pl.when(i == 0)
    def _():
        s_scr[...] = jnp.zeros_like(s_scr)

    for h in range(n_ret_heads):
        hs = slice(h * dk, (h + 1) * dk)
        s = s_scr[h]
        for c in range(tm // chunk):
            rows = slice(c * chunk, (c + 1) * chunk)
            q = rq[rows, hs]
            k = rk[rows, hs]
            v = rv[rows, hs].astype(BF16)
            scores = lax.dot_general(q.astype(BF16), k.astype(BF16), _NT, preferred_element_type=F32) * din_ref[h]
            inner = jnp.dot(scores.astype(BF16), v, preferred_element_type=F32)
            cross = jnp.dot((q * qdec_ref[h]).astype(BF16), s.astype(BF16), preferred_element_type=F32)
            ret_ref[rows, hs] = _head_norm_gate(inner + cross, rng[:, hs], rg[rows, hs]).astype(BF16)
            kdT = (k * kdec_ref[h]).T.astype(BF16)
            s = cdec_ref[h] * s + jnp.dot(kdT, v, preferred_element_type=F32)
        s_scr[h] = s

    @pl.when(i == pl.num_programs(1) - 1)
    def _():
        s_out_ref[0] = s_scr[...]


def _inproj_prompt(x, g, w_all, cos2, sin2, rng, *, batch, seq, d_ret, d_moba, n_ret_heads, tm):
    d_model = x.shape[1]
    nt = seq // tm
    dk = d_ret // n_ret_heads
    chunk = RET_CHUNK
    decay_in, q_dec, k_dec, chunk_dec = _ret_tables(n_ret_heads, chunk)
    qdec_b = jnp.broadcast_to(q_dec.T[:, :, None], (n_ret_heads, chunk, dk))
    kdec_b = jnp.broadcast_to(k_dec.T[:, :, None], (n_ret_heads, chunk, dk))
    cdec_b = jnp.broadcast_to(chunk_dec[:, None, None], (n_ret_heads, dk, dk))
    tok_spec = lambda n: pl.BlockSpec((tm, n), lambda b, i: (b * nt + i, 0))
    full = lambda a: pl.BlockSpec(a.shape, lambda b, i: (0,) * a.ndim)
    t = batch * seq
    out_shape = (
        [jax.ShapeDtypeStruct((t, d_ret), BF16)]
        + [jax.ShapeDtypeStruct((t, d_moba), BF16)] * 2
        + [jax.ShapeDtypeStruct((batch, d_moba, seq), F32)] * 2
        + [jax.ShapeDtypeStruct((batch, n_ret_heads, dk, dk), F32)]
    )
    out_specs = (
        [tok_spec(d_ret)] + [tok_spec(d_moba)] * 2
        + [pl.BlockSpec((1, d_moba, tm), lambda b, i: (b, 0, i))] * 2
        + [pl.BlockSpec((1, n_ret_heads, dk, dk), lambda b, i: (b, 0, 0, 0))]
    )
    return pl.pallas_call(
        functools.partial(_inproj_prompt_kernel, d_ret=d_ret, d_moba=d_moba, n_ret_heads=n_ret_heads, chunk=chunk),
        grid=(batch, nt),
        in_specs=[tok_spec(d_model), full(g), full(w_all),
                  pl.BlockSpec((tm, dk), lambda b, i: (i, 0)), pl.BlockSpec((tm, dk), lambda b, i: (i, 0)),
                  full(rng), full(decay_in), full(qdec_b), full(kdec_b), full(cdec_b)],
        out_specs=out_specs,
        out_shape=out_shape,
        scratch_shapes=[pltpu.VMEM((n_ret_heads, dk, dk), F32), pltpu.VMEM((2 * d_moba, d_model), BF16)],
        compiler_params=_cparams("arbitrary", "arbitrary"),
        name="inproj_retention_prompt",
    )(x, g, w_all, cos2, sin2, rng, decay_in, qdec_b, kdec_b, cdec_b)


def _inproj_sample_kernel(x_ref, g_ref, w_ref, cos_ref, sin_ref,
                          rq_ref, rk_ref, rv_ref, rg_ref, mq_ref, mk_ref, mv_ref,
                          *, d_ret, d_moba, n_ret_heads):
    dk = d_ret // n_ret_heads
    xn = _rms(x_ref[...], g_ref[...]).astype(BF16)
    cos2 = cos_ref[...]
    sin2 = sin_ref[...]

    def proj(c0, n):
        return jnp.dot(xn, w_ref[:, c0:c0 + n], preferred_element_type=F32)

    rq_ref[...] = _rotary(proj(0, d_ret), cos2, sin2, n_ret_heads, dk)
    rk_ref[...] = _rotary(proj(d_ret, d_ret), cos2, sin2, n_ret_heads, dk) * (dk ** -0.5)
    rv_ref[...] = proj(2 * d_ret, d_ret)
    rg_ref[...] = proj(3 * d_ret, d_ret)
    mq_ref[...] = proj(4 * d_ret, d_moba)
    mk_ref[...] = proj(4 * d_ret + d_moba, d_moba)
    mv_ref[...] = proj(4 * d_ret + 2 * d_moba, d_moba)


def _inproj_sample(x, g, w_all, cos2, sin2, *, d_ret, d_moba, n_ret_heads):
    t = x.shape[0]
    args = (x, g, w_all, cos2, sin2)
    full = lambda a: pl.BlockSpec(a.shape, lambda i: (0,) * a.ndim)
    out_shape = [jax.ShapeDtypeStruct((t, d_ret), F32)] * 4 + [jax.ShapeDtypeStruct((t, d_moba), F32)] * 3
    return pl.pallas_call(
        functools.partial(_inproj_sample_kernel, d_ret=d_ret, d_moba=d_moba, n_ret_heads=n_ret_heads),
        grid=(1,),
        in_specs=[full(a) for a in args],
        out_specs=[pl.BlockSpec(s.shape, lambda i: (0, 0)) for s in out_shape],
        out_shape=out_shape,
        compiler_params=_cparams("arbitrary"),
        name="inproj_sample",
    )(*args)


def _ret_tables(n_heads, chunk):
    log_g = jnp.log1p(-jnp.exp2(-5.0 - jnp.arange(n_heads, dtype=F32)))
    n = jnp.arange(chunk, dtype=F32)
    rel = n[:, None] - n[None, :]
    causal = rel >= 0
    decay_in = jnp.where(causal[None], jnp.exp(jnp.where(causal, rel, 0.0)[None] * log_g[:, None, None]), 0.0)
    q_dec = jnp.exp((n[:, None] + 1.0) * log_g[None, :])
    k_dec = jnp.exp((chunk - 1.0 - n[:, None]) * log_g[None, :])
    chunk_dec = jnp.exp(chunk * log_g)
    return decay_in, q_dec, k_dec, chunk_dec


def _ret_sample_kernel(rq_ref, rk_ref, rv_ref, rg_ref, rng_ref, s_ref, din_ref, qdec_ref, kdec_ref, cdec_ref,
                       o_ref, s_out_ref, *, n_heads, dk, dv):
    row = lax.broadcasted_iota(jnp.int32, (dk, dk), 0)
    col = lax.broadcasted_iota(jnp.int32, (dk, dk), 1)
    eye = (row == col).astype(BF16)
    rng = rng_ref[...]
    for b, h in [(b, h) for b in range(rq_ref.shape[0]) for h in range(n_heads)]:
        q = rq_ref[b, :, h * dk:(h + 1) * dk]
        k = rk_ref[b, :, h * dk:(h + 1) * dk]
        v = rv_ref[b, :, h * dv:(h + 1) * dv].astype(BF16)
        s = s_ref[b, h]
        scores = lax.dot_general(q.astype(BF16), k.astype(BF16), _NT, preferred_element_type=F32) * din_ref[h]
        inner = jnp.dot(scores.astype(BF16), v, preferred_element_type=F32)
        cross = jnp.dot((q * qdec_ref[h]).astype(BF16), s.astype(BF16), preferred_element_type=F32)
        o_ref[b, :, h * dv:(h + 1) * dv] = _head_norm_gate(
            inner + cross, rng[:, h * dv:(h + 1) * dv], rg_ref[b, :, h * dv:(h + 1) * dv])
        kd = (k * kdec_ref[h]).astype(BF16)
        kdT = lax.dot_general(eye, kd, _NT, preferred_element_type=F32).astype(BF16)
        s_out_ref[b, h] = cdec_ref[h] * s + jnp.dot(kdT, v, preferred_element_type=F32)


def _ret_sample(rq, rk, rv, rg, rng, state, *, n_heads, dk, dv):
    bs, ls, _ = rq.shape
    decay_in, q_dec, k_dec, chunk_dec = _ret_tables(n_heads, ls)
    qdec_b = jnp.broadcast_to(q_dec.T[:, :, None], (n_heads, ls, dk))
    kdec_b = jnp.broadcast_to(k_dec.T[:, :, None], (n_heads, ls, dk))
    cdec_b = jnp.broadcast_to(chunk_dec[:, None, None], (n_heads, dk, dv))
    sb = _seqs_per_step(bs)
    tok = lambda n: pl.BlockSpec((sb, ls, n), lambda b: (b, 0, 0))
    full = lambda a: pl.BlockSpec(a.shape, lambda b: (0,) * a.ndim)
    st = pl.BlockSpec((sb, n_heads, dk, dv), lambda b: (b, 0, 0, 0))
    return pl.pallas_call(
        functools.partial(_ret_sample_kernel, n_heads=n_heads, dk=dk, dv=dv),
        grid=(bs // sb,),
        in_specs=[tok(n_heads * dk), tok(n_heads * dk), tok(n_heads * dv), tok(n_heads * dv), full(rng), st,
                  full(decay_in), full(qdec_b), full(kdec_b), full(cdec_b)],
        out_specs=[tok(n_heads * dv), st],
        out_shape=[jax.ShapeDtypeStruct((bs, ls, n_heads * dv), F32),
                   jax.ShapeDtypeStruct((bs, n_heads, dk, dv), F32)],
        compiler_params=_cparams("parallel"),
        name="retention_sample",
    )(rq, rk, rv, rg, rng, state, decay_in, qdec_b, kdec_b, cdec_b)


def _stream_block_sums(pt_ref, ck_ref, ksum_ref, pbuf, acc_scr, sems, *, pps, ppb, gps, n_pages):
    step = (pl.program_id(0) * pl.num_programs(1) + pl.program_id(1)) * pl.num_programs(2) + pl.program_id(2)
    n_steps = pl.num_programs(0) * pl.num_programs(1) * pl.num_programs(2)
    cur = step % 2

    def issue(u, buf):
        base = (u // gps) * n_pages + (u % gps) * pps
        for i in range(pps):
            pltpu.make_async_copy(ck_ref.at[pt_ref[base + i]], pbuf.at[buf, i], sems.at[buf]).start()

    @pl.when(step == 0)
    def _():
        issue(step, cur)

    @pl.when(step + 1 < n_steps)
    def _():
        issue(step + 1, 1 - cur)

    pltpu.make_async_copy(ck_ref.at[pl.ds(0, pps)], pbuf.at[cur], sems.at[cur]).wait()

    group = step % gps
    d_moba, page_size = acc_scr.shape[0], pbuf.shape[-1]
    lane = lax.broadcasted_iota(jnp.int32, acc_scr.shape, 1)
    acc = jnp.where(group == 0, 0.0, acc_scr[...])
    for u in range(pps // ppb):
        ks = pbuf[cur, u * ppb]
        for pg in range(1, ppb):
            ks = ks + pbuf[cur, u * ppb + pg]
        n = group * (pps // ppb) + u
        ks = ks.reshape(d_moba, page_size)
        if u % 2:
            onecol = (lax.broadcasted_iota(jnp.int32, (page_size, LANES), 1) == n).astype(BF16)
            acc = acc + jnp.dot(ks.astype(BF16), onecol, preferred_element_type=F32)
        else:
            acc = jnp.where(lane == n, jnp.sum(ks, axis=1, keepdims=True), acc)
    acc_scr[...] = acc

    @pl.when(group == gps - 1)
    def _():
        ksum_ref[0] = acc


def _moba_prompt_kernel(*refs, blk, hd, n_blocks, topk, stream):
    if stream:
        (pt_ref, q_ref, kTd_ref, vd_ref, kT_ref, v_ref, eneg_ref, cbias_ref, ck_ref,
         o_ref, ksum_ref, km_scr, kaug_scr, pbuf, acc_scr, sems) = refs
        _stream_block_sums(pt_ref, ck_ref, ksum_ref, pbuf, acc_scr, sems, **stream)
    else:
        q_ref, kTd_ref, vd_ref, kT_ref, v_ref, eneg_ref, cbias_ref, o_ref, km_scr, kaug_scr = refs
    c = pl.program_id(2)
    width = 2 * hd
    scale = hd ** -0.5

    @pl.when(c == 0)
    def _():
        lane = lax.broadcasted_iota(jnp.int32, (width, LANES), 1)
        km = jnp.zeros((width, LANES), F32)
        for n in range(n_blocks):
            s = jnp.sum(kT_ref[0, :, n * blk:(n + 1) * blk], axis=1, keepdims=True) * (1.0 / blk)
            km = jnp.where(lane == n, s, km)
        km_scr[...] = km.T
        kaug_scr[0:width, :] = kT_ref[0].astype(BF16)
        kaug_scr[width:width + LANES, :] = eneg_ref[...]

    q = q_ref[...]
    lane_q = lax.broadcasted_iota(jnp.int32, (blk, width), 1)
    zq = jnp.zeros_like(q)
    qst = jnp.concatenate([jnp.where(lane_q < hd, q, zq), jnp.where(lane_q >= hd, q, zq)], axis=0)
    kmbT = km_scr[...].astype(BF16)
    kd = kTd_ref[0].astype(BF16)
    vd = vd_ref[...]

    nbp = -(-n_blocks // 8) * 8
    row_b = lax.broadcasted_iota(jnp.int32, (nbp, 2 * blk), 0)
    gate = lax.dot_general(kmbT, qst, _NT, preferred_element_type=F32)[0:nbp, :]
    gate = jnp.where(row_b < c, gate, NEG)
    cnt = jnp.zeros((nbp, 2 * blk), jnp.int32)
    for m in range(n_blocks):
        gm = gate[m:m + 1, :]
        beats = (gm > gate) | ((gm == gate) & (m < row_b))
        cnt = cnt + beats.astype(jnp.int32)
    sel = (row_b < c) & (cnt < topk)
    penT = jnp.where(sel | (row_b >= n_blocks), 0.0, 1.0)
    penT = jnp.concatenate([penT, jnp.zeros((LANES - nbp, 2 * blk), F32)], axis=0)
    pen = penT.T.astype(BF16)
    qs = qst * scale
    qa = jnp.concatenate([qs, pen], axis=1)
    s_d = jnp.dot(qs, kd, preferred_element_type=F32) + cbias_ref[...]
    m_d = jnp.max(s_d, axis=1, keepdims=True)

    def finish(n_wide):
        outs = []
        for j in range(2):
            rows = slice(j * blk, (j + 1) * blk)
            s_dj, m_dj = s_d[rows], m_d[rows]
            if n_wide:
                w = n_wide * blk
                s_w = jnp.dot(qa[rows], kaug_scr[:, :w], preferred_element_type=F32)
                mx = jnp.maximum(m_dj, jnp.max(s_w, axis=1, keepdims=True))
                p_w = jnp.exp(s_w - mx)
                p_d = jnp.exp(s_dj - mx)
                den = jnp.sum(p_d, axis=1, keepdims=True) + jnp.sum(p_w, axis=1, keepdims=True)
                acc = (jnp.dot(p_d.astype(BF16), vd, preferred_element_type=F32)
                       + jnp.dot(p_w.astype(BF16), v_ref[0:w, :], preferred_element_type=F32))
            else:
                p_d = jnp.exp(s_dj - m_dj)
                den = jnp.sum(p_d, axis=1, keepdims=True)
                acc = jnp.dot(p_d.astype(BF16), vd, preferred_element_type=F32)
            outs.append(acc / den)
        o_ref[...] = jnp.where(lane_q < hd, outs[0], outs[1]).astype(BF16)

    widths = list(range(1, n_blocks))
    lo = 1
    for n_wide in widths:
        @pl.when((c >= lo) & (c <= n_wide))
        def _(n_wide=n_wide):
            finish(n_wide)
        lo = n_wide + 1

    @pl.when(c == 0)
    def _():
        finish(0)


def _stream_plan(n_steps, page_table, cacheT):
    bs, n_pages = page_table.shape
    ppb = MOBA_BLOCK // cacheT.shape[-1]
    total = bs * n_pages
    if total % n_steps:
        return None
    pps = total // n_steps
    if pps % ppb or n_pages % pps or cacheT.shape[0] < pps:
        return None
    return dict(pps=pps, ppb=ppb, gps=n_pages // pps, n_pages=n_pages)


def _moba_prompt(mq, mv, kT, *, batch, seq, n_heads, hd, page_table=None, cacheT=None):
    blk = MOBA_BLOCK
    nb = seq // blk
    width = 2 * hd
    npair = n_heads // 2
    assert nb <= LANES and width == LANES
    eneg = jnp.where(jnp.arange(LANES)[:, None] == (jnp.arange(seq) // blk)[None, :], NEG, 0.0).astype(BF16)
    cbias = jnp.where(jnp.arange(blk)[None, :] <= (jnp.arange(2 * blk) % blk)[:, None], 0.0, -jnp.inf).astype(F32)
    stream = None if cacheT is None else _stream_plan(batch * npair * nb, page_table, cacheT)
    in_specs = [
        pl.BlockSpec((blk, width), lambda b, hp, c, *_: (b * nb + c, hp)),
        pl.BlockSpec((1, width, blk), lambda b, hp, c, *_: (b, hp, c)),
        pl.BlockSpec((blk, width), lambda b, hp, c, *_: (b * nb + c, hp)),
        pl.BlockSpec((1, width, seq), lambda b, hp, c, *_: (b, hp, 0)),
        pl.BlockSpec((seq, width), lambda b, hp, c, *_: (b, hp)),
        pl.BlockSpec((LANES, seq), lambda b, hp, c, *_: (0, 0)),
        pl.BlockSpec((2 * blk, blk), lambda b, hp, c, *_: (0, 0)),
    ]
    out_specs = [pl.BlockSpec((blk, width), lambda b, hp, c, *_: (b * nb + c, hp))]
    out_shape = [jax.ShapeDtypeStruct((batch * seq, n_heads * hd), BF16)]
    scratch = [pltpu.VMEM((width, LANES), F32), pltpu.VMEM((width + LANES, seq), BF16)]
    args = [mq, kT, mv, kT, mv, eneg, cbias]
    if stream:
        bs = page_table.shape[0]
        mh_all, hd_all, page_size = cacheT.shape[1:]
        gps = stream["gps"]
        in_specs.append(pl.BlockSpec(memory_space=pl.ANY))
        args.append(cacheT)
        out_specs.append(pl.BlockSpec((1, mh_all * hd_all, LANES),
                                      lambda b, hp, c, *_: (((b * npair + hp) * nb + c) // gps, 0, 0)))
        out_shape.append(jax.ShapeDtypeStruct((bs, mh_all * hd_all, LANES), F32))
        scratch += [pltpu.VMEM((2, stream["pps"], mh_all, hd_all, page_size), F32),
                    pltpu.VMEM((mh_all * hd_all, LANES), F32), pltpu.SemaphoreType.DMA((2,))]
        args = [page_table.reshape(-1)] + args
    grid_spec = pltpu.PrefetchScalarGridSpec(
        num_scalar_prefetch=1 if stream else 0,
        grid=(batch, npair, nb),
        in_specs=in_specs,
        out_specs=out_specs,
        scratch_shapes=scratch,
    )
    outs = pl.pallas_call(
        functools.partial(_moba_prompt_kernel, blk=blk, hd=hd, n_blocks=nb, topk=MOBA_TOPK, stream=stream),
        grid_spec=grid_spec,
        out_shape=out_shape,
        compiler_params=_cparams("arbitrary", "arbitrary", "arbitrary"),
        name="moba_prompt",
    )(*args)
    return (outs[0], outs[1]) if stream else (outs[0], None)


def _moba_gate_kernel(pt_ref, q_ref, *refs, pages_per_step, pages_per_block, n_heads, hd, ls, n_past, topk):
    del pt_ref
    pages = refs[:pages_per_step]
    idx_ref = refs[pages_per_step]
    km_scr = refs[pages_per_step + 1]
    j = pl.program_id(1)
    nj = pl.num_programs(1)
    d_moba = n_heads * hd
    page_size = pages[0].shape[-1]
    blocks_per_step = pages_per_step // pages_per_block

    @pl.when(j == 0)
    def _():
        km_scr[...] = jnp.zeros_like(km_scr)

    col = lax.broadcasted_iota(jnp.int32, (page_size, LANES), 1)
    acc = km_scr[...]
    for u in range(blocks_per_step):
        ks = pages[u * pages_per_block][0]
        for pg in range(1, pages_per_block):
            ks = ks + pages[u * pages_per_block + pg][0]
        ks = ks.reshape(d_moba, page_size).astype(BF16)
        onecol = (col == j * blocks_per_step + u).astype(BF16)
        acc = acc + jnp.dot(ks, onecol, preferred_element_type=F32)
    km_scr[...] = acc

    @pl.when(j == nj - 1)
    def _():
        idx_ref[0] = _gate_topk(km_scr[...], q_ref[0], block_len=pages_per_block * page_size,
                                n_heads=n_heads, hd=hd, n_past=n_past, topk=topk)


def _gate_topk(ksum, q, *, block_len, n_heads, hd, n_past, topk):
    ls, d_moba = q.shape
    kmean = (ksum * (1.0 / block_len)).astype(BF16)
    rows = ls * n_heads
    qrep = jnp.concatenate([jnp.broadcast_to(q[i:i + 1, :], (n_heads, d_moba)) for i in range(ls)], axis=0)
    r = lax.broadcasted_iota(jnp.int32, (rows, d_moba), 0)
    cc = lax.broadcasted_iota(jnp.int32, (rows, d_moba), 1)
    qbd = jnp.where((cc // hd) == (r % n_heads), qrep, 0.0).astype(BF16)
    gate = jnp.dot(qbd, kmean, preferred_element_type=F32)
    lane = lax.broadcasted_iota(jnp.int32, (rows, LANES), 1)
    work = jnp.where(lane < n_past, gate, jnp.where(lane == n_past, NEG, -jnp.inf))
    out = jnp.zeros((rows, LANES), jnp.int32)
    for t in range(topk):
        mx = jnp.max(work, axis=1, keepdims=True)
        it = jnp.min(jnp.where(work == mx, lane, LANES), axis=1, keepdims=True)
        out = jnp.where(lane == t, it, out)
        work = jnp.where(lane == it, -jnp.inf, work)
    return out


def _moba_gate_top_kernel(ksum_ref, q_ref, idx_ref, **kw):
    for b in range(q_ref.shape[0]):
        idx_ref[b] = _gate_topk(ksum_ref[b], q_ref[b], **kw)


def _moba_gate_top(ksum, mq_s, *, n_heads, hd, n_past):
    bs, ls, d_moba = mq_s.shape
    sb = _seqs_per_step(bs)
    return pl.pallas_call(
        functools.partial(_moba_gate_top_kernel, block_len=MOBA_BLOCK, n_heads=n_heads, hd=hd, n_past=n_past,
                          topk=MOBA_TOPK),
        grid=(bs // sb,),
        in_specs=[pl.BlockSpec((sb, d_moba, LANES), lambda b: (b, 0, 0)),
                  pl.BlockSpec((sb, ls, d_moba), lambda b: (b, 0, 0))],
        out_specs=pl.BlockSpec((sb, ls * n_heads, LANES), lambda b: (b, 0, 0)),
        out_shape=jax.ShapeDtypeStruct((bs, ls * n_heads, LANES), jnp.int32),
        compiler_params=_cparams("parallel"),
        name="moba_sample_gate_top",
    )(ksum, mq_s)


def _moba_gate(page_table, mq_s, cacheT, *, n_heads, hd, ls, pages_per_step):
    bs, n_pages = page_table.shape
    page_size = cacheT.shape[-1]
    ppb = MOBA_BLOCK // page_size
    n_past = n_pages // ppb
    steps = n_pages // pages_per_step
    d_moba = n_heads * hd

    def page_spec(i):
        return pl.BlockSpec((1, n_heads, hd, page_size),
                            lambda b, j, pt: (pt[b * n_pages + j * pages_per_step + i], 0, 0, 0))

    grid_spec = pltpu.PrefetchScalarGridSpec(
        num_scalar_prefetch=1,
        grid=(bs, steps),
        in_specs=[pl.BlockSpec((1, ls, d_moba), lambda b, j, pt: (b, 0, 0))]
        + [page_spec(i) for i in range(pages_per_step)],
        out_specs=pl.BlockSpec((1, ls * n_heads, LANES), lambda b, j, pt: (b, 0, 0)),
        scratch_shapes=[pltpu.VMEM((d_moba, LANES), F32)],
    )
    return pl.pallas_call(
        functools.partial(_moba_gate_kernel, pages_per_step=pages_per_step, pages_per_block=ppb,
                          n_heads=n_heads, hd=hd, ls=ls, n_past=n_past, topk=MOBA_TOPK),
        grid_spec=grid_spec,
        out_shape=jax.ShapeDtypeStruct((bs, ls * n_heads, LANES), jnp.int32),
        compiler_params=_cparams("parallel", "arbitrary"),
        name="moba_sample_gate",
    )(page_table.reshape(-1), mq_s, *([cacheT] * pages_per_step))


def _moba_sample_kernel(pt_ref, idx_ref, qc_ref, kc_ref, vc_ref, ck_ref, cv_ref, o_ref, kbuf, vbuf, sems,
                        *, ls, topk, ppb, n_heads, hd, n_past, n_pages):
    b = pl.program_id(0)
    nb = pl.num_programs(0)
    cur = b % 2
    per_head = ls * topk * ppb
    rows_per_head = per_head // n_heads
    scale = hd ** -0.5

    def slab_at(buf_ref, buf, h, r):
        return buf_ref.at[buf, h * rows_per_head + r // n_heads, r % n_heads]

    def issue(bb, buf):
        def per_head_body(h, c):
            for i in range(ls):
                for j in range(topk):
                    blk = jnp.minimum(idx_ref[((bb * ls + i) * n_heads + h) * topk + j], n_past - 1)
                    for pg in range(ppb):
                        page = pt_ref[bb * n_pages + blk * ppb + pg]
                        r = (i * topk + j) * ppb + pg
                        pltpu.make_async_copy(ck_ref.at[page, h], slab_at(kbuf, buf, h, r), sems.at[buf]).start()
                        pltpu.make_async_copy(cv_ref.at[page, h], slab_at(vbuf, buf, h, r), sems.at[buf]).start()
            return c

        lax.fori_loop(0, n_heads, per_head_body, 0)

    @pl.when(b == 0)
    def _():
        issue(b, cur)

    @pl.when(b + 1 < nb)
    def _():
        issue(b + 1, 1 - cur)

    pltpu.make_async_copy(ck_ref.at[pl.ds(0, per_head)], kbuf.at[cur], sems.at[cur]).wait()
    pltpu.make_async_copy(cv_ref.at[pl.ds(0, per_head)], vbuf.at[cur], sems.at[cur]).wait()

    lane_own = lax.broadcasted_iota(jnp.int32, (1, ls), 1)

    def head_body(h, c):
        qc = qc_ref[0, h]
        knew = kc_ref[0, h]
        vnew = vc_ref[0, h]
        for i in range(ls):
            qcol = qc[:, i:i + 1]
            s_list = []
            for j in range(topk):
                ok = idx_ref[((b * ls + i) * n_heads + h) * topk + j] < n_past
                for pg in range(ppb):
                    kt = slab_at(kbuf, cur, h, (i * topk + j) * ppb + pg)[...]
                    s = jnp.sum(kt * qcol, axis=0, keepdims=True) * scale
                    s_list.append(jnp.where(ok, s, -jnp.inf))
            s_own = jnp.sum(knew * qcol, axis=0, keepdims=True) * scale
            s_own = jnp.where(lane_own <= i, s_own, -jnp.inf)
            mx = jnp.max(s_own, axis=1, keepdims=True)
            for s in s_list:
                mx = jnp.maximum(mx, jnp.max(s, axis=1, keepdims=True))
            p_own = jnp.exp(s_own - mx)
            denom = jnp.sum(p_own, axis=1, keepdims=True)
            o = jnp.sum(vnew * p_own, axis=1, keepdims=True)
            pv = None
            for t, s in enumerate(s_list):
                p = jnp.exp(s - mx)
                denom = denom + jnp.sum(p, axis=1, keepdims=True)
                term = slab_at(vbuf, cur, h, i * topk * ppb + t)[...] * p
                pv = term if pv is None else pv + term
            o = o + jnp.sum(pv, axis=1, keepdims=True)
            o_ref[0, h, :, i:i + 1] = o / denom
        return c

    lax.fori_loop(0, n_heads, head_body, 0)


def _moba_sample(page_table, idx, q_cols, k_cols, v_cols, cacheT_k, cacheT_v, *, n_heads, hd, ls):
    bs, n_pages = page_table.shape
    page_size = cacheT_k.shape[-1]
    ppb = MOBA_BLOCK // page_size
    n_past = n_pages // ppb
    topk = MOBA_TOPK
    per_head = ls * topk * ppb
    assert per_head % n_heads == 0 and cacheT_k.shape[0] >= per_head
    col_spec = pl.BlockSpec((1, n_heads, hd, ls), lambda b, pt, ix: (b, 0, 0, 0))
    any_spec = pl.BlockSpec(memory_space=pl.ANY)
    slab_buf = pltpu.VMEM((2, per_head, n_heads, hd, page_size), F32)
    grid_spec = pltpu.PrefetchScalarGridSpec(
        num_scalar_prefetch=2,
        grid=(bs,),
        in_specs=[col_spec, col_spec, col_spec, any_spec, any_spec],
        out_specs=col_spec,
        scratch_shapes=[slab_buf, slab_buf, pltpu.SemaphoreType.DMA((2,))],
    )
    return pl.pallas_call(
        functools.partial(_moba_sample_kernel, ls=ls, topk=topk, ppb=ppb, n_heads=n_heads, hd=hd, n_past=n_past,
                          n_pages=n_pages),
        grid_spec=grid_spec,
        out_shape=jax.ShapeDtypeStruct((bs, n_heads, hd, ls), F32),
        compiler_params=_cparams("arbitrary"),
        name="moba_sample_attn",
    )(page_table.reshape(-1), idx, q_cols, k_cols, v_cols, cacheT_k, cacheT_v)


def _merge_kernel(*refs, n_in_tiles, **kw):
    i = pl.program_id(0)

    @pl.when(i < n_in_tiles)
    def _():
        _merge_body(*refs, **kw)

    @pl.when(i >= n_in_tiles)
    def _():
        hres_ref, xn_ref, slot_ref, w_ref, cnt_ref = refs[-5:]
        hres_ref[...] = jnp.zeros_like(hres_ref)
        xn_ref[...] = jnp.zeros_like(xn_ref)
        slot_ref[...] = jnp.full(slot_ref.shape, -1, jnp.int32)
        w_ref[...] = jnp.zeros_like(w_ref)
        cnt_ref[...] = jnp.zeros_like(cnt_ref)


def _merge_body(x_ref, ret_ref, mo_ref, wo_ref, n2_ref, rwT_ref, rb_ref, *refs, n_experts, top_k, n_valid):
    hres_ref, xn_ref, slot_ref, w_ref, cnt_ref = refs[-5:]
    tm = x_ref.shape[0]
    d_ret = ret_ref.shape[1]
    mix = (jnp.dot(ret_ref[...], wo_ref[0:d_ret, :], preferred_element_type=F32)
           + jnp.dot(mo_ref[...], wo_ref[d_ret:, :], preferred_element_type=F32))
    hres = x_ref[...] + mix
    hres_ref[...] = hres
    xn = _rms(hres, n2_ref[...]).astype(BF16)
    xn_ref[...] = xn
    logits = lax.dot_general(rwT_ref[...], xn, _NT, preferred_element_type=F32) + rb_ref[...]
    row = lax.broadcasted_iota(jnp.int32, logits.shape, 0)
    work = logits
    vals, hots = [], []
    for _ in range(top_k):
        mx = jnp.max(work, axis=0, keepdims=True)
        it = jnp.min(jnp.where(work == mx, row, n_experts), axis=0, keepdims=True)
        hot = row == it
        vals.append(mx)
        hots.append(hot)
        work = jnp.where(hot, -jnp.inf, work)
    exps = [jnp.exp(v - vals[0]) for v in vals]
    denom = exps[0]
    for e in exps[1:]:
        denom = denom + e

    valid = lax.broadcasted_iota(jnp.int32, (1, tm), 1) < n_valid
    mask = hots[0]
    for hot in hots[1:]:
        mask = mask | hot
    mask = mask & valid
    maskf = mask.astype(F32)
    r_i = lax.broadcasted_iota(jnp.int32, (tm, tm), 0)
    c_i = lax.broadcasted_iota(jnp.int32, (tm, tm), 1)
    rank = jnp.dot(maskf.astype(BF16), (r_i < c_i).astype(BF16), preferred_element_type=F32)
    cnt = jnp.sum(maskf, axis=1, keepdims=True).astype(jnp.int32)
    cnt_pad = ((cnt + (SLOT_ALIGN - 1)) // SLOT_ALIGN) * SLOT_ALIGN
    e_r = lax.broadcasted_iota(jnp.int32, (n_experts, n_experts), 0)
    e_c = lax.broadcasted_iota(jnp.int32, (n_experts, n_experts), 1)
    cpb = jnp.broadcast_to(cnt_pad.astype(F32), (n_experts, LANES)).astype(BF16)
    tile_off = jnp.dot((e_c < e_r).astype(BF16), cpb, preferred_element_type=F32)[:, 0:1]
    slot = tile_off + rank

    row8 = lax.broadcasted_iota(jnp.int32, (8, tm), 0)
    slot_out = jnp.full((8, tm), -1, jnp.int32)
    w_out = jnp.zeros((8, tm), F32)
    for k in range(top_k):
        sk = jnp.sum(jnp.where(hots[k], slot, 0.0), axis=0, keepdims=True).astype(jnp.int32)
        sk = jnp.where(valid, sk, -1)
        slot_out = jnp.where(row8 == k, sk, slot_out)
        w_out = jnp.where(row8 == k, exps[k] / denom, w_out)
    slot_ref[...] = slot_out
    w_ref[...] = w_out
    cnt_ref[0] = jnp.broadcast_to(cnt, (n_experts, LANES))


def _merge(x, ret, moba_o, w_out, norm2_g, rwT, router_b, prev, *,
           tm, n_tiles_total, tile0, n_valid, n_clear_tiles=0):
    t, d_model = x.shape
    n_experts = rwT.shape[0]
    t_pad = n_tiles_total * tm
    n_in = t // tm
    tok_in = lambda n: pl.BlockSpec((tm, n), lambda i: (jnp.minimum(i, n_in - 1), 0))
    tok = lambda n: pl.BlockSpec((tm, n), lambda i: (tile0 + i, 0))
    full = lambda a: pl.BlockSpec(a.shape, lambda i: (0,) * a.ndim)
    out_shape = [jax.ShapeDtypeStruct((t_pad, d_model), F32), jax.ShapeDtypeStruct((t_pad, d_model), BF16),
                 jax.ShapeDtypeStruct((8, t_pad), jnp.int32), jax.ShapeDtypeStruct((8, t_pad), F32),
                 jax.ShapeDtypeStruct((n_tiles_total, n_experts, LANES), jnp.int32)]
    out_specs = [tok(d_model), tok(d_model),
                 pl.BlockSpec((8, tm), lambda i: (0, tile0 + i)), pl.BlockSpec((8, tm), lambda i: (0, tile0 + i)),
                 pl.BlockSpec((1, n_experts, LANES), lambda i: (tile0 + i, 0, 0))]
    args = [x, ret, moba_o, w_out, norm2_g, rwT, router_b]
    in_specs = [tok_in(d_model), tok_in(ret.shape[1]), tok_in(moba_o.shape[1]),
                full(w_out), full(norm2_g), full(rwT), full(router_b)]
    aliases = {}
    if prev is not None:
        aliases = {len(args) + k: k for k in range(len(prev))}
        args += list(prev)
        in_specs += [pl.BlockSpec(memory_space=pl.ANY)] * len(prev)
    return pl.pallas_call(
        functools.partial(_merge_kernel, n_experts=n_experts, top_k=TOP_K, n_valid=n_valid, n_in_tiles=n_in),
        grid=(n_in + n_clear_tiles,),
        in_specs=in_specs,
        out_specs=out_specs,
        out_shape=out_shape,
        input_output_aliases=aliases,
        compiler_params=_cparams("parallel"),
        name="merge_router",
    )(*args)


def _run_copies(cp_ref, to_ref, dr_ref, t, n_experts, make_copy):
    def per_expert(e, carry):
        rows = cp_ref[t * n_experts + e]
        src0 = to_ref[t * n_experts + e]
        dst0 = dr_ref[t * n_experts + e]
        n_big = rows // RUN_PIECE
        done = n_big * RUN_PIECE

        def big(j, c):
            make_copy(pl.multiple_of(src0 + j * RUN_PIECE, SLOT_ALIGN),
                      pl.multiple_of(dst0 + j * RUN_PIECE, SLOT_ALIGN), RUN_PIECE).start()
            return c

        def small(j, c):
            make_copy(pl.multiple_of(src0 + done + j * SLOT_ALIGN, SLOT_ALIGN),
                      pl.multiple_of(dst0 + done + j * SLOT_ALIGN, SLOT_ALIGN), SLOT_ALIGN).start()
            return c

        lax.fori_loop(0, n_big, big, 0)
        lax.fori_loop(0, (rows - done) // SLOT_ALIGN, small, 0)
        return carry

    lax.fori_loop(0, n_experts, per_expert, 0)


def _run_rows(cp_ref, t, n_experts):
    return lax.fori_loop(0, n_experts, lambda e, c: c + cp_ref[t * n_experts + e], 0)


def _wait_rows(rows, make_copy):
    lax.fori_loop(0, rows // WAIT_PIECE, lambda j, c: (make_copy(0, 0, WAIT_PIECE).wait(), c)[1], 0)
    lax.fori_loop(0, (rows % WAIT_PIECE) // SLOT_ALIGN, lambda j, c: (make_copy(0, 0, SLOT_ALIGN).wait(), c)[1], 0)


def _dispatch_kernel(cp_ref, to_ref, dr_ref, ts_ref, tn_ref, xn_ref, slot_ref, xs_ref, slots_scr, zero_scr, sems,
                     *, n_experts, n_slots, chunk, top_k):
    t = pl.program_id(0)
    nt = pl.num_programs(0)
    cur = t % 2
    tm = xn_ref.shape[0]
    xn = xn_ref[...]
    sl = slot_ref[...]
    used = to_ref[t * n_experts + n_experts - 1] + cp_ref[t * n_experts + n_experts - 1]
    def sort_chunk(c):
        s_iota = c * chunk + lax.broadcasted_iota(jnp.int32, (chunk, tm), 0)
        pm = sl[0:1, :] == s_iota
        for k in range(1, top_k):
            pm = pm | (sl[k:k + 1, :] == s_iota)
        slots_scr[cur, c * chunk:(c + 1) * chunk, :] = jnp.dot(
            pm.astype(BF16), xn, preferred_element_type=F32).astype(BF16)

    for c in range(n_slots // chunk):
        if (c + 1) * chunk <= tm * top_k:
            sort_chunk(c)
        else:
            pl.when(c * chunk < used)(functools.partial(sort_chunk, c))

    def copy_from(buf):
        def make_copy(src, dst, rows):
            return pltpu.make_async_copy(slots_scr.at[buf, pl.ds(src, rows)], xs_ref.at[pl.ds(dst, rows)],
                                         sems.at[buf])
        return make_copy

    _run_copies(cp_ref, to_ref, dr_ref, t, n_experts, copy_from(cur))

    @pl.when(t > 0)
    def _():
        _wait_rows(_run_rows(cp_ref, t - 1, n_experts), copy_from(1 - cur))

    @pl.when(t == nt - 1)
    def _():
        zero_scr[...] = jnp.zeros_like(zero_scr)

        def zcopy(dst):
            return pltpu.make_async_copy(zero_scr.at[pl.ds(0, SLOT_ALIGN)], xs_ref.at[pl.ds(dst, SLOT_ALIGN)],
                                         sems.at[cur])

        def per_expert(e, tot):
            def one(j, c):
                zcopy(pl.multiple_of(ts_ref[e] + j * SLOT_ALIGN, SLOT_ALIGN)).start()
                return c
            lax.fori_loop(0, tn_ref[e], one, 0)
            return tot + tn_ref[e]

        n_tail = lax.fori_loop(0, n_experts, per_expert, 0)
        _wait_rows(_run_rows(cp_ref, t, n_experts) + n_tail * SLOT_ALIGN, copy_from(cur))

        tme = zero_scr.shape[0]

        def ztile(j):
            return pltpu.make_async_copy(zero_scr, xs_ref.at[pl.ds(pl.multiple_of(j * tme, tme), tme)], sems.at[cur])

        n_used = tn_ref[n_experts]
        n_all = xs_ref.shape[0] // tme
        lax.fori_loop(n_used, n_all, lambda j, c: (ztile(j).start(), c)[1], 0)
        lax.fori_loop(n_used, n_all, lambda j, c: (ztile(j).wait(), c)[1], 0)


def _dispatch(cp, to, dr, ts, tn, xn_all, slot, *, tm, tme, n_experts, n_slots, p_rows):
    t_pad, d_model = xn_all.shape
    grid_spec = pltpu.PrefetchScalarGridSpec(
        num_scalar_prefetch=5,
        grid=(t_pad // tm,),
        in_specs=[pl.BlockSpec((tm, d_model), lambda i, *_: (i, 0)),
                  pl.BlockSpec((8, tm), lambda i, *_: (0, i))],
        out_specs=pl.BlockSpec(memory_space=pl.ANY),
        scratch_shapes=[pltpu.VMEM((2, n_slots, d_model), BF16), pltpu.VMEM((tme, d_model), BF16),
                        pltpu.SemaphoreType.DMA((2,))],
    )
    return pl.pallas_call(
        functools.partial(_dispatch_kernel, n_experts=n_experts, n_slots=n_slots, chunk=256, top_k=TOP_K),
        grid_spec=grid_spec,
        out_shape=jax.ShapeDtypeStruct((p_rows, d_model), BF16),
        compiler_params=_cparams("arbitrary"),
        name="moe_dispatch",
    )(cp, to, dr, ts, tn, xn_all, slot)


def _expert_kernel(te_ref, tv_ref, xi_ref, x_ref, wgu_ref, bgu_ref, wd_ref, bd_ref, y_ref, wp_scr, wdb_scr, *, d_ff):
    j = pl.program_id(0)
    grp = 2 * LANES
    n_grp = 2 * d_ff // grp

    @pl.when(tv_ref[j] > 0)
    def _():
        first = (j == 0) | (te_ref[j] != te_ref[jnp.maximum(j - 1, 0)])

        @pl.when(first)
        def _():
            r = lax.broadcasted_iota(jnp.int32, (grp, grp), 0)
            c = lax.broadcasted_iota(jnp.int32, (grp, grp), 1)
            perm = (((c < LANES) & (r == 2 * c)) | ((c >= LANES) & (r == 2 * (c - LANES) + 1))).astype(BF16)
            for g in range(n_grp):
                wp_scr[:, g * grp:(g + 1) * grp] = jnp.dot(
                    wgu_ref[0, :, g * grp:(g + 1) * grp].astype(BF16), perm, preferred_element_type=F32).astype(BF16)
            wdb_scr[...] = wd_ref[0].astype(BF16)

        def ffn(rows):
            x = x_ref[0:rows, :]
            parts = []
            for g in range(n_grp):
                u = (jnp.dot(x, wp_scr[:, g * grp:(g + 1) * grp], preferred_element_type=F32)
                     + bgu_ref[0, :, g * grp:(g + 1) * grp])
                glu = jnp.minimum(u[:, :LANES], SWIGLU_LIMIT)
                lin = jnp.clip(u[:, LANES:], -SWIGLU_LIMIT, SWIGLU_LIMIT)
                parts.append((glu * jax.nn.sigmoid(SWIGLU_ALPHA * glu) * (lin + 1.0)).astype(BF16))
            a = jnp.concatenate(parts, axis=1)
            y_ref[0:rows, :] = (jnp.dot(a, wdb_scr[...], preferred_element_type=F32) + bd_ref[0]).astype(BF16)

        tme = x_ref.shape[0]

        @pl.when(tv_ref[j] > tme // 2)
        def _():
            ffn(tme)

        @pl.when(tv_ref[j] <= tme // 2)
        def _():
            ffn(tme // 2)
            y_ref[tme // 2:, :] = jnp.zeros((tme - tme // 2, y_ref.shape[1]), y_ref.dtype)

    @pl.when(tv_ref[j] == 0)
    def _():
        y_ref[...] = jnp.zeros_like(y_ref)


def _experts(te, tv, xi, x_sorted, wgu, bgu_perm, wd, bd, *, tme):
    p_rows, d_model = x_sorted.shape
    n_experts, _, d_ff2 = wgu.shape
    d_ff = d_ff2 // 2
    ex = lambda a: pl.BlockSpec((1,) + a.shape[1:], lambda j, te_, tv_, xi_: (te_[j],) + (0,) * (a.ndim - 1))
    grid_spec = pltpu.PrefetchScalarGridSpec(
        num_scalar_prefetch=3,
        grid=(p_rows // tme,),
        in_specs=[pl.BlockSpec((tme, d_model), lambda j, te_, tv_, xi_: (xi_[j], 0)),
                  ex(wgu), ex(bgu_perm), ex(wd), ex(bd)],
        out_specs=pl.BlockSpec((tme, d_model), lambda j, *_: (j, 0)),
        scratch_shapes=[pltpu.VMEM((d_model, d_ff2), BF16), pltpu.VMEM((d_ff, d_model), BF16)],
    )
    return pl.pallas_call(
        functools.partial(_expert_kernel, d_ff=d_ff),
        grid_spec=grid_spec,
        out_shape=jax.ShapeDtypeStruct((p_rows, d_model), BF16),
        compiler_params=_cparams("arbitrary"),
        name="moe_experts",
    )(te, tv, xi, x_sorted, wgu, bgu_perm, wd, bd)


def _combine_kernel(cp_ref, to_ref, dr_ref, hres_ref, slotT_ref, wT_ref, fg_ref, ys_ref, yp_ref, ysm_ref,
                    slots_scr, sems, *, n_experts, n_slots, chunk, top_k, n_prompt_tiles):
    t = pl.program_id(0)
    nt = pl.num_programs(0)
    cur = t % 2
    tm, d_model = hres_ref.shape

    def copy_into(buf):
        def make_copy(src, dst, rows):
            return pltpu.make_async_copy(ys_ref.at[pl.ds(dst, rows)], slots_scr.at[buf, pl.ds(src, rows)],
                                         sems.at[buf])
        return make_copy

    def fetch(tile, buf):
        _run_copies(cp_ref, to_ref, dr_ref, tile, n_experts, copy_into(buf))
        used = to_ref[tile * n_experts + n_experts - 1] + cp_ref[tile * n_experts + n_experts - 1]

        def zero_one(j, c):
            slots_scr[buf, pl.ds(pl.multiple_of(used + j * SLOT_ALIGN, SLOT_ALIGN), SLOT_ALIGN), :] = jnp.zeros(
                (SLOT_ALIGN, d_model), BF16)
            return c

        lax.fori_loop(0, (n_slots - used) // SLOT_ALIGN, zero_one, 0)

    @pl.when(t == 0)
    def _():
        fetch(t, cur)

    @pl.when(t + 1 < nt)
    def _():
        fetch(t + 1, 1 - cur)

    _wait_rows(_run_rows(cp_ref, t, n_experts), copy_into(cur))

    sl = slotT_ref[...]
    wt = wT_ref[...]
    acc = hres_ref[...]
    for c in range(n_slots // chunk):
        s_iota = c * chunk + lax.broadcasted_iota(jnp.int32, (tm, chunk), 1)
        pw = jnp.where(sl[:, 0:1] == s_iota, wt[:, 0:1], 0.0)
        for k in range(1, top_k):
            pw = pw + jnp.where(sl[:, k:k + 1] == s_iota, wt[:, k:k + 1], 0.0)
        acc = acc + jnp.dot(pw.astype(BF16), slots_scr[cur, c * chunk:(c + 1) * chunk, :],
                            preferred_element_type=F32)
    y = _rms(acc, fg_ref[...])

    @pl.when(t < n_prompt_tiles)
    def _():
        yp_ref[...] = y

    @pl.when(t >= n_prompt_tiles)
    def _():
        ysm_ref[...] = y[:ysm_ref.shape[0], :]


def _combine(cp, to, dr, hres_all, slotT, wT, fg, y_sorted, *, tm, n_experts, n_slots, n_prompt_tiles, n_sample):
    t_pad, d_model = hres_all.shape
    npt = n_prompt_tiles
    grid_spec = pltpu.PrefetchScalarGridSpec(
        num_scalar_prefetch=3,
        grid=(t_pad // tm,),
        in_specs=[pl.BlockSpec((tm, d_model), lambda i, *_: (i, 0)),
                  pl.BlockSpec((tm, 8), lambda i, *_: (i, 0)),
                  pl.BlockSpec((tm, 8), lambda i, *_: (i, 0)),
                  pl.BlockSpec(fg.shape, lambda i, *_: (0, 0)),
                  pl.BlockSpec(memory_space=pl.ANY)],
        out_specs=[pl.BlockSpec((tm, d_model), lambda i, *_: (jnp.minimum(i, npt - 1), 0)),
                   pl.BlockSpec((n_sample, d_model), lambda i, *_: (0, 0))],
        scratch_shapes=[pltpu.VMEM((2, n_slots, d_model), BF16), pltpu.SemaphoreType.DMA((2,))],
    )
    return pl.pallas_call(
        functools.partial(_combine_kernel, n_experts=n_experts, n_slots=n_slots, chunk=256, top_k=TOP_K,
                          n_prompt_tiles=npt),
        grid_spec=grid_spec,
        out_shape=[jax.ShapeDtypeStruct((npt * tm, d_model), F32), jax.ShapeDtypeStruct((n_sample, d_model), F32)],
        compiler_params=_cparams("arbitrary"),
        name="moe_combine",
    )(cp, to, dr, hres_all, slotT, wT, fg, y_sorted)


def _route_tables(cnt, *, tme, n_row_tiles):
    n_tiles, n_experts = cnt.shape
    cnt_pad = (cnt + (SLOT_ALIGN - 1)) // SLOT_ALIGN * SLOT_ALIGN
    tile_off = jnp.cumsum(cnt_pad, axis=1) - cnt_pad
    tot = jnp.sum(cnt_pad, axis=0)
    tot_t = (tot + (tme - 1)) // tme * tme
    ends = jnp.cumsum(tot_t)
    base = ends - tot_t
    dst_row = base[None, :] + jnp.cumsum(cnt_pad, axis=0) - cnt_pad
    j = jnp.arange(n_row_tiles, dtype=jnp.int32)
    te = jnp.minimum(jnp.sum(((ends // tme)[None, :] <= j[:, None]).astype(jnp.int32), axis=1), n_experts - 1)
    n_used = ends[-1] // tme
    run_end = jnp.sum(jnp.where(te[:, None] == jnp.arange(n_experts)[None, :], (base + tot)[None, :], 0), axis=1)
    tv = jnp.where(j < n_used, jnp.clip(run_end - j * tme, 0, tme), 0)
    xi = jnp.minimum(j, n_used - 1)
    tail_start = base + tot
    tail_n = jnp.concatenate([(tot_t - tot) // SLOT_ALIGN, n_used[None]])
    flat = lambda a: a.reshape(-1).astype(jnp.int32)
    return (flat(cnt_pad), flat(tile_off), flat(dst_row), flat(tail_start), flat(tail_n),
            flat(te), flat(tv), flat(xi))


def kernel(x_prompt, x_sample, cache_k, cache_v, state_ret, page_table, norm1_g, w_in, ret_norm_g, w_out,
           norm2_g, router_w, router_b, w_gate_up, b_gate_up, w_down, b_down, final_norm_g):
    bp, lp, d_model = x_prompt.shape
    bs, ls, _ = x_sample.shape
    depth = w_in.shape[0]
    assert depth == 1, "single-layer step"
    n_pages = page_table.shape[1]
    page_size, mh, hd = cache_k.shape[2], cache_k.shape[3], cache_k.shape[4]
    rh, dk, dv = state_ret.shape[2], state_ret.shape[3], state_ret.shape[4]
    d_ret = rh * dk
    d_moba = mh * hd
    n_experts = router_w.shape[2]
    d_ff = w_down.shape[2]
    past_len = n_pages * page_size
    layer = 0

    w = w_in[layer]
    w_all = w.astype(BF16)
    g1 = norm1_g[layer][None, :]
    wo = w_out[layer].astype(BF16)
    rng = ret_norm_g[layer][None, :]
    n2 = norm2_g[layer][None, :]
    rwT = router_w[layer].T.astype(BF16)
    rb = router_b[layer][:, None]
    wgu = w_gate_up[layer]
    bgu_perm = b_gate_up[layer].reshape(n_experts, -1, LANES, 2).transpose(0, 1, 3, 2).reshape(n_experts, 1, 2 * d_ff)
    wd = w_down[layer]
    bd = b_down[layer][:, None, :]
    fg = final_norm_g[None, :]

    tm = _token_tile(lp)
    pos_p = jnp.arange(lp, dtype=jnp.int32)
    cos_p, sin_p = _rope_tables(pos_p, dk)
    xp = x_prompt.reshape(bp * lp, d_model)
    assert dk == dv
    ret, mq, mv, kT, vT, s_p = _inproj_prompt(
        xp, g1, w_all, cos_p, sin_p, rng,
        batch=bp, seq=lp, d_ret=d_ret, d_moba=d_moba, n_ret_heads=rh, tm=tm)
    ckT = cache_k[layer].transpose(0, 2, 3, 1)
    cvT = cache_v[layer].transpose(0, 2, 3, 1)
    moba_o, ksum = _moba_prompt(mq, mv, kT, batch=bp, seq=lp, n_heads=mh, hd=hd, page_table=page_table, cacheT=ckT)
    n_sample = bs * ls
    assert n_sample <= tm, "sample group must fit one token tile"
    npt = bp * lp // tm
    n_tiles = npt + 1
    bufs = _merge(xp, ret, moba_o, wo, n2, rwT, rb, None,
                  tm=tm, n_tiles_total=n_tiles, tile0=0, n_valid=tm, n_clear_tiles=1)

    pos_s = past_len + jnp.arange(ls, dtype=jnp.int32)
    cos_s, sin_s = _rope_tables(jnp.tile(pos_s, bs), dk)
    xs = x_sample.reshape(bs * ls, d_model)
    rq_s, rk_s, rv_s, rg_s, mq_s, mk_s, mv_s = _inproj_sample(
        xs, g1, w_all, cos_s, sin_s, d_ret=d_ret, d_moba=d_moba, n_ret_heads=rh)
    r3 = lambda a: a.reshape(bs, ls, a.shape[1])
    ret_s, s_s = _ret_sample(r3(rq_s), r3(rk_s), r3(rv_s), r3(rg_s), rng, state_ret[layer],
                             n_heads=rh, dk=dk, dv=dv)
    if ksum is None:
        idx_pad = _moba_gate(page_table, r3(mq_s), ckT, n_heads=mh, hd=hd, ls=ls, pages_per_step=min(32, n_pages))
    else:
        idx_pad = _moba_gate_top(ksum, r3(mq_s), n_heads=mh, hd=hd, n_past=n_pages // (MOBA_BLOCK // page_size))
    idx = idx_pad[:, :, :MOBA_TOPK].reshape(-1)
    cols = lambda a: a.reshape(bs, ls, mh, hd).transpose(0, 2, 3, 1)
    o_cols = _moba_sample(page_table, idx, cols(mq_s), cols(mk_s), cols(mv_s), ckT, cvT,
                          n_heads=mh, hd=hd, ls=ls)
    moba_o_s = o_cols.transpose(0, 3, 1, 2).reshape(bs * ls, d_moba).astype(BF16)
    padt = lambda a: jnp.pad(a, ((0, tm - n_sample), (0, 0)))
    hres_all, xn_all, slot, wts, cnt = _merge(
        padt(xs), padt(ret_s.reshape(n_sample, d_ret).astype(BF16)), padt(moba_o_s), wo, n2, rwT, rb, bufs,
        tm=tm, n_tiles_total=n_tiles, tile0=npt, n_valid=n_sample)

    tme = 512
    n_slots = -(-(tm * TOP_K + n_experts * (SLOT_ALIGN - 1)) // 256) * 256
    n_assign = TOP_K * (bp * lp + n_sample)
    n_row_tiles = -(-(n_assign + (SLOT_ALIGN - 1) * n_experts * n_tiles + n_experts * (tme - 1)) // tme)
    cp, to, dr, ts, tn, te, tv, xi = _route_tables(cnt[:, :, 0], tme=tme, n_row_tiles=n_row_tiles)
    x_sorted = _dispatch(cp, to, dr, ts, tn, xn_all, slot, tm=tm, tme=tme, n_experts=n_experts, n_slots=n_slots,
                         p_rows=n_row_tiles * tme)
    y_sorted = _experts(te, tv, xi, x_sorted, wgu, bgu_perm, wd, bd, tme=tme)
    y_p, y_s = _combine(cp, to, dr, hres_all, slot.T, wts.T, fg, y_sorted, tm=tm, n_experts=n_experts,
                        n_slots=n_slots, n_prompt_tiles=npt, n_sample=n_sample)

    y_prompt = y_p.reshape(bp, lp, d_model)
    y_sample = y_s.reshape(bs, ls, d_model)
    k_prompt = kT.reshape(bp, mh, hd, lp).transpose(0, 3, 1, 2)[None]
    v_prompt = vT.reshape(bp, mh, hd, lp).transpose(0, 3, 1, 2)[None]
    k_sample = mk_s.reshape(1, bs, ls, mh, hd)
    v_sample = mv_s.reshape(1, bs, ls, mh, hd)
    return (y_prompt, y_sample, k_prompt, v_prompt, s_p[None], k_sample, v_sample, s_s[None])
```
